```python
import math
import jax, jax.numpy as jnp
from jax import lax
import numpy as np

D_MODEL = 1024
BATCH = 16
SEQ = 2048
DEPTH = 2
DEC_BATCH = 128
DEC_SEQ = 4
PAST_LEN = 16384
PAGE_SIZE = 128

HEAD_DIM = 64
N_MIXERS = 4
GROUP_WIDTH = D_MODEL // N_MIXERS
MIX_WIDTH = N_MIXERS * GROUP_WIDTH
NORM_EPS = 1e-6
RW_WIDTH = GROUP_WIDTH
RW_HEADS = RW_WIDTH // HEAD_DIM
RW_DECAY_RANK = 64
RW_ICLR_RANK = 64
RW_GATE_RANK = 128
RW_LN_EPS = 64e-5
RW_SPLITS = (RW_WIDTH, RW_WIDTH, RW_WIDTH, RW_DECAY_RANK, RW_ICLR_RANK, RW_GATE_RANK)
RW_PROJ = sum(RW_SPLITS)
S5_WIDTH = GROUP_WIDTH
S5_CH = 16
S5_GROUPS = S5_WIDTH // S5_CH
S5_STATE = 64
GDN_WIDTH = GROUP_WIDTH
GDN_HEADS = GDN_WIDTH // HEAD_DIM
GDN_CONV = 4
GDN_CHUNK = 64
GDN_CONV_CH = 3 * GDN_WIDTH
GDN_SPLITS = (GDN_CONV_CH, GDN_HEADS, GDN_HEADS, GDN_WIDTH)
GDN_PROJ = sum(GDN_SPLITS)
SWA_WIDTH = GROUP_WIDTH
SWA_HEADS = SWA_WIDTH // HEAD_DIM
SWA_KV_HEADS = max(1, SWA_HEADS // 2)
SWA_GROUP = SWA_HEADS // SWA_KV_HEADS
WINDOW = 128
SWA_SPLITS = (SWA_WIDTH, SWA_KV_HEADS * HEAD_DIM, SWA_KV_HEADS * HEAD_DIM)
SWA_PROJ = sum(SWA_SPLITS)
ROPE_DIM = HEAD_DIM // 4
ROPE_THETA = 500000.0
PROJ_SPLITS = (RW_PROJ, S5_WIDTH, GDN_PROJ, SWA_PROJ)
PROJ_WIDTH = sum(PROJ_SPLITS)
D_FF = 4 * D_MODEL
N_STATE_KINDS = 8

kernel_name = 'hybrid_rwkv7_s5_gdn_swa_decode_step'


def split_cols(x, sizes):
    idx = np.cumsum(sizes)[:-1].tolist()
    return jnp.split(x, idx, axis=-1)


def rms_norm(x, g):
    x32 = x.astype(jnp.float32)
    return x32 * lax.rsqrt(jnp.mean(x32 * x32, axis=-1, keepdims=True) + NORM_EPS) * g


def l2_normalize(x):
    return x * lax.rsqrt(jnp.maximum(jnp.sum(x * x, axis=-1, keepdims=True), 1e-12))


def causal_dwconv(x, buf, w):
    width = w.shape[0]
    L = x.shape[1]
    xp = jnp.concatenate([buf, x], axis=1)
    y = sum(xp[:, i:i + L] * w[i] for i in range(width))
    return y, xp[:, L:]


def partial_rope(x, pos):
    half = ROPE_DIM // 2
    inv = ROPE_THETA ** (-jnp.arange(0, ROPE_DIM, 2, dtype=jnp.float32) / ROPE_DIM)
    ang = pos.astype(jnp.float32)[:, None] * inv[None, :]
    cos = jnp.cos(ang)[None, :, None, :]
    sin = jnp.sin(ang)[None, :, None, :]
    x1 = x[..., :half]
    x2 = x[..., half:ROPE_DIM]
    return jnp.concatenate([x1 * cos - x2 * sin, x2 * cos + x1 * sin, x[..., ROPE_DIM:]], axis=-1)


def rwkv7_mixer(p, shift_buf, S0, prm):
    B, L, _ = p.shape
    prev = jnp.concatenate([shift_buf, p[:, :-1]], axis=1)
    xs = p + (prev - p) * prm['rw_mu']
    r, k, v, wd, ad, gd = split_cols(xs, RW_SPLITS)
    z_w = prm['rw_w0'] + jnp.tanh(wd) @ prm['rw_w2']
    w = jnp.exp(-jnp.exp(-jax.nn.softplus(-z_w) - 0.5))
    a = jax.nn.sigmoid(prm['rw_a0'] + ad @ prm['rw_a2'])
    g = jax.nn.sigmoid(gd) @ prm['rw_g2']
    heads = lambda t: t.reshape(B, L, RW_HEADS, HEAD_DIM)
    kk = l2_normalize(heads(k * prm['rw_kk']))
    k = k * (1.0 + (a - 1.0) * prm['rw_ka'])
    r, w, k, v, a = map(heads, (r, w, k, v, a))

    def step(S, inp):
        r_t, w_t, k_t, v_t, kk_t, a_t = inp
        sa = jnp.einsum('bhvk,bhk->bhv', S, -kk_t)
        S = (S * w_t[:, :, None, :] + sa[..., None] * (kk_t * a_t)[:, :, None, :]
             + v_t[..., None] * k_t[:, :, None, :])
        return S, jnp.einsum('bhvk,bhk->bhv', S, r_t)

    seq = tuple(jnp.moveaxis(t, 1, 0) for t in (r, w, k, v, kk, a))
    S, y = lax.scan(step, S0, seq)
    y = jnp.moveaxis(y, 0, 1)
    mu = jnp.mean(y, axis=-1, keepdims=True)
    var = jnp.mean(jnp.square(y - mu), axis=-1, keepdims=True)
    yn = ((y - mu) * lax.rsqrt(var + RW_LN_EPS)).reshape(B, L, RW_WIDTH) * prm['rw_ln_w'] + prm['rw_ln_b']
    bonus = (jnp.sum(r * k * prm['rw_rk'], axis=-1, keepdims=True) * v).reshape(B, L, RW_WIDTH)
    return (yn + bonus) * g, p[:, -1:], S


def _complex_affine_combine(e1, e2):
    a1r, a1i, b1r, b1i = e1
    a2r, a2i, b2r, b2i = e2
    return (a2r * a1r - a2i * a1i, a2r * a1i + a2i * a1r,
            a2r * b1r - a2i * b1i + b2r, a2r * b1i + a2i * b1r + b2i)


def s5_mixer(u, x0_re, x0_im, prm):
    B, L, _ = u.shape
    ug = u.reshape(B, L, S5_GROUPS, S5_CH)
    a_re, a_im = prm['s5_a_re'], prm['s5_a_im']
    dt = jnp.exp(prm['s5_log_dt'])[:, None]
    mag = jnp.exp(dt * a_re)
    ab_re = mag * jnp.cos(dt * a_im)
    ab_im = mag * jnp.sin(dt * a_im)
    den = a_re * a_re + a_im * a_im
    nr = ab_re - 1.0
    cf_re = (nr * a_re + ab_im * a_im) / den
    cf_im = (ab_im * a_re - nr * a_im) / den
    bb_re = cf_re[..., None] * prm['s5_b_re'] - cf_im[..., None] * prm['s5_b_im']
    bb_im = cf_re[..., None] * prm['s5_b_im'] + cf_im[..., None] * prm['s5_b_re']
    bu_re = jnp.einsum('blgc,gnc->blgn', ug, bb_re)
    bu_im = jnp.einsum('blgc,gnc->blgn', ug, bb_im)
    elems = (jnp.broadcast_to(ab_re, bu_re.shape), jnp.broadcast_to(ab_im, bu_im.shape), bu_re, bu_im)
    p_re, p_im, h_re, h_im = lax.associative_scan(_complex_affine_combine, elems, axis=1)
    h_re = h_re + p_re * x0_re[:, None] - p_im * x0_im[:, None]
    h_im = h_im + p_re * x0_im[:, None] + p_im * x0_re[:, None]
    y = (jnp.einsum('blgn,gcn->blgc', h_re, prm['s5_c_re'])
         - jnp.einsum('blgn,gcn->blgc', h_im, prm['s5_c_im']))
    y = y.reshape(B, L, S5_WIDTH) + prm['s5_d'] * u
    z = jax.nn.gelu(y)
    out = z * jax.nn.sigmoid(z @ prm['s5_w_glu'] + prm['s5_b_glu'])
    return out, h_re[:, -1], h_im[:, -1]


def gated_delta_chunked(q, k, v, beta, g, S0):
    B, L, H, D = q.shape
    C = GDN_CHUNK if L % GDN_CHUNK == 0 else L
    n = L // C

    def chunks(t):
        t = t.reshape((B, n, C) + t.shape[2:])
        return jnp.moveaxis(jnp.moveaxis(t, 1, 0), 3, 2)

    incl = jnp.tril(jnp.ones((C, C), dtype=bool))
    strict = jnp.tril(jnp.ones((C, C), dtype=bool), -1)
    eye = jnp.eye(C, dtype=jnp.float32)

    def step(S, inp):
        qc, kc, vc, bc, gc = inp
        G = jnp.cumsum(gc, axis=-1)
        decay = jnp.exp(jnp.where(incl, G[..., :, None] - G[..., None, :], -jnp.inf))
        A = jnp.where(strict, bc[..., :, None] * decay * jnp.einsum('bhtd,bhjd->bhtj', kc, kc), 0.0)
        gam = jnp.exp(G)[..., None]
        rhs = bc[..., None] * (vc - gam * jnp.einsum('bhvk,bhtk->bhtv', S, kc))
        U = lax.linalg.triangular_solve(eye + A, rhs, left_side=True, lower=True, unit_diagonal=True)
        o = (gam * jnp.einsum('bhvk,bhtk->bhtv', S, qc)
             + jnp.einsum('bhtj,bhjv->bhtv', jnp.einsum('bhtd,bhjd->bhtj', qc, kc) * decay, U))
        G_end = G[..., -1:]
        S_new = (jnp.exp(G_end)[..., None] * S
                 + jnp.einsum('bhjv,bhjk->bhvk', U * jnp.exp(G_end - G)[..., None], kc))
        return S_new, o

    S, o = lax.scan(step, S0, tuple(map(chunks, (q, k, v, beta, g))))
    o = jnp.swapaxes(jnp.moveaxis(o, 0, 1), 2, 3).reshape(B, L, H, D)
    return o, S


def gdn_mixer(p, conv_buf, S0, prm):
    B, L, _ = p.shape
    qkv, b_in, a_in, z = split_cols(p, GDN_SPLITS)
    qkv, new_buf = causal_dwconv(qkv, conv_buf, prm['gdn_conv_w'])
    q, k, v = split_cols(jax.nn.silu(qkv), (GDN_WIDTH, GDN_WIDTH, GDN_WIDTH))
    heads = lambda t: t.reshape(B, L, GDN_HEADS, HEAD_DIM)
    q = l2_normalize(heads(q)) * HEAD_DIM ** -0.5
    k = l2_normalize(heads(k))
    beta = jax.nn.sigmoid(b_in)
    g = -jnp.exp(prm['gdn_a_log']) * jax.nn.softplus(a_in + prm['gdn_dt_bias'])
    o, S = gated_delta_chunked(q, k, heads(v), beta, g, S0)
    o = rms_norm(o, prm['gdn_norm_w']) * jax.nn.silu(heads(z))
    return o.reshape(B, L, GDN_WIDTH), new_buf, S


def swa_mixer(p, k_buf, v_buf, start, sinks):
    B, L, _ = p.shape
    q, k, v = split_cols(p, SWA_SPLITS)
    pos = start + jnp.arange(L)
    q = partial_rope(q.reshape(B, L, SWA_HEADS, HEAD_DIM), pos)
    k = partial_rope(k.reshape(B, L, SWA_KV_HEADS, HEAD_DIM), pos)
    v = v.reshape(B, L, SWA_KV_HEADS, HEAD_DIM)
    kc = jnp.concatenate([k_buf, k], axis=1)
    vc = jnp.concatenate([v_buf, v], axis=1)
    Wb = k_buf.shape[1]
    Qb = WINDOW if L % WINDOW == 0 else L
    nb = L // Qb
    idx = (jnp.arange(nb) * Qb)[:, None] + jnp.arange(Wb + Qb)[None, :]
    kb = kc[:, idx]
    vb = vc[:, idx]
    qb = q.reshape(B, nb, Qb, SWA_KV_HEADS, SWA_GROUP, HEAD_DIM)
    s = jnp.einsum('bnqhgd,bnkhd->bnhgqk', qb, kb) * HEAD_DIM ** -0.5
    qpos = start + jnp.arange(L).reshape(nb, Qb)
    kpos = start - Wb + idx
    rel = qpos[:, :, None] - kpos[:, None, :]
    mask = (rel >= 0) & (rel < WINDOW) & (kpos[:, None, :] >= 0)
    s = jnp.where(mask[None, :, None, None], s, -jnp.inf)
    sink = jnp.broadcast_to(sinks.reshape(1, 1, SWA_KV_HEADS, SWA_GROUP, 1, 1), s.shape[:-1] + (1,))
    pr = jax.nn.softmax(jnp.concatenate([s, sink], axis=-1), axis=-1)[..., :-1]
    o = jnp.einsum('bnhgqk,bnkhd->bnqhgd', pr, vb).reshape(B, L, SWA_WIDTH)
    return o, kc[:, -Wb:], vc[:, -Wb:]


def decoder_layer(x, start, st, prm):
    rw_S, rw_shift, s5_re, s5_im, gdn_S, gdn_conv, swa_k, swa_v = st
    h = rms_norm(x, prm['g_mix_pre'])
    p_rw, p_s5, p_gdn, p_swa = split_cols(h @ prm['w_in'], PROJ_SPLITS)
    o_rw, rw_shift, rw_S = rwkv7_mixer(p_rw, rw_shift, rw_S, prm)
    o_s5, s5_re, s5_im = s5_mixer(p_s5, s5_re, s5_im, prm)
    o_gdn, gdn_conv, gdn_S = gdn_mixer(p_gdn, gdn_conv, gdn_S, prm)
    o_swa, swa_k, swa_v = swa_mixer(p_swa, swa_k, swa_v, start, prm['swa_sinks'])
    mix = jnp.concatenate([o_rw, o_s5, o_gdn, o_swa], axis=-1) @ prm['w_out']
    x = x + rms_norm(mix, prm['g_mix_post'])
    h = rms_norm(x, prm['g_mlp_pre'])
    f = jnp.square(jax.nn.relu(h @ prm['w_up'])) @ prm['w_down']
    x = x + rms_norm(f, prm['g_mlp_post'])
    return x, (rw_S, rw_shift, s5_re, s5_im, gdn_S, gdn_conv, swa_k, swa_v)


def stack_layers(states, i):
    return jnp.stack([st[i] for st in states], axis=0)


def setup_inputs(seed: int = 0) -> dict:
    key = jax.random.key(seed)
    ks = iter(jax.random.split(key, 64))
    nrm = lambda shape, scale: scale * jax.random.normal(next(ks), shape, jnp.float32)
    uni = lambda shape, lo, hi: jax.random.uniform(next(ks), shape, jnp.float32, lo, hi)
    L = DEPTH
    win_buf = min(WINDOW, PAST_LEN)
    gdn_dt = jnp.exp(uni((L, GDN_HEADS), math.log(1e-3), math.log(1e-1)))
    return {
        'x_prompt': nrm((BATCH, SEQ, D_MODEL), 1.0),
        'x_sample': nrm((DEC_BATCH, DEC_SEQ, D_MODEL), 1.0),
        'state_rwkv': nrm((L, DEC_BATCH, RW_HEADS, HEAD_DIM, HEAD_DIM), 0.3),
        'state_rwkv_shift': nrm((L, DEC_BATCH, 1, RW_PROJ), 1.0),
        'state_s5_re': nrm((L, DEC_BATCH, S5_GROUPS, S5_STATE), 0.5),
        'state_s5_im': nrm((L, DEC_BATCH, S5_GROUPS, S5_STATE), 0.5),
        'state_gdn': nrm((L, DEC_BATCH, GDN_HEADS, HEAD_DIM, HEAD_DIM), 0.1),
        'state_gdn_conv': nrm((L, DEC_BATCH, GDN_CONV - 1, GDN_CONV_CH), 1.0),
        'cache_swa_k': nrm((L, DEC_BATCH, win_buf, SWA_KV_HEADS, HEAD_DIM), 1.0),
        'cache_swa_v': nrm((L, DEC_BATCH, win_buf, SWA_KV_HEADS, HEAD_DIM), 1.0),
        'g_mix_pre': 1.0 + nrm((L, D_MODEL), 0.02),
        'g_mix_post': 1.0 + nrm((L, D_MODEL), 0.02),
        'g_mlp_pre': 1.0 + nrm((L, D_MODEL), 0.02),
        'g_mlp_post': 1.0 + nrm((L, D_MODEL), 0.02),
        'w_in': nrm((L, D_MODEL, PROJ_WIDTH), D_MODEL ** -0.5),
        'w_out': nrm((L, MIX_WIDTH, D_MODEL), MIX_WIDTH ** -0.5),
        'rw_mu': uni((L, RW_PROJ), 0.2, 0.8),
        'rw_w0': uni((L, RW_WIDTH), -4.0, 1.0),
        'rw_w2': nrm((L, RW_DECAY_RANK, RW_WIDTH), 0.1),
        'rw_a0': nrm((L, RW_WIDTH), 0.1),
        'rw_a2': nrm((L, RW_ICLR_RANK, RW_WIDTH), 0.5 * RW_ICLR_RANK ** -0.5),
        'rw_g2': nrm((L, RW_GATE_RANK, RW_WIDTH), RW_GATE_RANK ** -0.5),
        'rw_kk': 0.85 + nrm((L, RW_WIDTH), 0.02),
        'rw_ka': 1.0 + nrm((L, RW_WIDTH), 0.02),
        'rw_rk': nrm((L, RW_HEADS, HEAD_DIM), 0.1),
        'rw_ln_w': 1.0 + nrm((L, RW_WIDTH), 0.02),
        'rw_ln_b': nrm((L, RW_WIDTH), 0.02),
        's5_a_re': -0.5 + nrm((L, S5_GROUPS, S5_STATE), 0.01),
        's5_a_im': jnp.broadcast_to(jnp.pi * jnp.arange(S5_STATE, dtype=jnp.float32), (L, S5_GROUPS, S5_STATE))
                   + nrm((L, S5_GROUPS, S5_STATE), 0.01),
        's5_log_dt': uni((L, S5_GROUPS), math.log(1e-3), math.log(1e-1)),
        's5_b_re': nrm((L, S5_GROUPS, S5_STATE, S5_CH), (2 * S5_CH) ** -0.5),
        's5_b_im': nrm((L, S5_GROUPS, S5_STATE, S5_CH), (2 * S5_CH) ** -0.5),
        's5_c_re': nrm((L, S5_GROUPS, S5_CH, S5_STATE), (2 * S5_STATE) ** -0.5),
        's5_c_im': nrm((L, S5_GROUPS, S5_CH, S5_STATE), (2 * S5_STATE) ** -0.5),
        's5_d': nrm((L, S5_WIDTH), 1.0),
        's5_w_glu': nrm((L, S5_WIDTH, S5_WIDTH), S5_WIDTH ** -0.5),
        's5_b_glu': nrm((L, S5_WIDTH), 0.02),
        'gdn_conv_w': nrm((L, GDN_CONV, GDN_CONV_CH), 0.5),
        'gdn_a_log': jnp.log(uni((L, GDN_HEADS), 1.0, 16.0)),
        'gdn_dt_bias': gdn_dt + jnp.log(-jnp.expm1(-gdn_dt)),
        'gdn_norm_w': 1.0 + nrm((L, HEAD_DIM), 0.02),
        'swa_sinks': nrm((L, SWA_HEADS), 1.0),
        'w_up': nrm((L, D_MODEL, D_FF), D_MODEL ** -0.5),
        'w_down': nrm((L, D_FF, D_MODEL), D_FF ** -0.5),
    }


def reference(x_prompt, x_sample, state_rwkv, state_rwkv_shift, state_s5_re, state_s5_im, state_gdn,
              state_gdn_conv, cache_swa_k, cache_swa_v, g_mix_pre, g_mix_post, g_mlp_pre, g_mlp_post,
              w_in, w_out, rw_mu, rw_w0, rw_w2, rw_a0, rw_a2, rw_g2, rw_kk, rw_ka, rw_rk, rw_ln_w, rw_ln_b,
              s5_a_re, s5_a_im, s5_log_dt, s5_b_re, s5_b_im, s5_c_re, s5_c_im, s5_d, s5_w_glu, s5_b_glu,
              gdn_conv_w, gdn_a_log, gdn_dt_bias, gdn_norm_w, swa_sinks, w_up, w_down):
    f32 = jnp.float32
    xp = x_prompt.astype(f32)
    xs = x_sample.astype(f32)
    zp = lambda *shape: jnp.zeros((BATCH,) + shape, f32)
    prompt_init = (zp(RW_HEADS, HEAD_DIM, HEAD_DIM), zp(1, RW_PROJ), zp(S5_GROUPS, S5_STATE),
                   zp(S5_GROUPS, S5_STATE), zp(GDN_HEADS, HEAD_DIM, HEAD_DIM), zp(GDN_CONV - 1, GDN_CONV_CH),
                   zp(WINDOW, SWA_KV_HEADS, HEAD_DIM), zp(WINDOW, SWA_KV_HEADS, HEAD_DIM))
    new_p, new_s = [], []
    for l in range(DEPTH):
        prm = {n: a[l].astype(f32) for n, a in (
            ('g_mix_pre', g_mix_pre), ('g_mix_post', g_mix_post), ('g_mlp_pre', g_mlp_pre),
            ('g_mlp_post', g_mlp_post), ('w_in', w_in), ('w_out', w_out), ('rw_mu', rw_mu), ('rw_w0', rw_w0),
            ('rw_w2', rw_w2), ('rw_a0', rw_a0), ('rw_a2', rw_a2), ('rw_g2', rw_g2), ('rw_kk', rw_kk),
            ('rw_ka', rw_ka), ('rw_rk', rw_rk), ('rw_ln_w', rw_ln_w), ('rw_ln_b', rw_ln_b),
            ('s5_a_re', s5_a_re), ('s5_a_im', s5_a_im), ('s5_log_dt', s5_log_dt), ('s5_b_re', s5_b_re),
            ('s5_b_im', s5_b_im), ('s5_c_re', s5_c_re), ('s5_c_im', s5_c_im), ('s5_d', s5_d),
            ('s5_w_glu', s5_w_glu), ('s5_b_glu', s5_b_glu), ('gdn_conv_w', gdn_conv_w),
            ('gdn_a_log', gdn_a_log), ('gdn_dt_bias', gdn_dt_bias), ('gdn_norm_w', gdn_norm_w),
            ('swa_sinks', swa_sinks), ('w_up', w_up), ('w_down', w_down))}
        xp, st_p = decoder_layer(xp, 0, prompt_init, prm)
        st_in = tuple(a[l].astype(f32) for a in (state_rwkv, state_rwkv_shift, state_s5_re, state_s5_im,
                                                 state_gdn, state_gdn_conv, cache_swa_k, cache_swa_v))
        xs, st_s = decoder_layer(xs, PAST_LEN, st_in, prm)
        new_p.append(st_p)
        new_s.append(st_s)
    y_prompt = xp.astype(x_prompt.dtype)
    y_sample = xs.astype(x_sample.dtype)
    return (y_prompt, y_sample,
            stack_layers(new_p, 0), stack_layers(new_s, 0),
            stack_layers(new_p, 1), stack_layers(new_s, 1),
            stack_layers(new_p, 2), stack_layers(new_s, 2),
            stack_layers(new_p, 3), stack_layers(new_s, 3),
            stack_layers(new_p, 4), stack_layers(new_s, 4),
            stack_layers(new_p, 5), stack_layers(new_s, 5),
            stack_layers(new_p, 6), stack_layers(new_s, 6),
            stack_layers(new_p, 7), stack_layers(new_s, 7))
```

```python
import functools
import math

import jax
import jax.numpy as jnp
from jax import lax
from jax.experimental import pallas as pl
from jax.experimental.pallas import tpu as pltpu

F32 = jnp.float32
BF16 = jnp.bfloat16

D_MODEL = 1024
HEAD_DIM = 64
N_HEADS = 4
GROUP_WIDTH = N_HEADS * HEAD_DIM
NORM_EPS = 1e-6
RW_LN_EPS = 64e-5
RW_PROJ = 1024
RW_LORA_OFF = 3 * GROUP_WIDTH
RW_DECAY_RANK, RW_ICLR_RANK, RW_GATE_RANK = 64, 64, 128
S5_GROUPS, S5_CH, S5_STATE = 16, 16, 64
S5_LANES = S5_GROUPS * S5_STATE
GDN_CONV = 4
GDN_QKV = 3 * GROUP_WIDTH
SWA_KV_HEADS = 2
SWA_KV_WIDTH = SWA_KV_HEADS * HEAD_DIM
SWA_PROJ = GROUP_WIDTH + 2 * SWA_KV_WIDTH
WINDOW = 128
ROPE_DIM = 16
ROPE_THETA = 500000.0
D_FF = 4096
LANES = 128
SUBLANES = 8
CHUNK = 64
TRI_BLOCK = 16
VMEM_LIMIT = 56 * 1024 * 1024

PROJ_WIDTHS = (RW_PROJ, GROUP_WIDTH, GDN_QKV, GROUP_WIDTH, LANES, SWA_PROJ)


def _mm(a, b, precision=None):
    return jnp.dot(a, b, preferred_element_type=F32, precision=precision)


def _mm_nt(a, b, precision=None):
    return lax.dot_general(a, b, (((1,), (1,)), ((), ())), preferred_element_type=F32, precision=precision)


def _mm_tn(a, b, precision=None):
    return lax.dot_general(a, b, (((0,), (0,)), ((), ())), preferred_element_type=F32, precision=precision)


def _split3(x):
    hi = x.astype(BF16)
    r1 = x - hi.astype(F32)
    mid = r1.astype(BF16)
    lo = (r1 - mid.astype(F32)).astype(BF16)
    return hi, mid, lo


def _mm_exact_rhs(x, m):
    hi, mid, lo = _split3(x)
    return _mm(hi, m) + _mm(mid, m) + _mm(lo, m)


def _mm_exact_lhs(m, x):
    hi, mid, lo = _split3(x)
    return _mm(m, hi) + _mm(m, mid) + _mm(m, lo)


def _head_ones(width):
    r = lax.broadcasted_iota(jnp.int32, (width, width), 0)
    c = lax.broadcasted_iota(jnp.int32, (width, width), 1)
    return (r // HEAD_DIM == c // HEAD_DIM).astype(BF16)


def _head_sum(x, ones):
    return _mm_exact_rhs(x, ones)


def _tri_masks(n):
    r = lax.broadcasted_iota(jnp.int32, (n, n), 0)
    c = lax.broadcasted_iota(jnp.int32, (n, n), 1)
    return r, c, r > c, r >= c


def _softplus(x):
    return jnp.maximum(x, 0.0) + jnp.log1p(jnp.exp(-jnp.abs(x)))


def _silu(x):
    return x * jax.nn.sigmoid(x)


def _rms(x, g):
    return x * lax.rsqrt(jnp.mean(x * x, axis=-1, keepdims=True) + NORM_EPS) * g


def _tri_inv(low, n, precision):
    r, c, _, _ = _tri_masks(n)
    eye = (r == c).astype(F32)
    bs = min(TRI_BLOCK, n)
    if n > bs:
        same = (r // bs) == (c // bs)
        diag = jnp.where(same, low, 0.0)
        rest = low - diag
    else:
        diag, rest = low, None
    inv = eye - diag
    pw, p = diag, 1
    while 2 * p < bs:
        pw = _mm(pw, pw, precision)
        inv = inv + _mm(inv, pw, precision)
        p *= 2
    if rest is None:
        return inv
    nil = _mm(inv, rest, precision)
    out = eye - nil
    pw, p = nil, 1
    while 2 * p < n // bs:
        pw = _mm(pw, pw, precision)
        out = out + _mm(out, pw, precision)
        p *= 2
    return _mm(out, inv, precision)


def _const_spec(shape):
    zeros = (0,) * len(shape)
    return pl.BlockSpec(shape, lambda *_: zeros, pipeline_mode=pl.Buffered(1))


def _in_proj_kernel(x_ref, g_ref, w_ref, *out_refs):
    h = _rms(x_ref[...], g_ref[...]).astype(BF16)
    off = 0
    for o_ref, width in zip(out_refs, PROJ_WIDTHS):
        o_ref[...] = _mm(h, w_ref[:, off:off + width])
        off += width


def _in_proj(x2d, g, w_all, tm):
    t = x2d.shape[0]
    wtot = sum(PROJ_WIDTHS)
    return pl.pallas_call(
        _in_proj_kernel,
        grid=(t // tm,),
        in_specs=[pl.BlockSpec((tm, D_MODEL), lambda i: (i, 0)),
                  _const_spec((1, D_MODEL)),
                  _const_spec((D_MODEL, wtot))],
        out_specs=[pl.BlockSpec((tm, w), lambda i: (i, 0)) for w in PROJ_WIDTHS],
        out_shape=[jax.ShapeDtypeStruct((t, w), F32) for w in PROJ_WIDTHS],
        compiler_params=pltpu.CompilerParams(dimension_semantics=("arbitrary",), vmem_limit_bytes=VMEM_LIMIT),
        name="in_proj",
    )(x2d, g, w_all)


FF_CHUNK = 1024


def _post_kernel(x_ref, o0_ref, o1_ref, o2_ref, o3_ref, wout_ref, gpost_ref, gpre_ref, wup_ref, wdn_ref,
                 gmlp_ref, y_ref):
    mix = None
    for i, o_ref in enumerate((o0_ref, o1_ref, o2_ref, o3_ref)):
        part = _mm(o_ref[...].astype(BF16), wout_ref[i * GROUP_WIDTH:(i + 1) * GROUP_WIDTH, :])
        mix = part if mix is None else mix + part
    x1 = x_ref[...] + _rms(mix, gpost_ref[...])
    h = _rms(x1, gpre_ref[...]).astype(BF16)
    acc = None
    for c in range(D_FF // FF_CHUNK):
        u = _mm(h, wup_ref[:, c * FF_CHUNK:(c + 1) * FF_CHUNK])
        u = jnp.square(jnp.maximum(u, 0.0)).astype(BF16)
        part = _mm(u, wdn_ref[c * FF_CHUNK:(c + 1) * FF_CHUNK, :])
        acc = part if acc is None else acc + part
    y_ref[...] = x1 + _rms(acc, gmlp_ref[...])


def _post(x2d, outs, w_out, g_post, g_pre, w_up, w_dn, g_mlp, tm):
    t = x2d.shape[0]
    row = lambda w: pl.BlockSpec((tm, w), lambda i: (i, 0))
    return pl.pallas_call(
        _post_kernel,
        grid=(t // tm,),
        in_specs=[row(D_MODEL)] + [row(GROUP_WIDTH)] * 4 + [
            _const_spec((D_MODEL, D_MODEL)), _const_spec((1, D_MODEL)), _const_spec((1, D_MODEL)),
            _const_spec((D_MODEL, D_FF)), _const_spec((D_FF, D_MODEL)), _const_spec((1, D_MODEL))],
        out_specs=row(D_MODEL),
        out_shape=jax.ShapeDtypeStruct((t, D_MODEL), F32),
        compiler_params=pltpu.CompilerParams(dimension_semantics=("arbitrary",), vmem_limit_bytes=VMEM_LIMIT),
        name="post_mlp",
    )(x2d, *outs, w_out, g_post, g_pre, w_up, w_dn, g_mlp)


def _rwkv_kernel(p_ref, shift_ref, s0_ref, mu_ref, w0_ref, a0_ref, kkp_ref, kap_ref, rk_ref, lnw_ref, lnb_ref,
                 w2_ref, a2_ref, g2_ref, o_ref, s_out_ref, s_scr, carry_scr, *, chunk, n_valid, tri_precision):
    c_idx = pl.program_id(1)

    @pl.when(c_idx == 0)
    def _():
        s_scr[...] = s0_ref[0]
        carry_scr[...] = shift_ref[0]

    p = p_ref[0]
    row = lax.broadcasted_iota(jnp.int32, (chunk, 1), 0)
    prev = jnp.where(row == 0, carry_scr[...], pltpu.roll(p, 1, 0))
    carry_scr[...] = p[n_valid - 1:n_valid, :]
    xs = p + (prev - p) * mu_ref[...]
    r = xs[:, 0:GROUP_WIDTH]
    k = xs[:, GROUP_WIDTH:2 * GROUP_WIDTH]
    v = xs[:, 2 * GROUP_WIDTH:3 * GROUP_WIDTH]
    lora = xs[:, RW_LORA_OFF:RW_PROJ]

    z_w = w0_ref[...] + _mm(jnp.tanh(lora).astype(BF16), w2_ref[...])
    logw = -jnp.exp(-_softplus(-z_w) - 0.5)
    a = jax.nn.sigmoid(a0_ref[...] + _mm(lora.astype(BF16), a2_ref[...]))
    gate = _mm(jax.nn.sigmoid(lora).astype(BF16), g2_ref[...])

    ones = _head_ones(GROUP_WIDTH)
    kk = k * kkp_ref[...]
    kk = kk * lax.rsqrt(jnp.maximum(_head_sum(kk * kk, ones), 1e-12))
    k2 = k * (1.0 + (a - 1.0) * kap_ref[...])
    b = kk * a
    if n_valid < chunk:
        valid = row < n_valid
        logw = jnp.where(valid, logw, 0.0)
        kk = jnp.where(valid, kk, 0.0)
        b = jnp.where(valid, b, 0.0)
        k2v = jnp.where(valid, k2, 0.0)
        vv = jnp.where(valid, v, 0.0)
    else:
        k2v, vv = k2, v

    ri, ci, strict, incl = _tri_masks(chunk)
    log_cum = _mm_exact_lhs(incl.astype(BF16), logw)
    log_end = log_cum[chunk - 1:chunk, :]
    w_in = jnp.exp(log_cum)
    w_inv = jnp.exp(-log_cum)
    w_tail = jnp.exp(log_end - log_cum)
    a_bar = kk * jnp.exp(log_cum - logw)
    b_bar = b * w_inv
    k_bar = k2v * w_inv
    r_bar = r * w_in
    b_end = b * w_tail
    k_end = k2v * w_tail
    w_end = jnp.exp(log_end)

    ys = []
    for h in range(N_HEADS):
        sl = slice(h * HEAD_DIM, (h + 1) * HEAD_DIM)
        ar = jnp.concatenate([a_bar[:, sl], r_bar[:, sl]], axis=0)
        bk = jnp.concatenate([b_bar[:, sl], k_bar[:, sl]], axis=0)
        gram = _mm_nt(ar, bk)
        l_ab = jnp.where(strict, gram[:chunk, :chunk], 0.0)
        l_ak = jnp.where(strict, gram[:chunk, chunk:], 0.0)
        m_rb = jnp.where(incl, gram[chunk:, :chunk], 0.0)
        m_rk = jnp.where(incl, gram[chunk:, chunk:], 0.0)
        tinv = _tri_inv(l_ab, chunk, tri_precision)
        s_h = s_scr[h]
        ars = _mm_nt(ar, s_h)
        v_h = vv[:, sl]
        u = -_mm(tinv, ars[:chunk] + _mm(l_ak, v_h))
        ys.append(ars[chunk:] + _mm(m_rb, u) + _mm(m_rk, v_h))
        s_scr[h] = s_h * w_end[:, sl] + _mm_tn(u, b_end[:, sl]) + _mm_tn(v_h, k_end[:, sl])
    y = jnp.concatenate(ys, axis=1)

    inv_d = 1.0 / HEAD_DIM
    mean = _head_sum(y, ones) * inv_d
    d = y - mean
    var = _head_sum(d * d, ones) * inv_d
    yn = d * lax.rsqrt(var + RW_LN_EPS) * lnw_ref[...] + lnb_ref[...]
    bonus = _head_sum(r * k2 * rk_ref[...], ones) * v
    o_ref[0] = (yn + bonus) * gate

    @pl.when(c_idx == pl.num_programs(1) - 1)
    def _():
        s_out_ref[0] = s_scr[...]


def _rwkv(p, shift, s0, prm, chunk, n_valid, tri_precision):
    bsz, seq, _ = p.shape
    vec = _const_spec((1, GROUP_WIDTH))
    mat = _const_spec((GROUP_WIDTH, GROUP_WIDTH))
    state = pl.BlockSpec((1, N_HEADS, HEAD_DIM, HEAD_DIM), lambda b, c: (b, 0, 0, 0))
    return pl.pallas_call(
        functools.partial(_rwkv_kernel, chunk=chunk, n_valid=n_valid, tri_precision=tri_precision),
        grid=(bsz, seq // chunk),
        in_specs=[pl.BlockSpec((1, chunk, RW_PROJ), lambda b, c: (b, c, 0)),
                  pl.BlockSpec((1, 1, RW_PROJ), lambda b, c: (b, 0, 0)),
                  state, _const_spec((1, RW_PROJ))] + [vec] * 7 + [mat] * 3,
        out_specs=[pl.BlockSpec((1, chunk, GROUP_WIDTH), lambda b, c: (b, c, 0)), state],
        out_shape=[jax.ShapeDtypeStruct((bsz, seq, GROUP_WIDTH), F32),
                   jax.ShapeDtypeStruct(s0.shape, F32)],
        scratch_shapes=[pltpu.VMEM((N_HEADS, HEAD_DIM, HEAD_DIM), F32), pltpu.VMEM((1, RW_PROJ), F32)],
        compiler_params=pltpu.CompilerParams(dimension_semantics=("arbitrary", "arbitrary")),
        name="rwkv7",
    )(p, shift, s0, prm["mu"], prm["w0"], prm["a0"], prm["kk"], prm["ka"], prm["rk"], prm["ln_w"], prm["ln_b"],
      prm["w2"], prm["a2"], prm["g2"])


def _gdn_kernel(qkv_ref, z_ref, ba_ref, buf_ref, s0_ref, convw_ref, alog_ref, dtb_ref, nw_ref,
                o_ref, s_out_ref, s_scr, x_scr, *, chunk, n_valid, tri_precision):
    c_idx = pl.program_id(1)
    pad = SUBLANES

    @pl.when(c_idx == 0)
    def _():
        s_scr[...] = s0_ref[0]
        x_scr[0:pad, :] = jnp.zeros((pad, GDN_QKV), F32)
        x_scr[pad - (GDN_CONV - 1):pad, :] = buf_ref[0]

    x_scr[pad:pad + chunk, :] = qkv_ref[0]
    conv = None
    for j in range(GDN_CONV):
        term = x_scr[pad - j:pad - j + chunk, :] * convw_ref[GDN_CONV - 1 - j:GDN_CONV - j, :]
        conv = term if conv is None else conv + term
    x_scr[0:pad, :] = x_scr[n_valid:n_valid + pad, :]
    act = _silu(conv)
    q = act[:, 0:GROUP_WIDTH]
    k = act[:, GROUP_WIDTH:2 * GROUP_WIDTH]
    v = act[:, 2 * GROUP_WIDTH:3 * GROUP_WIDTH]
    ones = _head_ones(GROUP_WIDTH)
    q = q * lax.rsqrt(jnp.maximum(_head_sum(q * q, ones), 1e-12)) * (HEAD_DIM ** -0.5)
    k = k * lax.rsqrt(jnp.maximum(_head_sum(k * k, ones), 1e-12))

    ba = ba_ref[0]
    lane = lax.broadcasted_iota(jnp.int32, (chunk, LANES), 1)
    row = lax.broadcasted_iota(jnp.int32, (chunk, LANES), 0)
    beta = jax.nn.sigmoid(ba)
    glog = -jnp.exp(alog_ref[...]) * _softplus(ba + dtb_ref[...])
    if n_valid < chunk:
        beta = jnp.where(row < n_valid, beta, 0.0)
        glog = jnp.where(row < n_valid, glog, 0.0)
    ri, ci, strict, incl = _tri_masks(chunk)
    gcum = _mm_exact_lhs(incl.astype(BF16), glog)
    both_t = jnp.where(lane < N_HEADS, beta, gcum).T

    os_ = []
    for h in range(N_HEADS):
        sl = slice(h * HEAD_DIM, (h + 1) * HEAD_DIM)
        g_col = gcum[:, N_HEADS + h:N_HEADS + h + 1]
        g_row = both_t[N_HEADS + h:N_HEADS + h + 1, :]
        beta_col = beta[:, h:h + 1]
        decay = jnp.where(incl, jnp.exp(jnp.where(incl, g_col - g_row, 0.0)), 0.0)
        gam = jnp.exp(g_col)
        k_h, q_h, v_h = k[:, sl], q[:, sl], v[:, sl]
        kq = jnp.concatenate([k_h, q_h], axis=0)
        gram = _mm_nt(kq, k_h)
        a_mat = jnp.where(strict, beta_col * decay * gram[:chunk], 0.0)
        tinv = _tri_inv(a_mat, chunk, tri_precision)
        s_h = s_scr[h]
        kqs = _mm_nt(kq, s_h)
        rhs = beta_col * (v_h - gam * kqs[:chunk])
        u = _mm(tinv, rhs)
        os_.append(gam * kqs[chunk:] + _mm(gram[chunk:] * decay, u))
        g_end = g_col[chunk - 1:chunk, :]
        s_scr[h] = jnp.exp(g_end) * s_h + _mm_tn(u * jnp.exp(g_end - g_col), k_h)
    o = jnp.concatenate(os_, axis=1)
    ms = _head_sum(o * o, ones) * (1.0 / HEAD_DIM)
    o_ref[0] = o * lax.rsqrt(ms + NORM_EPS) * nw_ref[...] * _silu(z_ref[0])

    @pl.when(c_idx == pl.num_programs(1) - 1)
    def _():
        s_out_ref[0] = s_scr[...]


def _gdn(qkv, z, ba, buf, s0, prm, chunk, n_valid, tri_precision):
    bsz, seq, _ = qkv.shape
    state = pl.BlockSpec((1, N_HEADS, HEAD_DIM, HEAD_DIM), lambda b, c: (b, 0, 0, 0))
    blk = lambda w: pl.BlockSpec((1, chunk, w), lambda b, c: (b, c, 0))
    return pl.pallas_call(
        functools.partial(_gdn_kernel, chunk=chunk, n_valid=n_valid, tri_precision=tri_precision),
        grid=(bsz, seq // chunk),
        in_specs=[blk(GDN_QKV), blk(GROUP_WIDTH), blk(LANES),
                  pl.BlockSpec((1, GDN_CONV - 1, GDN_QKV), lambda b, c: (b, 0, 0)), state,
                  _const_spec((GDN_CONV, GDN_QKV)), _const_spec((1, LANES)), _const_spec((1, LANES)),
                  _const_spec((1, GROUP_WIDTH))],
        out_specs=[blk(GROUP_WIDTH), state],
        out_shape=[jax.ShapeDtypeStruct((bsz, seq, GROUP_WIDTH), F32), jax.ShapeDtypeStruct(s0.shape, F32)],
        scratch_shapes=[pltpu.VMEM((N_HEADS, HEAD_DIM, HEAD_DIM), F32),
                        pltpu.VMEM((SUBLANES + chunk, GDN_QKV), F32)],
        compiler_params=pltpu.CompilerParams(dimension_semantics=("arbitrary", "arbitrary")),
        name="gdn",
    )(qkv, z, ba, buf, s0, prm["conv_w"], prm["a_log"], prm["dt_bias"], prm["norm_w"])


def _gelu_tanh(x):
    return 0.5 * x * (1.0 + jnp.tanh(math.sqrt(2.0 / math.pi) * (x + 0.044715 * (x * x * x))))


def _s5_kernel(u_ref, x0re_ref, x0im_ref, are_ref, aim_ref, ldt_ref, bre_ref, bim_ref, cre_ref, cim_ref, d_ref,
               wglu_ref, bglu_ref, o_ref, hre_out_ref, him_out_ref, hre_scr, him_scr, sre_scr, sim_scr, *, steps, rows):
    c_idx = pl.program_id(1)

    @pl.when(c_idx == 0)
    def _():
        sre_scr[...] = x0re_ref[...]
        sim_scr[...] = x0im_ref[...]

    a_re, a_im = are_ref[...], aim_ref[...]
    dt = jnp.exp(ldt_ref[...])
    mag = jnp.exp(dt * a_re)
    ab_re = mag * jnp.cos(dt * a_im)
    ab_im = mag * jnp.sin(dt * a_im)
    den = a_re * a_re + a_im * a_im
    nr = ab_re - 1.0
    cf_re = (nr * a_re + ab_im * a_im) / den
    cf_im = (ab_im * a_re - nr * a_im) / den

    u = u_ref[...].reshape(steps * rows, GROUP_WIDTH)
    ub = u.astype(BF16)
    bu_re = _mm(ub, bre_ref[...])
    bu_im = _mm(ub, bim_ref[...])
    hre_scr[...] = cf_re * bu_re - cf_im * bu_im
    him_scr[...] = cf_re * bu_im + cf_im * bu_re

    abr = jnp.broadcast_to(ab_re, (rows, S5_LANES))
    abi = jnp.broadcast_to(ab_im, (rows, S5_LANES))

    def step(t, carry):
        h_re, h_im = carry
        base = pl.multiple_of(t * rows, rows)
        n_re = abr * h_re - abi * h_im + hre_scr[pl.ds(base, rows), :]
        n_im = abr * h_im + abi * h_re + him_scr[pl.ds(base, rows), :]
        hre_scr[pl.ds(base, rows), :] = n_re
        him_scr[pl.ds(base, rows), :] = n_im
        return n_re, n_im

    h_re, h_im = lax.fori_loop(0, steps, step, (sre_scr[...], sim_scr[...]))
    sre_scr[...] = h_re
    sim_scr[...] = h_im

    y = (_mm(hre_scr[...].astype(BF16), cre_ref[...]) - _mm(him_scr[...].astype(BF16), cim_ref[...])
         + d_ref[...] * u)
    zz = _gelu_tanh(y)
    out = zz * jax.nn.sigmoid(_mm(zz.astype(BF16), wglu_ref[...]) + bglu_ref[...])
    o_ref[...] = out.reshape(steps, rows, GROUP_WIDTH)

    @pl.when(c_idx == pl.num_programs(1) - 1)
    def _():
        hre_out_ref[...] = h_re
        him_out_ref[...] = h_im


def _s5(u_t, x0_re, x0_im, prm, steps, rows):
    seq, bsz, _ = u_t.shape
    st = pl.BlockSpec((rows, S5_LANES), lambda b, c: (b, 0))
    vec = _const_spec((1, S5_LANES))
    return pl.pallas_call(
        functools.partial(_s5_kernel, steps=steps, rows=rows),
        grid=(bsz // rows, seq // steps),
        in_specs=[pl.BlockSpec((steps, rows, GROUP_WIDTH), lambda b, c: (c, b, 0)), st, st, vec, vec, vec,
                  _const_spec((GROUP_WIDTH, S5_LANES)), _const_spec((GROUP_WIDTH, S5_LANES)),
                  _const_spec((S5_LANES, GROUP_WIDTH)), _const_spec((S5_LANES, GROUP_WIDTH)),
                  _const_spec((1, GROUP_WIDTH)), _const_spec((GROUP_WIDTH, GROUP_WIDTH)),
                  _const_spec((1, GROUP_WIDTH))],
        out_specs=[pl.BlockSpec((steps, rows, GROUP_WIDTH), lambda b, c: (c, b, 0)), st, st],
        out_shape=[jax.ShapeDtypeStruct((seq, bsz, GROUP_WIDTH), F32),
                   jax.ShapeDtypeStruct((bsz, S5_LANES), F32), jax.ShapeDtypeStruct((bsz, S5_LANES), F32)],
        scratch_shapes=[pltpu.VMEM((steps * rows, S5_LANES), F32), pltpu.VMEM((steps * rows, S5_LANES), F32),
                        pltpu.VMEM((rows, S5_LANES), F32), pltpu.VMEM((rows, S5_LANES), F32)],
        compiler_params=pltpu.CompilerParams(dimension_semantics=("arbitrary", "arbitrary"),
                                             vmem_limit_bytes=VMEM_LIMIT),
        name="s5",
    )(u_t, x0_re, x0_im, prm["a_re"], prm["a_im"], prm["log_dt"], prm["b_re"], prm["b_im"], prm["c_re"],
      prm["c_im"], prm["d"], prm["w_glu"], prm["b_glu"])


def _rope(x, cos, sin, width):
    fwd = pltpu.roll(x, width - ROPE_DIM // 2, 1)
    bwd = pltpu.roll(x, ROPE_DIM // 2, 1)
    lane = lax.broadcasted_iota(jnp.int32, x.shape, 1) % HEAD_DIM
    return x * cos + jnp.where(lane < ROPE_DIM // 2, fwd, bwd) * sin


def _swa_kernel(p_ref, cos_ref, sin_ref, kbuf_ref, vbuf_ref, sink_ref, o_ref, krot_ref, k_scr, v_scr,
                *, qb, start, carry):
    j = pl.program_id(1)
    wb = WINDOW

    @pl.when(j == 0)
    def _():
        k_scr[0:wb, :] = kbuf_ref[0]
        v_scr[0:wb, :] = vbuf_ref[0]
        if qb < wb:
            k_scr[wb:2 * wb, :] = jnp.zeros((wb, SWA_KV_WIDTH), F32)
            v_scr[wb:2 * wb, :] = jnp.zeros((wb, SWA_KV_WIDTH), F32)

    p = p_ref[0]
    cos, sin = cos_ref[...], sin_ref[...]
    q = _rope(p[:, 0:GROUP_WIDTH], cos, sin, GROUP_WIDTH)
    k = _rope(p[:, GROUP_WIDTH:GROUP_WIDTH + SWA_KV_WIDTH], cos[:, 0:SWA_KV_WIDTH], sin[:, 0:SWA_KV_WIDTH],
              SWA_KV_WIDTH)
    v = p[:, GROUP_WIDTH + SWA_KV_WIDTH:SWA_PROJ]
    krot_ref[0] = k
    k_scr[wb:wb + qb, :] = k
    v_scr[wb:wb + qb, :] = v
    keys = k_scr[...]
    vals = v_scr[...]

    nk = 2 * wb
    rq = lax.broadcasted_iota(jnp.int32, (2 * qb, nk), 0) % qb
    ck = lax.broadcasted_iota(jnp.int32, (2 * qb, nk), 1)
    rel = rq + wb - ck
    kpos = start + j * qb - wb + ck
    valid = (rel >= 0) & (rel < WINDOW) & (kpos >= 0)
    upper = lax.broadcasted_iota(jnp.int32, (2 * qb, 1), 0) < qb
    sinks = sink_ref[...]

    outs = []
    for h in range(SWA_KV_HEADS):
        q2 = jnp.concatenate([q[:, (2 * h) * HEAD_DIM:(2 * h + 1) * HEAD_DIM],
                              q[:, (2 * h + 1) * HEAD_DIM:(2 * h + 2) * HEAD_DIM]], axis=0)
        kh = keys[:, h * HEAD_DIM:(h + 1) * HEAD_DIM]
        vh = vals[:, h * HEAD_DIM:(h + 1) * HEAD_DIM]
        s = jnp.where(valid, _mm_nt(q2, kh) * (HEAD_DIM ** -0.5), -jnp.inf)
        sink = jnp.where(upper, sinks[:, 2 * h:2 * h + 1], sinks[:, 2 * h + 1:2 * h + 2])
        m = jnp.maximum(jnp.max(s, axis=-1, keepdims=True), sink)
        e = jnp.exp(s - m)
        den = jnp.sum(e, axis=-1, keepdims=True) + jnp.exp(sink - m)
        o2 = _mm(e, vh) / den
        outs += [o2[:qb], o2[qb:]]
    o_ref[0] = jnp.concatenate(outs, axis=1)

    if carry:
        k_scr[0:wb, :] = k
        v_scr[0:wb, :] = v


def _swa(p, cos, sin, kbuf, vbuf, sinks, qb, start):
    bsz, seq, _ = p.shape
    nb = seq // qb
    assert nb == 1 or qb == WINDOW
    blk = lambda w: pl.BlockSpec((1, qb, w), lambda b, j: (b, j, 0))
    tab = pl.BlockSpec((qb, GROUP_WIDTH), lambda b, j: (j, 0))
    buf = pl.BlockSpec((1, WINDOW, SWA_KV_WIDTH), lambda b, j: (b, 0, 0))
    return pl.pallas_call(
        functools.partial(_swa_kernel, qb=qb, start=start, carry=nb > 1),
        grid=(bsz, nb),
        in_specs=[blk(SWA_PROJ), tab, tab, buf, buf, _const_spec((1, LANES))],
        out_specs=[blk(GROUP_WIDTH), pl.BlockSpec((1, qb, SWA_KV_WIDTH), lambda b, j: (b, 0, 0))],
        out_shape=[jax.ShapeDtypeStruct((bsz, seq, GROUP_WIDTH), F32),
                   jax.ShapeDtypeStruct((bsz, qb, SWA_KV_WIDTH), F32)],
        scratch_shapes=[pltpu.VMEM((2 * WINDOW, SWA_KV_WIDTH), F32), pltpu.VMEM((2 * WINDOW, SWA_KV_WIDTH), F32)],
        compiler_params=pltpu.CompilerParams(dimension_semantics=("arbitrary", "arbitrary")),
        name="swa",
    )(p, cos, sin, kbuf, vbuf, sinks)


def _rope_tables(start, seq):
    half = ROPE_DIM // 2
    inv = ROPE_THETA ** (-jnp.arange(0, ROPE_DIM, 2, dtype=F32) / ROPE_DIM)
    ang = (start + jnp.arange(seq)).astype(F32)[:, None] * inv[None, :]
    cos, sin = jnp.cos(ang), jnp.sin(ang)
    rest = HEAD_DIM - ROPE_DIM
    cos_h = jnp.concatenate([cos, cos, jnp.ones((seq, rest), F32)], axis=1)
    sin_h = jnp.concatenate([-sin, sin, jnp.zeros((seq, rest), F32)], axis=1)
    return jnp.tile(cos_h, (1, N_HEADS)), jnp.tile(sin_h, (1, N_HEADS))


def _pad_rows(w, top, total):
    return jnp.pad(w, ((top, total - top - w.shape[0]), (0, 0)))


def _block_diag_in(b):
    eye = jnp.eye(S5_GROUPS, dtype=b.dtype)
    return jnp.einsum("gnc,gh->gchn", b, eye).reshape(S5_GROUPS * S5_CH, S5_GROUPS * S5_STATE)


def _block_diag_out(c):
    eye = jnp.eye(S5_GROUPS, dtype=c.dtype)
    return jnp.einsum("gcn,gh->gnhc", c, eye).reshape(S5_GROUPS * S5_STATE, S5_GROUPS * S5_CH)


def _layer_params(l, g_mix_pre, g_mix_post, g_mlp_pre, g_mlp_post, w_in, w_out, rw_mu, rw_w0, rw_w2, rw_a0, rw_a2,
                  rw_g2, rw_kk, rw_ka, rw_rk, rw_ln_w, rw_ln_b, s5_a_re, s5_a_im, s5_log_dt, s5_b_re, s5_b_im,
                  s5_c_re, s5_c_im, s5_d, s5_w_glu, s5_b_glu, gdn_conv_w, gdn_a_log, gdn_dt_bias, gdn_norm_w,
                  swa_sinks, w_up, w_down):
    row = lambda a: a[l].astype(F32).reshape(1, -1)
    wi = w_in[l].astype(F32)
    o_s5 = RW_PROJ
    o_gdn = o_s5 + GROUP_WIDTH
    o_ba = o_gdn + GDN_QKV
    o_z = o_ba + 2 * N_HEADS
    o_swa = o_z + GROUP_WIDTH
    w_all = jnp.concatenate([
        wi[:, :o_s5], wi[:, o_s5:o_gdn], wi[:, o_gdn:o_ba], wi[:, o_z:o_swa],
        jnp.pad(wi[:, o_ba:o_z], ((0, 0), (0, LANES - 2 * N_HEADS))), wi[:, o_swa:]], axis=1).astype(BF16)
    lane_pad = lambda a: jnp.pad(a[l].astype(F32), (N_HEADS, LANES - 2 * N_HEADS)).reshape(1, LANES)
    return {
        "g_mix_pre": row(g_mix_pre), "g_mix_post": row(g_mix_post), "g_mlp_pre": row(g_mlp_pre),
        "g_mlp_post": row(g_mlp_post), "w_all": w_all, "w_out": w_out[l].astype(BF16),
        "w_up": w_up[l].astype(BF16), "w_down": w_down[l].astype(BF16),
        "rw": {"mu": row(rw_mu), "w0": row(rw_w0), "a0": row(rw_a0), "kk": row(rw_kk), "ka": row(rw_ka),
               "rk": row(rw_rk), "ln_w": row(rw_ln_w), "ln_b": row(rw_ln_b),
               "w2": _pad_rows(rw_w2[l].astype(F32), 0, GROUP_WIDTH).astype(BF16),
               "a2": _pad_rows(rw_a2[l].astype(F32), RW_DECAY_RANK, GROUP_WIDTH).astype(BF16),
               "g2": _pad_rows(rw_g2[l].astype(F32), RW_DECAY_RANK + RW_ICLR_RANK, GROUP_WIDTH).astype(BF16)},
        "s5": {"a_re": row(s5_a_re), "a_im": row(s5_a_im),
               "log_dt": jnp.repeat(s5_log_dt[l].astype(F32), S5_STATE).reshape(1, S5_LANES),
               "b_re": _block_diag_in(s5_b_re[l].astype(F32)).astype(BF16),
               "b_im": _block_diag_in(s5_b_im[l].astype(F32)).astype(BF16),
               "c_re": _block_diag_out(s5_c_re[l].astype(F32)).astype(BF16),
               "c_im": _block_diag_out(s5_c_im[l].astype(F32)).astype(BF16),
               "d": row(s5_d), "w_glu": s5_w_glu[l].astype(BF16), "b_glu": row(s5_b_glu)},
        "gdn": {"conv_w": gdn_conv_w[l].astype(F32), "a_log": lane_pad(gdn_a_log), "dt_bias": lane_pad(gdn_dt_bias),
                "norm_w": jnp.tile(gdn_norm_w[l].astype(F32), N_HEADS).reshape(1, GROUP_WIDTH)},
        "sinks": jnp.pad(swa_sinks[l].astype(F32), (0, LANES - N_HEADS)).reshape(1, LANES),
    }


def _layer(x, start, st, prm, tri_precision):
    rw_s, rw_shift, s5_re, s5_im, gdn_s, gdn_conv, swa_k, swa_v = st
    bsz, seq, _ = x.shape
    tokens = bsz * seq
    tm = min(512, tokens)
    chunk = CHUNK if seq % CHUNK == 0 else -(-seq // SUBLANES) * SUBLANES
    seq_pad = -(-seq // chunk) * chunk
    n_valid = chunk - (seq_pad - seq)
    assert seq_pad == seq or seq_pad == chunk
    assert seq >= GDN_CONV - 1

    x2d = x.reshape(tokens, D_MODEL)
    p_rw, p_s5, p_qkv, p_z, p_ba, p_swa = (
        a.reshape(bsz, seq, -1) for a in _in_proj(x2d, prm["g_mix_pre"], prm["w_all"], tm))
    pad = lambda a: a if seq_pad == seq else jnp.pad(a, ((0, 0), (0, seq_pad - seq), (0, 0)))

    o_rw, rw_s_new = _rwkv(pad(p_rw), rw_shift, rw_s, prm["rw"], chunk, n_valid, tri_precision)
    rw_shift_new = p_rw[:, seq - 1:, :]

    rows = 16
    steps = min(seq, 64)
    o_s5_t, s5_re_new, s5_im_new = _s5(jnp.swapaxes(p_s5, 0, 1), s5_re.reshape(bsz, S5_LANES),
                                       s5_im.reshape(bsz, S5_LANES), prm["s5"], steps, rows)
    o_s5 = jnp.swapaxes(o_s5_t, 0, 1)

    o_gdn, gdn_s_new = _gdn(pad(p_qkv), pad(p_z), pad(p_ba), gdn_conv, gdn_s, prm["gdn"], chunk, n_valid,
                            tri_precision)
    gdn_conv_new = p_qkv[:, seq - (GDN_CONV - 1):, :]

    qb = WINDOW if seq % WINDOW == 0 else seq_pad
    cos, sin = _rope_tables(start, seq_pad)
    wbuf = swa_k.shape[1]
    o_swa, k_rot = _swa(pad(p_swa), cos, sin, swa_k.reshape(bsz, wbuf, SWA_KV_WIDTH),
                        swa_v.reshape(bsz, wbuf, SWA_KV_WIDTH), prm["sinks"], qb, start)
    k_new = k_rot[:, :qb - (seq_pad - seq), :] if seq_pad != seq else k_rot
    v_new = p_swa[:, seq - min(seq, wbuf):, GROUP_WIDTH + SWA_KV_WIDTH:]
    kv_shape = (bsz, -1, SWA_KV_HEADS, HEAD_DIM)
    if k_new.shape[1] >= wbuf:
        swa_k_new = k_new[:, -wbuf:].reshape(kv_shape)
        swa_v_new = v_new[:, -wbuf:].reshape(kv_shape)
    else:
        swa_k_new = jnp.concatenate([swa_k, k_new.reshape(kv_shape)], axis=1)[:, -wbuf:]
        swa_v_new = jnp.concatenate([swa_v, v_new.reshape(kv_shape)], axis=1)[:, -wbuf:]

    flat = lambda a: a[:, :seq, :].reshape(tokens, GROUP_WIDTH)
    y = _post(x2d, (flat(o_rw), flat(o_s5), flat(o_gdn), flat(o_swa)), prm["w_out"], prm["g_mix_post"],
              prm["g_mlp_pre"], prm["w_up"], prm["w_down"], prm["g_mlp_post"], tm)
    new_state = (rw_s_new, rw_shift_new, s5_re_new.reshape(s5_re.shape), s5_im_new.reshape(s5_im.shape),
                 gdn_s_new, gdn_conv_new, swa_k_new, swa_v_new)
    return y.reshape(bsz, seq, D_MODEL), new_state


def kernel(x_prompt, x_sample, state_rwkv, state_rwkv_shift, state_s5_re, state_s5_im, state_gdn, state_gdn_conv,
           cache_swa_k, cache_swa_v, g_mix_pre, g_mix_post, g_mlp_pre, g_mlp_post, w_in, w_out, rw_mu, rw_w0, rw_w2,
           rw_a0, rw_a2, rw_g2, rw_kk, rw_ka, rw_rk, rw_ln_w, rw_ln_b, s5_a_re, s5_a_im, s5_log_dt, s5_b_re, s5_b_im,
           s5_c_re, s5_c_im, s5_d, s5_w_glu, s5_b_glu, gdn_conv_w, gdn_a_log, gdn_dt_bias, gdn_norm_w, swa_sinks,
           w_up, w_down):
    weights = (g_mix_pre, g_mix_post, g_mlp_pre, g_mlp_post, w_in, w_out, rw_mu, rw_w0, rw_w2, rw_a0, rw_a2, rw_g2,
               rw_kk, rw_ka, rw_rk, rw_ln_w, rw_ln_b, s5_a_re, s5_a_im, s5_log_dt, s5_b_re, s5_b_im, s5_c_re, s5_c_im,
               s5_d, s5_w_glu, s5_b_glu, gdn_conv_w, gdn_a_log, gdn_dt_bias, gdn_norm_w, swa_sinks, w_up, w_down)
    depth = w_in.shape[0]
    bp = x_prompt.shape[0]
    past_len = 16384
    tri_precision = lax.Precision.HIGHEST
    xp = x_prompt.astype(F32)
    xs = x_sample.astype(F32)
    zp = lambda *shape: jnp.zeros((bp,) + shape, F32)
    prompt_init = (zp(N_HEADS, HEAD_DIM, HEAD_DIM), zp(1, RW_PROJ), zp(S5_GROUPS, S5_STATE), zp(S5_GROUPS, S5_STATE),
                   zp(N_HEADS, HEAD_DIM, HEAD_DIM), zp(GDN_CONV - 1, GDN_QKV),
                   zp(WINDOW, SWA_KV_HEADS, HEAD_DIM), zp(WINDOW, SWA_KV_HEADS, HEAD_DIM))
    sample_states = (state_rwkv, state_rwkv_shift, state_s5_re, state_s5_im, state_gdn, state_gdn_conv,
                     cache_swa_k, cache_swa_v)
    new_p, new_s = [], []
    for l in range(depth):
        prm = _layer_params(l, *weights)
        xp, st_p = _layer(xp, 0, prompt_init, prm, tri_precision)
        xs, st_s = _layer(xs, past_len, tuple(a[l].astype(F32) for a in sample_states), prm, tri_precision)
        new_p.append(st_p)
        new_s.append(st_s)
    outs = [xp.astype(x_prompt.dtype), xs.astype(x_sample.dtype)]
    for i in range(len(sample_states)):
        outs.append(jnp.stack([st[i] for st in new_p], axis=0))
        outs.append(jnp.stack([st[i] for st in new_s], axis=0))
    return tuple(outs)
```

```python
import functools
import math

import jax
import jax.numpy as jnp
from jax import lax
from jax.experimental import pallas as pl
from jax.experimental.pallas import tpu as pltpu

F32 = jnp.float32
BF16 = jnp.bfloat16

D_MODEL = 1024
HEAD_DIM = 64
N_HEADS = 4
GROUP_WIDTH = N_HEADS * HEAD_DIM
NORM_EPS = 1e-6
RW_LN_EPS = 64e-5
RW_PROJ = 1024
RW_LORA_OFF = 3 * GROUP_WIDTH
RW_DECAY_RANK, RW_ICLR_RANK, RW_GATE_RANK = 64, 64, 128
S5_GROUPS, S5_CH, S5_STATE = 16, 16, 64
S5_LANES = S5_GROUPS * S5_STATE
GDN_CONV = 4
GDN_QKV = 3 * GROUP_WIDTH
SWA_KV_HEADS = 2
SWA_KV_WIDTH = SWA_KV_HEADS * HEAD_DIM
SWA_PROJ = GROUP_WIDTH + 2 * SWA_KV_WIDTH
WINDOW = 128
ROPE_DIM = 16
ROPE_THETA = 500000.0
D_FF = 4096
LANES = 128
SUBLANES = 8
CHUNK = 64
MIX_CHUNKS = 4
TRI_BLOCK = 16
VMEM_LIMIT = 56 * 1024 * 1024

PROJ_WIDTHS = (RW_PROJ, GROUP_WIDTH, GDN_QKV, GROUP_WIDTH, LANES, SWA_PROJ)


def _dot(a, b, dims):
    return lax.dot_general(a.astype(BF16), b.astype(BF16), (dims, ((), ())), preferred_element_type=F32)


def _mm(a, b):
    return _dot(a, b, ((1,), (0,)))


def _mm_nt(a, b):
    return _dot(a, b, ((1,), (1,)))


def _mm_tn(a, b):
    return _dot(a, b, ((0,), (0,)))


def _split3(x):
    hi = x.astype(BF16)
    r1 = x - hi.astype(F32)
    mid = r1.astype(BF16)
    lo = (r1 - mid.astype(F32)).astype(BF16)
    return hi, mid, lo


def _mm_exact_rhs(x, m):
    hi, mid, lo = _split3(x)
    return _mm(hi, m) + _mm(mid, m) + _mm(lo, m)


def _mm_exact_lhs(m, x):
    hi, mid, lo = _split3(x)
    return _mm(m, hi) + _mm(m, mid) + _mm(m, lo)


def _head_ones(width):
    r = lax.broadcasted_iota(jnp.int32, (width, width), 0)
    c = lax.broadcasted_iota(jnp.int32, (width, width), 1)
    return (r // HEAD_DIM == c // HEAD_DIM).astype(BF16)


def _head_sum(x, ones):
    return _mm_exact_rhs(x, ones)


def _tri_masks(n):
    r = lax.broadcasted_iota(jnp.int32, (n, n), 0)
    c = lax.broadcasted_iota(jnp.int32, (n, n), 1)
    return r, c, r > c, r >= c


def _softplus(x):
    return jnp.maximum(x, 0.0) + jnp.log1p(jnp.exp(-jnp.abs(x)))


def _silu(x):
    return x * jax.nn.sigmoid(x)


def _rms(x, g):
    return x * lax.rsqrt(jnp.mean(x * x, axis=-1, keepdims=True) + NORM_EPS) * g


def _each(fn, *lists):
    return [fn(*xs) for xs in zip(*lists)]


def _tri_inv(lows, n):
    r, c, _, _ = _tri_masks(n)
    eye = (r == c).astype(F32)
    bs = min(TRI_BLOCK, n)
    if n > bs:
        same = (r // bs) == (c // bs)
        diags = [jnp.where(same, low, 0.0) for low in lows]
        rests = _each(lambda low, d: low - d, lows, diags)
    else:
        diags, rests = lows, None
    invs = [eye - d for d in diags]
    pws, p = diags, 1
    while 2 * p < bs:
        pws = _each(lambda x: _mm(x, x), pws)
        invs = _each(lambda i, x: i + _mm(i, x), invs, pws)
        p *= 2
    if rests is None:
        return invs
    nils = _each(_mm, invs, rests)
    outs = [eye - x for x in nils]
    pws, p = nils, 1
    while 2 * p < n // bs:
        pws = _each(lambda x: _mm(x, x), pws)
        outs = _each(lambda o, x: o + _mm(o, x), outs, pws)
        p *= 2
    return _each(_mm, outs, invs)


def _const_spec(shape):
    zeros = (0,) * len(shape)
    return pl.BlockSpec(shape, lambda *_: zeros, pipeline_mode=pl.Buffered(1))


def _in_proj_kernel(x_ref, g_ref, w_ref, *out_refs):
    h = _rms(x_ref[...], g_ref[...]).astype(BF16)
    off = 0
    for o_ref, width in zip(out_refs, PROJ_WIDTHS):
        o_ref[...] = _mm(h, w_ref[:, off:off + width])
        off += width


def _in_proj(x2d, g, w_all, tm):
    t = x2d.shape[0]
    wtot = sum(PROJ_WIDTHS)
    return pl.pallas_call(
        _in_proj_kernel,
        grid=(t // tm,),
        in_specs=[pl.BlockSpec((tm, D_MODEL), lambda i: (i, 0)),
                  _const_spec((1, D_MODEL)),
                  _const_spec((D_MODEL, wtot))],
        out_specs=[pl.BlockSpec((tm, w), lambda i: (i, 0)) for w in PROJ_WIDTHS],
        out_shape=[jax.ShapeDtypeStruct((t, w), F32) for w in PROJ_WIDTHS],
        compiler_params=pltpu.CompilerParams(dimension_semantics=("arbitrary",), vmem_limit_bytes=VMEM_LIMIT),
        name="in_proj",
    )(x2d, g, w_all)


FF_CHUNK = 1024


def _post_kernel(x_ref, o0_ref, o1_ref, o2_ref, o3_ref, wout_ref, gpost_ref, gpre_ref, wup_ref, wdn_ref,
                 gmlp_ref, y_ref):
    mix = None
    for i, o_ref in enumerate((o0_ref, o1_ref, o2_ref, o3_ref)):
        part = _mm(o_ref[...].astype(BF16), wout_ref[i * GROUP_WIDTH:(i + 1) * GROUP_WIDTH, :])
        mix = part if mix is None else mix + part
    x1 = x_ref[...] + _rms(mix, gpost_ref[...])
    h = _rms(x1, gpre_ref[...]).astype(BF16)
    acc = None
    for c in range(D_FF // FF_CHUNK):
        u = _mm(h, wup_ref[:, c * FF_CHUNK:(c + 1) * FF_CHUNK])
        u = jnp.square(jnp.maximum(u, 0.0)).astype(BF16)
        part = _mm(u, wdn_ref[c * FF_CHUNK:(c + 1) * FF_CHUNK, :])
        acc = part if acc is None else acc + part
    y_ref[...] = x1 + _rms(acc, gmlp_ref[...])


def _post(x2d, outs, w_out, g_post, g_pre, w_up, w_dn, g_mlp, tm):
    t = x2d.shape[0]
    row = lambda w: pl.BlockSpec((tm, w), lambda i: (i, 0))
    return pl.pallas_call(
        _post_kernel,
        grid=(t // tm,),
        in_specs=[row(D_MODEL)] + [row(GROUP_WIDTH)] * 4 + [
            _const_spec((D_MODEL, D_MODEL)), _const_spec((1, D_MODEL)), _const_spec((1, D_MODEL)),
            _const_spec((D_MODEL, D_FF)), _const_spec((D_FF, D_MODEL)), _const_spec((1, D_MODEL))],
        out_specs=row(D_MODEL),
        out_shape=jax.ShapeDtypeStruct((t, D_MODEL), F32),
        compiler_params=pltpu.CompilerParams(dimension_semantics=("arbitrary",), vmem_limit_bytes=VMEM_LIMIT),
        name="post_mlp",
    )(x2d, *outs, w_out, g_post, g_pre, w_up, w_dn, g_mlp)


def _chunk_sums(x, chunk, rows):
    r = lax.broadcasted_iota(jnp.int32, (rows, rows), 0)
    c = lax.broadcasted_iota(jnp.int32, (rows, rows), 1)
    same = (r // chunk) == (c // chunk)
    cum = _mm_exact_lhs((same & (r >= c)).astype(BF16), x)
    tot = _mm_exact_lhs(same.astype(BF16), x) if rows > chunk else cum[rows - 1:rows, :]
    return cum, tot


def _rwkv_kernel(p_ref, shift_ref, s0_ref, mu_ref, w0_ref, a0_ref, kkp_ref, kap_ref, rk_ref, lnw_ref, lnb_ref,
                 w2_ref, a2_ref, g2_ref, o_ref, s_out_ref, s_scr, carry_scr, *, chunk, n_chunks, n_valid):
    c_idx = pl.program_id(1)
    rows = chunk * n_chunks

    @pl.when(c_idx == 0)
    def _():
        s_scr[...] = s0_ref[0]
        carry_scr[...] = shift_ref[0]

    p = p_ref[0]
    row = lax.broadcasted_iota(jnp.int32, (rows, 1), 0)
    prev = jnp.where(row == 0, carry_scr[...], pltpu.roll(p, 1, 0))
    last = rows - chunk + n_valid
    carry_scr[...] = p[last - 1:last, :]
    xs = p + (prev - p) * mu_ref[...]
    r = xs[:, 0:GROUP_WIDTH]
    k = xs[:, GROUP_WIDTH:2 * GROUP_WIDTH]
    v = xs[:, 2 * GROUP_WIDTH:3 * GROUP_WIDTH]
    lora = xs[:, RW_LORA_OFF:RW_PROJ]

    z_w = w0_ref[...] + _mm(jnp.tanh(lora), w2_ref[...])
    logw = -jnp.exp(-_softplus(-z_w) - 0.5)
    a = jax.nn.sigmoid(a0_ref[...] + _mm(lora, a2_ref[...]))
    gate = _mm(jax.nn.sigmoid(lora), g2_ref[...])

    ones = _head_ones(GROUP_WIDTH)
    kk = k * kkp_ref[...]
    kk = kk * lax.rsqrt(jnp.maximum(_head_sum(kk * kk, ones), 1e-12))
    k2 = k * (1.0 + (a - 1.0) * kap_ref[...])
    b = kk * a
    if n_valid < chunk:
        valid = row < n_valid
        logw = jnp.where(valid, logw, 0.0)
        kk = jnp.where(valid, kk, 0.0)
        b = jnp.where(valid, b, 0.0)
        k2v = jnp.where(valid, k2, 0.0)
        vv = jnp.where(valid, v, 0.0)
    else:
        k2v, vv = k2, v

    log_cum, log_end = _chunk_sums(logw, chunk, rows)
    w_inv = jnp.exp(-log_cum)
    w_tail = jnp.exp(log_end - log_cum)
    a_bar = kk * jnp.exp(log_cum - logw)
    b_bar = b * w_inv
    k_bar = k2v * w_inv
    r_bar = r * jnp.exp(log_cum)
    b_end = b * w_tail
    k_end = k2v * w_tail
    w_end = jnp.exp(log_end)

    _, _, strict, incl = _tri_masks(chunk)
    tiles = [(slice(c * chunk, (c + 1) * chunk), slice(h * HEAD_DIM, (h + 1) * HEAD_DIM))
             for c in range(n_chunks) for h in range(N_HEADS)]
    cut = lambda x: [x[rs, sl] for rs, sl in tiles]
    ab, rb, bb, kb, be, ke, vh = map(cut, (a_bar, r_bar, b_bar, k_bar, b_end, k_end, vv))
    gram = _each(lambda a_, r_, b_, k_: _mm_nt(jnp.concatenate([a_, r_], axis=0), jnp.concatenate([b_, k_], axis=0)),
                 ab, rb, bb, kb)
    l_ab = [jnp.where(strict, g[:chunk, :chunk], 0.0) for g in gram]
    l_ak = [jnp.where(strict, g[:chunk, chunk:], 0.0) for g in gram]
    m_rb = [jnp.where(incl, g[chunk:, :chunk], 0.0) for g in gram]
    m_rk = [jnp.where(incl, g[chunk:, chunk:], 0.0) for g in gram]
    tinv = _tri_inv(l_ab, chunk)
    lv = _each(_mm, l_ak, vh)
    tt = _each(lambda t_, a_, l_: _mm(t_, jnp.concatenate([a_, l_], axis=1)), tinv, ab, lv)
    ta_c0 = [jnp.concatenate([t_[:, :HEAD_DIM], -t_[:, HEAD_DIM:]], axis=1) for t_ in tt]
    pq = _each(_mm_tn, ta_c0, be)
    vk = _each(_mm_tn, vh, ke)
    ry = _each(_mm, m_rb, ta_c0)
    rkv = _each(_mm, m_rk, vh)
    pb = [x[:HEAD_DIM] for x in pq]
    q_mat = _each(lambda x, y_: x[HEAD_DIM:] + y_, pq, vk)
    rr = _each(lambda r_, x: r_ - x[:, :HEAD_DIM], rb, ry)
    y0 = _each(lambda x, y_: x[:, HEAD_DIM:] + y_, ry, rkv)

    state = [s_scr[h] for h in range(N_HEADS)]
    y_rows = []
    for c in range(n_chunks):
        i0 = c * N_HEADS
        ys = _each(lambda y_, r_, s_: y_ + _mm_nt(r_, s_), y0[i0:i0 + N_HEADS], rr[i0:i0 + N_HEADS], state)
        we = [w_end[c * chunk:c * chunk + 1, h * HEAD_DIM:(h + 1) * HEAD_DIM] for h in range(N_HEADS)]
        state = _each(lambda s_, w_, p_, q_: s_ * w_ - _mm(s_, p_) + q_, state, we, pb[i0:i0 + N_HEADS],
                      q_mat[i0:i0 + N_HEADS])
        y_rows.append(jnp.concatenate(ys, axis=1))
    for h in range(N_HEADS):
        s_scr[h] = state[h]
    y = y_rows[0] if n_chunks == 1 else jnp.concatenate(y_rows, axis=0)

    inv_d = 1.0 / HEAD_DIM
    mean = _head_sum(y, ones) * inv_d
    d = y - mean
    var = _head_sum(d * d, ones) * inv_d
    yn = d * lax.rsqrt(var + RW_LN_EPS) * lnw_ref[...] + lnb_ref[...]
    bonus = _head_sum(r * k2 * rk_ref[...], ones) * v
    o_ref[0] = (yn + bonus) * gate

    @pl.when(c_idx == pl.num_programs(1) - 1)
    def _():
        s_out_ref[0] = s_scr[...]


def _rwkv(p, shift, s0, prm, chunk, n_chunks, n_valid):
    bsz, seq, _ = p.shape
    rows = chunk * n_chunks
    assert n_valid == chunk or n_chunks == 1
    vec = _const_spec((1, GROUP_WIDTH))
    mat = _const_spec((GROUP_WIDTH, GROUP_WIDTH))
    state = pl.BlockSpec((1, N_HEADS, HEAD_DIM, HEAD_DIM), lambda b, c: (b, 0, 0, 0))
    return pl.pallas_call(
        functools.partial(_rwkv_kernel, chunk=chunk, n_chunks=n_chunks, n_valid=n_valid),
        grid=(bsz, seq // rows),
        in_specs=[pl.BlockSpec((1, rows, RW_PROJ), lambda b, c: (b, c, 0)),
                  pl.BlockSpec((1, 1, RW_PROJ), lambda b, c: (b, 0, 0)),
                  state, _const_spec((1, RW_PROJ))] + [vec] * 7 + [mat] * 3,
        out_specs=[pl.BlockSpec((1, rows, GROUP_WIDTH), lambda b, c: (b, c, 0)), state],
        out_shape=[jax.ShapeDtypeStruct((bsz, seq, GROUP_WIDTH), F32),
                   jax.ShapeDtypeStruct(s0.shape, F32)],
        scratch_shapes=[pltpu.VMEM((N_HEADS, HEAD_DIM, HEAD_DIM), F32), pltpu.VMEM((1, RW_PROJ), F32)],
        compiler_params=pltpu.CompilerParams(dimension_semantics=("arbitrary", "arbitrary")),
        name="rwkv7",
    )(p, shift, s0, prm["mu"], prm["w0"], prm["a0"], prm["kk"], prm["ka"], prm["rk"], prm["ln_w"], prm["ln_b"],
      prm["w2"], prm["a2"], prm["g2"])


def _gdn_kernel(qkv_ref, z_ref, ba_ref, buf_ref, s0_ref, convw_ref, alog_ref, dtb_ref, nw_ref,
                o_ref, s_out_ref, s_scr, x_scr, *, chunk, n_chunks, n_valid):
    c_idx = pl.program_id(1)
    rows = chunk * n_chunks
    pad = SUBLANES

    @pl.when(c_idx == 0)
    def _():
        s_scr[...] = s0_ref[0]
        x_scr[0:pad, :] = jnp.zeros((pad, GDN_QKV), F32)
        x_scr[pad - (GDN_CONV - 1):pad, :] = buf_ref[0]

    x_scr[pad:pad + rows, :] = qkv_ref[0]
    conv = None
    for j in range(GDN_CONV):
        term = x_scr[pad - j:pad - j + rows, :] * convw_ref[GDN_CONV - 1 - j:GDN_CONV - j, :]
        conv = term if conv is None else conv + term
    last = rows - chunk + n_valid
    x_scr[0:pad, :] = x_scr[last:last + pad, :]
    act = _silu(conv)
    q = act[:, 0:GROUP_WIDTH]
    k = act[:, GROUP_WIDTH:2 * GROUP_WIDTH]
    v = act[:, 2 * GROUP_WIDTH:3 * GROUP_WIDTH]
    ones = _head_ones(GROUP_WIDTH)
    q = q * lax.rsqrt(jnp.maximum(_head_sum(q * q, ones), 1e-12)) * (HEAD_DIM ** -0.5)
    k = k * lax.rsqrt(jnp.maximum(_head_sum(k * k, ones), 1e-12))

    ba = ba_ref[0]
    row = lax.broadcasted_iota(jnp.int32, (rows, LANES), 0)
    beta = jax.nn.sigmoid(ba)
    glog = -jnp.exp(alog_ref[...]) * _softplus(ba + dtb_ref[...])
    if n_valid < chunk:
        beta = jnp.where(row < n_valid, beta, 0.0)
        glog = jnp.where(row < n_valid, glog, 0.0)
    gcum, gend = _chunk_sums(glog, chunk, rows)
    gcum_t = gcum.T
    gam_all = jnp.exp(gcum)
    tail_all = jnp.exp(gend - gcum)
    end_all = jnp.exp(gend)

    _, _, strict, incl = _tri_masks(chunk)
    tiles = [(slice(c * chunk, (c + 1) * chunk), h) for c in range(n_chunks) for h in range(N_HEADS)]
    head = lambda x: [x[rs, h * HEAD_DIM:(h + 1) * HEAD_DIM] for rs, h in tiles]
    gcol = lambda x: [x[rs, N_HEADS + h:N_HEADS + h + 1] for rs, h in tiles]
    k_h, q_h, v_h = head(k), head(q), head(v)
    g_col, gam, tail = gcol(gcum), gcol(gam_all), gcol(tail_all)
    g_row = [gcum_t[N_HEADS + h:N_HEADS + h + 1, rs] for rs, h in tiles]
    beta_col = [beta[rs, h:h + 1] for rs, h in tiles]
    decay = _each(lambda gc, gr: jnp.where(incl, jnp.exp(jnp.where(incl, gc - gr, 0.0)), 0.0), g_col, g_row)
    gram = _each(lambda k_, q_: _mm_nt(jnp.concatenate([k_, q_], axis=0), k_), k_h, q_h)
    a_mat = _each(lambda b_, d_, g_: jnp.where(strict, b_ * d_ * g_[:chunk], 0.0), beta_col, decay, gram)
    tinv = _tri_inv(a_mat, chunk)
    tt = _each(lambda t_, b_, g_, k_, v_: _mm(t_, jnp.concatenate([(b_ * g_) * k_, b_ * v_], axis=1)),
               tinv, beta_col, gam, k_h, v_h)
    qo = _each(lambda g_, d_, t_: _mm(g_[chunk:] * d_, t_), gram, decay, tt)
    pq = _each(lambda t_, k_, e_: _mm_tn(t_, k_ * e_), tt, k_h, tail)
    qq = _each(lambda g_, q_, x: g_ * q_ - x[:, :HEAD_DIM], gam, q_h, qo)
    o0 = [x[:, HEAD_DIM:] for x in qo]
    pb = [x[:HEAD_DIM] for x in pq]
    q_mat = [x[HEAD_DIM:] for x in pq]

    state = [s_scr[h] for h in range(N_HEADS)]
    o_rows = []
    for c in range(n_chunks):
        i0 = c * N_HEADS
        os_ = _each(lambda o_, q_, s_: o_ + _mm_nt(q_, s_), o0[i0:i0 + N_HEADS], qq[i0:i0 + N_HEADS], state)
        ge = [end_all[c * chunk:c * chunk + 1, N_HEADS + h:N_HEADS + h + 1] for h in range(N_HEADS)]
        state = _each(lambda s_, g_, p_, q_: s_ * g_ - _mm(s_, p_) + q_, state, ge, pb[i0:i0 + N_HEADS],
                      q_mat[i0:i0 + N_HEADS])
        o_rows.append(jnp.concatenate(os_, axis=1))
    for h in range(N_HEADS):
        s_scr[h] = state[h]
    o = o_rows[0] if n_chunks == 1 else jnp.concatenate(o_rows, axis=0)
    ms = _head_sum(o * o, ones) * (1.0 / HEAD_DIM)
    o_ref[0] = o * lax.rsqrt(ms + NORM_EPS) * nw_ref[...] * _silu(z_ref[0])

    @pl.when(c_idx == pl.num_programs(1) - 1)
    def _():
        s_out_ref[0] = s_scr[...]


def _gdn(qkv, z, ba, buf, s0, prm, chunk, n_chunks, n_valid):
    bsz, seq, _ = qkv.shape
    rows = chunk * n_chunks
    assert n_valid == chunk or n_chunks == 1
    state = pl.BlockSpec((1, N_HEADS, HEAD_DIM, HEAD_DIM), lambda b, c: (b, 0, 0, 0))
    blk = lambda w: pl.BlockSpec((1, rows, w), lambda b, c: (b, c, 0))
    return pl.pallas_call(
        functools.partial(_gdn_kernel, chunk=chunk, n_chunks=n_chunks, n_valid=n_valid),
        grid=(bsz, seq // rows),
        in_specs=[blk(GDN_QKV), blk(GROUP_WIDTH), blk(LANES),
                  pl.BlockSpec((1, GDN_CONV - 1, GDN_QKV), lambda b, c: (b, 0, 0)), state,
                  _const_spec((GDN_CONV, GDN_QKV)), _const_spec((1, LANES)), _const_spec((1, LANES)),
                  _const_spec((1, GROUP_WIDTH))],
        out_specs=[blk(GROUP_WIDTH), state],
        out_shape=[jax.ShapeDtypeStruct((bsz, seq, GROUP_WIDTH), F32), jax.ShapeDtypeStruct(s0.shape, F32)],
        scratch_shapes=[pltpu.VMEM((N_HEADS, HEAD_DIM, HEAD_DIM), F32),
                        pltpu.VMEM((SUBLANES + rows, GDN_QKV), F32)],
        compiler_params=pltpu.CompilerParams(dimension_semantics=("arbitrary", "arbitrary")),
        name="gdn",
    )(qkv, z, ba, buf, s0, prm["conv_w"], prm["a_log"], prm["dt_bias"], prm["norm_w"])


def _gelu_tanh(x):
    return 0.5 * x * (1.0 + jnp.tanh(math.sqrt(2.0 / math.pi) * (x + 0.044715 * (x * x * x))))


def _s5_kernel(u_ref, x0re_ref, x0im_ref, are_ref, aim_ref, ldt_ref, bre_ref, bim_ref, cre_ref, cim_ref, d_ref,
               wglu_ref, bglu_ref, o_ref, hre_out_ref, him_out_ref, hre_scr, him_scr, sre_scr, sim_scr, *, steps, rows):
    c_idx = pl.program_id(1)

    @pl.when(c_idx == 0)
    def _():
        sre_scr[...] = x0re_ref[...]
        sim_scr[...] = x0im_ref[...]

    a_re, a_im = are_ref[...], aim_ref[...]
    dt = jnp.exp(ldt_ref[...])
    mag = jnp.exp(dt * a_re)
    ab_re = mag * jnp.cos(dt * a_im)
    ab_im = mag * jnp.sin(dt * a_im)
    den = a_re * a_re + a_im * a_im
    nr = ab_re - 1.0
    cf_re = (nr * a_re + ab_im * a_im) / den
    cf_im = (ab_im * a_re - nr * a_im) / den

    u = u_ref[...].reshape(steps * rows, GROUP_WIDTH)
    ub = u.astype(BF16)
    bu_re = _mm(ub, bre_ref[...])
    bu_im = _mm(ub, bim_ref[...])
    hre_scr[...] = cf_re * bu_re - cf_im * bu_im
    him_scr[...] = cf_re * bu_im + cf_im * bu_re

    abr = jnp.broadcast_to(ab_re, (rows, S5_LANES))
    abi = jnp.broadcast_to(ab_im, (rows, S5_LANES))

    def step(t, carry):
        h_re, h_im = carry
        base = pl.multiple_of(t * rows, rows)
        n_re = abr * h_re - abi * h_im + hre_scr[pl.ds(base, rows), :]
        n_im = abr * h_im + abi * h_re + him_scr[pl.ds(base, rows), :]
        hre_scr[pl.ds(base, rows), :] = n_re
        him_scr[pl.ds(base, rows), :] = n_im
        return n_re, n_im

    h_re, h_im = lax.fori_loop(0, steps, step, (sre_scr[...], sim_scr[...]))
    sre_scr[...] = h_re
    sim_scr[...] = h_im

    y = (_mm(hre_scr[...].astype(BF16), cre_ref[...]) - _mm(him_scr[...].astype(BF16), cim_ref[...])
         + d_ref[...] * u)
    zz = _gelu_tanh(y)
    out = zz * jax.nn.sigmoid(_mm(zz.astype(BF16), wglu_ref[...]) + bglu_ref[...])
    o_ref[...] = out.reshape(steps, rows, GROUP_WIDTH)

    @pl.when(c_idx == pl.num_programs(1) - 1)
    def _():
        hre_out_ref[...] = h_re
        him_out_ref[...] = h_im


def _s5(u_t, x0_re, x0_im, prm, steps, rows):
    seq, bsz, _ = u_t.shape
    st = pl.BlockSpec((rows, S5_LANES), lambda b, c: (b, 0))
    vec = _const_spec((1, S5_LANES))
    return pl.pallas_call(
        functools.partial(_s5_kernel, steps=steps, rows=rows),
        grid=(bsz // rows, seq // steps),
        in_specs=[pl.BlockSpec((steps, rows, GROUP_WIDTH), lambda b, c: (c, b, 0)), st, st, vec, vec, vec,
                  _const_spec((GROUP_WIDTH, S5_LANES)), _const_spec((GROUP_WIDTH, S5_LANES)),
                  _const_spec((S5_LANES, GROUP_WIDTH)), _const_spec((S5_LANES, GROUP_WIDTH)),
                  _const_spec((1, GROUP_WIDTH)), _const_spec((GROUP_WIDTH, GROUP_WIDTH)),
                  _const_spec((1, GROUP_WIDTH))],
        out_specs=[pl.BlockSpec((steps, rows, GROUP_WIDTH), lambda b, c: (c, b, 0)), st, st],
        out_shape=[jax.ShapeDtypeStruct((seq, bsz, GROUP_WIDTH), F32),
                   jax.ShapeDtypeStruct((bsz, S5_LANES), F32), jax.ShapeDtypeStruct((bsz, S5_LANES), F32)],
        scratch_shapes=[pltpu.VMEM((steps * rows, S5_LANES), F32), pltpu.VMEM((steps * rows, S5_LANES), F32),
                        pltpu.VMEM((rows, S5_LANES), F32), pltpu.VMEM((rows, S5_LANES), F32)],
        compiler_params=pltpu.CompilerParams(dimension_semantics=("arbitrary", "arbitrary"),
                                             vmem_limit_bytes=VMEM_LIMIT),
        name="s5",
    )(u_t, x0_re, x0_im, prm["a_re"], prm["a_im"], prm["log_dt"], prm["b_re"], prm["b_im"], prm["c_re"],
      prm["c_im"], prm["d"], prm["w_glu"], prm["b_glu"])


def _rope(x, cos, sin, width):
    fwd = pltpu.roll(x, width - ROPE_DIM // 2, 1)
    bwd = pltpu.roll(x, ROPE_DIM // 2, 1)
    lane = lax.broadcasted_iota(jnp.int32, x.shape, 1) % HEAD_DIM
    return x * cos + jnp.where(lane < ROPE_DIM // 2, fwd, bwd) * sin


def _swa_kernel(p_ref, cos_ref, sin_ref, kbuf_ref, vbuf_ref, sink_ref, o_ref, krot_ref, k_scr, v_scr,
                *, qb, start, carry):
    j = pl.program_id(1)
    wb = WINDOW

    @pl.when(j == 0)
    def _():
        k_scr[0:wb, :] = kbuf_ref[0]
        v_scr[0:wb, :] = vbuf_ref[0]
        if qb < wb:
            k_scr[wb:2 * wb, :] = jnp.zeros((wb, SWA_KV_WIDTH), F32)
            v_scr[wb:2 * wb, :] = jnp.zeros((wb, SWA_KV_WIDTH), F32)

    p = p_ref[0]
    cos, sin = cos_ref[...], sin_ref[...]
    q = _rope(p[:, 0:GROUP_WIDTH], cos, sin, GROUP_WIDTH)
    k = _rope(p[:, GROUP_WIDTH:GROUP_WIDTH + SWA_KV_WIDTH], cos[:, 0:SWA_KV_WIDTH], sin[:, 0:SWA_KV_WIDTH],
              SWA_KV_WIDTH)
    v = p[:, GROUP_WIDTH + SWA_KV_WIDTH:SWA_PROJ]
    krot_ref[0] = k
    k_scr[wb:wb + qb, :] = k
    v_scr[wb:wb + qb, :] = v
    keys = k_scr[...]
    vals = v_scr[...]

    nk = 2 * wb
    rq = lax.broadcasted_iota(jnp.int32, (2 * qb, nk), 0) % qb
    ck = lax.broadcasted_iota(jnp.int32, (2 * qb, nk), 1)
    rel = rq + wb - ck
    kpos = start + j * qb - wb + ck
    valid = (rel >= 0) & (rel < WINDOW) & (kpos >= 0)
    upper = lax.broadcasted_iota(jnp.int32, (2 * qb, 1), 0) < qb
    sinks = sink_ref[...]

    outs = []
    for h in range(SWA_KV_HEADS):
        q2 = jnp.concatenate([q[:, (2 * h) * HEAD_DIM:(2 * h + 1) * HEAD_DIM],
                              q[:, (2 * h + 1) * HEAD_DIM:(2 * h + 2) * HEAD_DIM]], axis=0)
        kh = keys[:, h * HEAD_DIM:(h + 1) * HEAD_DIM]
        vh = vals[:, h * HEAD_DIM:(h + 1) * HEAD_DIM]
        s = jnp.where(valid, _mm_nt(q2, kh) * (HEAD_DIM ** -0.5), -jnp.inf)
        sink = jnp.where(upper, sinks[:, 2 * h:2 * h + 1], sinks[:, 2 * h + 1:2 * h + 2])
        m = jnp.maximum(jnp.max(s, axis=-1, keepdims=True), sink)
        e = jnp.exp(s - m)
        den = jnp.sum(e, axis=-1, keepdims=True) + jnp.exp(sink - m)
        o2 = _mm(e, vh) / den
        outs += [o2[:qb], o2[qb:]]
    o_ref[0] = jnp.concatenate(outs, axis=1)

    if carry:
        k_scr[0:wb, :] = k
        v_scr[0:wb, :] = v


def _swa(p, cos, sin, kbuf, vbuf, sinks, qb, start):
    bsz, seq, _ = p.shape
    nb = seq // qb
    assert nb == 1 or qb == WINDOW
    blk = lambda w: pl.BlockSpec((1, qb, w), lambda b, j: (b, j, 0))
    tab = pl.BlockSpec((qb, GROUP_WIDTH), lambda b, j: (j, 0))
    buf = pl.BlockSpec((1, WINDOW, SWA_KV_WIDTH), lambda b, j: (b, 0, 0))
    return pl.pallas_call(
        functools.partial(_swa_kernel, qb=qb, start=start, carry=nb > 1),
        grid=(bsz, nb),
        in_specs=[blk(SWA_PROJ), tab, tab, buf, buf, _const_spec((1, LANES))],
        out_specs=[blk(GROUP_WIDTH), pl.BlockSpec((1, qb, SWA_KV_WIDTH), lambda b, j: (b, 0, 0))],
        out_shape=[jax.ShapeDtypeStruct((bsz, seq, GROUP_WIDTH), F32),
                   jax.ShapeDtypeStruct((bsz, qb, SWA_KV_WIDTH), F32)],
        scratch_shapes=[pltpu.VMEM((2 * WINDOW, SWA_KV_WIDTH), F32), pltpu.VMEM((2 * WINDOW, SWA_KV_WIDTH), F32)],
        compiler_params=pltpu.CompilerParams(dimension_semantics=("arbitrary", "arbitrary")),
        name="swa",
    )(p, cos, sin, kbuf, vbuf, sinks)


def _rope_tables(start, seq):
    half = ROPE_DIM // 2
    inv = ROPE_THETA ** (-jnp.arange(0, ROPE_DIM, 2, dtype=F32) / ROPE_DIM)
    ang = (start + jnp.arange(seq)).astype(F32)[:, None] * inv[None, :]
    cos, sin = jnp.cos(ang), jnp.sin(ang)
    rest = HEAD_DIM - ROPE_DIM
    cos_h = jnp.concatenate([cos, cos, jnp.ones((seq, rest), F32)], axis=1)
    sin_h = jnp.concatenate([-sin, sin, jnp.zeros((seq, rest), F32)], axis=1)
    return jnp.tile(cos_h, (1, N_HEADS)), jnp.tile(sin_h, (1, N_HEADS))


def _pad_rows(w, top, total):
    return jnp.pad(w, ((top, total - top - w.shape[0]), (0, 0)))


def _block_diag_in(b):
    eye = jnp.eye(S5_GROUPS, dtype=b.dtype)
    return jnp.einsum("gnc,gh->gchn", b, eye).reshape(S5_GROUPS * S5_CH, S5_GROUPS * S5_STATE)


def _block_diag_out(c):
    eye = jnp.eye(S5_GROUPS, dtype=c.dtype)
    return jnp.einsum("gcn,gh->gnhc", c, eye).reshape(S5_GROUPS * S5_STATE, S5_GROUPS * S5_CH)


def _layer_params(l, g_mix_pre, g_mix_post, g_mlp_pre, g_mlp_post, w_in, w_out, rw_mu, rw_w0, rw_w2, rw_a0, rw_a2,
                  rw_g2, rw_kk, rw_ka, rw_rk, rw_ln_w, rw_ln_b, s5_a_re, s5_a_im, s5_log_dt, s5_b_re, s5_b_im,
                  s5_c_re, s5_c_im, s5_d, s5_w_glu, s5_b_glu, gdn_conv_w, gdn_a_log, gdn_dt_bias, gdn_norm_w,
                  swa_sinks, w_up, w_down):
    row = lambda a: a[l].astype(F32).reshape(1, -1)
    wi = w_in[l].astype(F32)
    o_s5 = RW_PROJ
    o_gdn = o_s5 + GROUP_WIDTH
    o_ba = o_gdn + GDN_QKV
    o_z = o_ba + 2 * N_HEADS
    o_swa = o_z + GROUP_WIDTH
    w_all = jnp.concatenate([
        wi[:, :o_s5], wi[:, o_s5:o_gdn], wi[:, o_gdn:o_ba], wi[:, o_z:o_swa],
        jnp.pad(wi[:, o_ba:o_z], ((0, 0), (0, LANES - 2 * N_HEADS))), wi[:, o_swa:]], axis=1).astype(BF16)
    lane_pad = lambda a: jnp.pad(a[l].astype(F32), (N_HEADS, LANES - 2 * N_HEADS)).reshape(1, LANES)
    return {
        "g_mix_pre": row(g_mix_pre), "g_mix_post": row(g_mix_post), "g_mlp_pre": row(g_mlp_pre),
        "g_mlp_post": row(g_mlp_post), "w_all": w_all, "w_out": w_out[l].astype(BF16),
        "w_up": w_up[l].astype(BF16), "w_down": w_down[l].astype(BF16),
        "rw": {"mu": row(rw_mu), "w0": row(rw_w0), "a0": row(rw_a0), "kk": row(rw_kk), "ka": row(rw_ka),
               "rk": row(rw_rk), "ln_w": row(rw_ln_w), "ln_b": row(rw_ln_b),
               "w2": _pad_rows(rw_w2[l].astype(F32), 0, GROUP_WIDTH).astype(BF16),
               "a2": _pad_rows(rw_a2[l].astype(F32), RW_DECAY_RANK, GROUP_WIDTH).astype(BF16),
               "g2": _pad_rows(rw_g2[l].astype(F32), RW_DECAY_RANK + RW_ICLR_RANK, GROUP_WIDTH).astype(BF16)},
        "s5": {"a_re": row(s5_a_re), "a_im": row(s5_a_im),
               "log_dt": jnp.repeat(s5_log_dt[l].astype(F32), S5_STATE).reshape(1, S5_LANES),
               "b_re": _block_diag_in(s5_b_re[l].astype(F32)).astype(BF16),
               "b_im": _block_diag_in(s5_b_im[l].astype(F32)).astype(BF16),
               "c_re": _block_diag_out(s5_c_re[l].astype(F32)).astype(BF16),
               "c_im": _block_diag_out(s5_c_im[l].astype(F32)).astype(BF16),
               "d": row(s5_d), "w_glu": s5_w_glu[l].astype(BF16), "b_glu": row(s5_b_glu)},
        "gdn": {"conv_w": gdn_conv_w[l].astype(F32), "a_log": lane_pad(gdn_a_log), "dt_bias": lane_pad(gdn_dt_bias),
                "norm_w": jnp.tile(gdn_norm_w[l].astype(F32), N_HEADS).reshape(1, GROUP_WIDTH)},
        "sinks": jnp.pad(swa_sinks[l].astype(F32), (0, LANES - N_HEADS)).reshape(1, LANES),
    }


def _layer(x, start, st, prm):
    rw_s, rw_shift, s5_re, s5_im, gdn_s, gdn_conv, swa_k, swa_v = st
    bsz, seq, _ = x.shape
    tokens = bsz * seq
    tm = min(512, tokens)
    chunk = CHUNK if seq % CHUNK == 0 else -(-seq // SUBLANES) * SUBLANES
    seq_pad = -(-seq // chunk) * chunk
    n_valid = chunk - (seq_pad - seq)
    n_chunks = math.gcd(seq_pad // chunk, MIX_CHUNKS)
    assert seq_pad == seq or seq_pad == chunk
    assert seq >= GDN_CONV - 1

    x2d = x.reshape(tokens, D_MODEL)
    p_rw, p_s5, p_qkv, p_z, p_ba, p_swa = (
        a.reshape(bsz, seq, -1) for a in _in_proj(x2d, prm["g_mix_pre"], prm["w_all"], tm))
    pad = lambda a: a if seq_pad == seq else jnp.pad(a, ((0, 0), (0, seq_pad - seq), (0, 0)))

    o_rw, rw_s_new = _rwkv(pad(p_rw), rw_shift, rw_s, prm["rw"], chunk, n_chunks, n_valid)
    rw_shift_new = p_rw[:, seq - 1:, :]

    rows = 16
    steps = min(seq, 64)
    o_s5_t, s5_re_new, s5_im_new = _s5(jnp.swapaxes(p_s5, 0, 1), s5_re.reshape(bsz, S5_LANES),
                                       s5_im.reshape(bsz, S5_LANES), prm["s5"], steps, rows)
    o_s5 = jnp.swapaxes(o_s5_t, 0, 1)

    o_gdn, gdn_s_new = _gdn(pad(p_qkv), pad(p_z), pad(p_ba), gdn_conv, gdn_s, prm["gdn"], chunk, n_chunks,
                            n_valid)
    gdn_conv_new = p_qkv[:, seq - (GDN_CONV - 1):, :]

    qb = WINDOW if seq % WINDOW == 0 else seq_pad
    cos, sin = _rope_tables(start, seq_pad)
    wbuf = swa_k.shape[1]
    o_swa, k_rot = _swa(pad(p_swa), cos, sin, swa_k.reshape(bsz, wbuf, SWA_KV_WIDTH),
                        swa_v.reshape(bsz, wbuf, SWA_KV_WIDTH), prm["sinks"], qb, start)
    k_new = k_rot[:, :qb - (seq_pad - seq), :] if seq_pad != seq else k_rot
    v_new = p_swa[:, seq - min(seq, wbuf):, GROUP_WIDTH + SWA_KV_WIDTH:]
    kv_shape = (bsz, -1, SWA_KV_HEADS, HEAD_DIM)
    if k_new.shape[1] >= wbuf:
        swa_k_new = k_new[:, -wbuf:].reshape(kv_shape)
        swa_v_new = v_new[:, -wbuf:].reshape(kv_shape)
    else:
        swa_k_new = jnp.concatenate([swa_k, k_new.reshape(kv_shape)], axis=1)[:, -wbuf:]
        swa_v_new = jnp.concatenate([swa_v, v_new.reshape(kv_shape)], axis=1)[:, -wbuf:]

    flat = lambda a: a[:, :seq, :].reshape(tokens, GROUP_WIDTH)
    y = _post(x2d, (flat(o_rw), flat(o_s5), flat(o_gdn), flat(o_swa)), prm["w_out"], prm["g_mix_post"],
              prm["g_mlp_pre"], prm["w_up"], prm["w_down"], prm["g_mlp_post"], tm)
    new_state = (rw_s_new, rw_shift_new, s5_re_new.reshape(s5_re.shape), s5_im_new.reshape(s5_im.shape),
                 gdn_s_new, gdn_conv_new, swa_k_new, swa_v_new)
    return y.reshape(bsz, seq, D_MODEL), new_state


def kernel(x_prompt, x_sample, state_rwkv, state_rwkv_shift, state_s5_re, state_s5_im, state_gdn, state_gdn_conv,
           cache_swa_k, cache_swa_v, g_mix_pre, g_mix_post, g_mlp_pre, g_mlp_post, w_in, w_out, rw_mu, rw_w0, rw_w2,
           rw_a0, rw_a2, rw_g2, rw_kk, rw_ka, rw_rk, rw_ln_w, rw_ln_b, s5_a_re, s5_a_im, s5_log_dt, s5_b_re, s5_b_im,
           s5_c_re, s5_c_im, s5_d, s5_w_glu, s5_b_glu, gdn_conv_w, gdn_a_log, gdn_dt_bias, gdn_norm_w, swa_sinks,
           w_up, w_down):
    weights = (g_mix_pre, g_mix_post, g_mlp_pre, g_mlp_post, w_in, w_out, rw_mu, rw_w0, rw_w2, rw_a0, rw_a2, rw_g2,
               rw_kk, rw_ka, rw_rk, rw_ln_w, rw_ln_b, s5_a_re, s5_a_im, s5_log_dt, s5_b_re, s5_b_im, s5_c_re, s5_c_im,
               s5_d, s5_w_glu, s5_b_glu, gdn_conv_w, gdn_a_log, gdn_dt_bias, gdn_norm_w, swa_sinks, w_up, w_down)
    depth = w_in.shape[0]
    bp = x_prompt.shape[0]
    past_len = 16384
    xp = x_prompt.astype(F32)
    xs = x_sample.astype(F32)
    zp = lambda *shape: jnp.zeros((bp,) + shape, F32)
    prompt_init = (zp(N_HEADS, HEAD_DIM, HEAD_DIM), zp(1, RW_PROJ), zp(S5_GROUPS, S5_STATE), zp(S5_GROUPS, S5_STATE),
                   zp(N_HEADS, HEAD_DIM, HEAD_DIM), zp(GDN_CONV - 1, GDN_QKV),
                   zp(WINDOW, SWA_KV_HEADS, HEAD_DIM), zp(WINDOW, SWA_KV_HEADS, HEAD_DIM))
    sample_states = (state_rwkv, state_rwkv_shift, state_s5_re, state_s5_im, state_gdn, state_gdn_conv,
                     cache_swa_k, cache_swa_v)
    new_p, new_s = [], []
    for l in range(depth):
        prm = _layer_params(l, *weights)
        xp, st_p = _layer(xp, 0, prompt_init, prm)
        xs, st_s = _layer(xs, past_len, tuple(a[l].astype(F32) for a in sample_states), prm)
        new_p.append(st_p)
        new_s.append(st_s)
    outs = [xp.astype(x_prompt.dtype), xs.astype(x_sample.dtype)]
    for i in range(len(sample_states)):
        outs.append(jnp.stack([st[i] for st in new_p], axis=0))
        outs.append(jnp.stack([st[i] for st in new_s], axis=0))
    return tuple(outs)
```

```python
import functools
import math

import jax
import jax.numpy as jnp
from jax import lax
from jax.experimental import pallas as pl
from jax.experimental.pallas import tpu as pltpu

F32 = jnp.float32
BF16 = jnp.bfloat16

D_MODEL = 1024
HEAD_DIM = 64
N_HEADS = 4
GROUP_WIDTH = N_HEADS * HEAD_DIM
NORM_EPS = 1e-6
RW_LN_EPS = 64e-5
RW_PROJ = 1024
RW_LORA_OFF = 3 * GROUP_WIDTH
RW_DECAY_RANK, RW_ICLR_RANK, RW_GATE_RANK = 64, 64, 128
S5_GROUPS, S5_CH, S5_STATE = 16, 16, 64
S5_LANES = S5_GROUPS * S5_STATE
GDN_CONV = 4
GDN_QKV = 3 * GROUP_WIDTH
SWA_KV_HEADS = 2
SWA_KV_WIDTH = SWA_KV_HEADS * HEAD_DIM
SWA_PROJ = GROUP_WIDTH + 2 * SWA_KV_WIDTH
WINDOW = 128
ROPE_DIM = 16
ROPE_THETA = 500000.0
D_FF = 4096
LANES = 128
SUBLANES = 8
CHUNK = 64
MIX_CHUNKS = 4
MIXER_BATCH = {"rwkv": (2, 16), "gdn": (1, 16), "swa": (4, 32)}
TRI_BLOCK = 16
VMEM_LIMIT = 56 * 1024 * 1024

PROJ_WIDTHS = (RW_PROJ, GROUP_WIDTH, GDN_QKV, GROUP_WIDTH, LANES, SWA_PROJ)


def _dot(a, b, dims):
    return lax.dot_general(a.astype(BF16), b.astype(BF16), (dims, ((), ())), preferred_element_type=F32)


def _mm(a, b):
    return _dot(a, b, ((1,), (0,)))


def _mm_nt(a, b):
    return _dot(a, b, ((1,), (1,)))


def _mm_tn(a, b):
    return _dot(a, b, ((0,), (0,)))


def _split3(x):
    hi = x.astype(BF16)
    r1 = x - hi.astype(F32)
    mid = r1.astype(BF16)
    lo = (r1 - mid.astype(F32)).astype(BF16)
    return hi, mid, lo


def _mm_exact_rhs(x, m):
    hi, mid, lo = _split3(x)
    return _mm(hi, m) + _mm(mid, m) + _mm(lo, m)


def _mm_exact_lhs(m, x):
    hi, mid, lo = _split3(x)
    return _mm(m, hi) + _mm(m, mid) + _mm(m, lo)


def _head_ones(width):
    r = lax.broadcasted_iota(jnp.int32, (width, width), 0)
    c = lax.broadcasted_iota(jnp.int32, (width, width), 1)
    return (r // HEAD_DIM == c // HEAD_DIM).astype(BF16)


def _head_sum(x, ones):
    return _mm_exact_rhs(x, ones)


def _tri_masks(n):
    r = lax.broadcasted_iota(jnp.int32, (n, n), 0)
    c = lax.broadcasted_iota(jnp.int32, (n, n), 1)
    return r, c, r > c, r >= c


def _softplus(x):
    return jnp.maximum(x, 0.0) + jnp.log1p(jnp.exp(-jnp.abs(x)))


def _silu(x):
    return x * jax.nn.sigmoid(x)


def _rms(x, g):
    return x * lax.rsqrt(jnp.mean(x * x, axis=-1, keepdims=True) + NORM_EPS) * g


def _each(fn, *lists):
    return [fn(*xs) for xs in zip(*lists)]


def _tri_inv(lows, n):
    r, c, _, _ = _tri_masks(n)
    eye = (r == c).astype(F32)
    bs = min(TRI_BLOCK, n)
    if n > bs:
        same = (r // bs) == (c // bs)
        diags = [jnp.where(same, low, 0.0) for low in lows]
        rests = _each(lambda low, d: low - d, lows, diags)
    else:
        diags, rests = lows, None
    invs = [eye - d for d in diags]
    pws, p = diags, 1
    while 2 * p < bs:
        pws = _each(lambda x: _mm(x, x), pws)
        invs = _each(lambda i, x: i + _mm(i, x), invs, pws)
        p *= 2
    if rests is None:
        return invs
    nils = _each(_mm, invs, rests)
    outs = [eye - x for x in nils]
    pws, p = nils, 1
    while 2 * p < n // bs:
        pws = _each(lambda x: _mm(x, x), pws)
        outs = _each(lambda o, x: o + _mm(o, x), outs, pws)
        p *= 2
    return _each(_mm, outs, invs)


def _const_spec(shape):
    zeros = (0,) * len(shape)
    return pl.BlockSpec(shape, lambda *_: zeros, pipeline_mode=pl.Buffered(1))


def _in_proj_kernel(x_ref, g_ref, w_ref, *out_refs):
    h = _rms(x_ref[...], g_ref[...]).astype(BF16)
    off = 0
    for o_ref, width in zip(out_refs, PROJ_WIDTHS):
        o_ref[...] = _mm(h, w_ref[:, off:off + width])
        off += width


def _in_proj(x2d, g, w_all, tm):
    t = x2d.shape[0]
    wtot = sum(PROJ_WIDTHS)
    return pl.pallas_call(
        _in_proj_kernel,
        grid=(t // tm,),
        in_specs=[pl.BlockSpec((tm, D_MODEL), lambda i: (i, 0)),
                  _const_spec((1, D_MODEL)),
                  _const_spec((D_MODEL, wtot))],
        out_specs=[pl.BlockSpec((tm, w), lambda i: (i, 0)) for w in PROJ_WIDTHS],
        out_shape=[jax.ShapeDtypeStruct((t, w), F32) for w in PROJ_WIDTHS],
        compiler_params=pltpu.CompilerParams(dimension_semantics=("arbitrary",), vmem_limit_bytes=VMEM_LIMIT),
        name="in_proj",
    )(x2d, g, w_all)


FF_CHUNK = 1024


def _post_kernel(x_ref, o0_ref, o1_ref, o2_ref, o3_ref, wout_ref, gpost_ref, gpre_ref, wup_ref, wdn_ref,
                 gmlp_ref, y_ref):
    mix = None
    for i, o_ref in enumerate((o0_ref, o1_ref, o2_ref, o3_ref)):
        part = _mm(o_ref[...].astype(BF16), wout_ref[i * GROUP_WIDTH:(i + 1) * GROUP_WIDTH, :])
        mix = part if mix is None else mix + part
    x1 = x_ref[...] + _rms(mix, gpost_ref[...])
    h = _rms(x1, gpre_ref[...]).astype(BF16)
    acc = None
    for c in range(D_FF // FF_CHUNK):
        u = _mm(h, wup_ref[:, c * FF_CHUNK:(c + 1) * FF_CHUNK])
        u = jnp.square(jnp.maximum(u, 0.0)).astype(BF16)
        part = _mm(u, wdn_ref[c * FF_CHUNK:(c + 1) * FF_CHUNK, :])
        acc = part if acc is None else acc + part
    y_ref[...] = x1 + _rms(acc, gmlp_ref[...])


def _post(x2d, outs, w_out, g_post, g_pre, w_up, w_dn, g_mlp, tm):
    t = x2d.shape[0]
    row = lambda w: pl.BlockSpec((tm, w), lambda i: (i, 0))
    return pl.pallas_call(
        _post_kernel,
        grid=(t // tm,),
        in_specs=[row(D_MODEL)] + [row(GROUP_WIDTH)] * 4 + [
            _const_spec((D_MODEL, D_MODEL)), _const_spec((1, D_MODEL)), _const_spec((1, D_MODEL)),
            _const_spec((D_MODEL, D_FF)), _const_spec((D_FF, D_MODEL)), _const_spec((1, D_MODEL))],
        out_specs=row(D_MODEL),
        out_shape=jax.ShapeDtypeStruct((t, D_MODEL), F32),
        compiler_params=pltpu.CompilerParams(dimension_semantics=("arbitrary",), vmem_limit_bytes=VMEM_LIMIT),
        name="post_mlp",
    )(x2d, *outs, w_out, g_post, g_pre, w_up, w_dn, g_mlp)


def _chunk_sums(x, chunk, rows):
    r = lax.broadcasted_iota(jnp.int32, (rows, rows), 0)
    c = lax.broadcasted_iota(jnp.int32, (rows, rows), 1)
    same = (r // chunk) == (c // chunk)
    cum = _mm_exact_lhs((same & (r >= c)).astype(BF16), x)
    tot = _mm_exact_lhs(same.astype(BF16), x) if rows > chunk else cum[rows - 1:rows, :]
    return cum, tot


def _rwkv_kernel(p_ref, shift_ref, s0_ref, mu_ref, w0_ref, a0_ref, kkp_ref, kap_ref, rk_ref, lnw_ref, lnb_ref,
                 w2_ref, a2_ref, g2_ref, o_ref, s_out_ref, s_scr, carry_scr, *, bb, chunk, n_chunks, n_valid):
    c_idx = pl.program_id(1)
    rows = chunk * n_chunks
    total = bb * rows

    @pl.when(c_idx == 0)
    def _():
        s_scr[...] = s0_ref[...]
        carry_scr[...] = shift_ref[...]

    first = lax.broadcasted_iota(jnp.int32, (rows, 1), 0) == 0
    last = rows - chunk + n_valid
    prevs = []
    for bi in range(bb):
        p_b = p_ref[bi]
        prevs.append(jnp.where(first, carry_scr[bi], pltpu.roll(p_b, 1, 0)))
        carry_scr[bi] = p_b[last - 1:last, :]
    p = p_ref[...].reshape(total, RW_PROJ)
    prev = prevs[0] if bb == 1 else jnp.concatenate(prevs, axis=0)
    row = lax.broadcasted_iota(jnp.int32, (total, 1), 0)
    xs = p + (prev - p) * mu_ref[...]
    r = xs[:, 0:GROUP_WIDTH]
    k = xs[:, GROUP_WIDTH:2 * GROUP_WIDTH]
    v = xs[:, 2 * GROUP_WIDTH:3 * GROUP_WIDTH]
    lora = xs[:, RW_LORA_OFF:RW_PROJ]

    z_w = w0_ref[...] + _mm(jnp.tanh(lora), w2_ref[...])
    logw = -jnp.exp(-_softplus(-z_w) - 0.5)
    a = jax.nn.sigmoid(a0_ref[...] + _mm(lora, a2_ref[...]))
    gate = _mm(jax.nn.sigmoid(lora), g2_ref[...])

    ones = _head_ones(GROUP_WIDTH)
    kk = k * kkp_ref[...]
    kk = kk * lax.rsqrt(jnp.maximum(_head_sum(kk * kk, ones), 1e-12))
    k2 = k * (1.0 + (a - 1.0) * kap_ref[...])
    b = kk * a
    if n_valid < chunk:
        valid = (row % chunk) < n_valid
        logw = jnp.where(valid, logw, 0.0)
        kk = jnp.where(valid, kk, 0.0)
        b = jnp.where(valid, b, 0.0)
        k2v = jnp.where(valid, k2, 0.0)
        vv = jnp.where(valid, v, 0.0)
    else:
        k2v, vv = k2, v

    log_cum, log_end = _chunk_sums(logw, chunk, total)
    w_inv = jnp.exp(-log_cum)
    w_tail = jnp.exp(log_end - log_cum)
    a_bar = kk * jnp.exp(log_cum - logw)
    b_bar = b * w_inv
    k_bar = k2v * w_inv
    r_bar = r * jnp.exp(log_cum)
    b_end = b * w_tail
    k_end = k2v * w_tail
    w_end = jnp.exp(log_end)

    _, _, strict, incl = _tri_masks(chunk)
    group = bb * N_HEADS
    tiles = [(slice(bi * rows + c * chunk, bi * rows + (c + 1) * chunk), slice(h * HEAD_DIM, (h + 1) * HEAD_DIM))
             for c in range(n_chunks) for bi in range(bb) for h in range(N_HEADS)]
    cut = lambda x: [x[rs, sl] for rs, sl in tiles]
    ab, rb, bs, kb, be, ke, vh = map(cut, (a_bar, r_bar, b_bar, k_bar, b_end, k_end, vv))
    gram = _each(lambda a_, r_, b_, k_: _mm_nt(jnp.concatenate([a_, r_], axis=0), jnp.concatenate([b_, k_], axis=0)),
                 ab, rb, bs, kb)
    l_ab = [jnp.where(strict, g[:chunk, :chunk], 0.0) for g in gram]
    l_ak = [jnp.where(strict, g[:chunk, chunk:], 0.0) for g in gram]
    m_rb = [jnp.where(incl, g[chunk:, :chunk], 0.0) for g in gram]
    m_rk = [jnp.where(incl, g[chunk:, chunk:], 0.0) for g in gram]
    tinv = _tri_inv(l_ab, chunk)
    lv = _each(_mm, l_ak, vh)
    tt = _each(lambda t_, a_, l_: _mm(t_, jnp.concatenate([a_, l_], axis=1)), tinv, ab, lv)
    ta_c0 = [jnp.concatenate([t_[:, :HEAD_DIM], -t_[:, HEAD_DIM:]], axis=1) for t_ in tt]
    pq = _each(_mm_tn, ta_c0, be)
    vk = _each(_mm_tn, vh, ke)
    ry = _each(_mm, m_rb, ta_c0)
    rkv = _each(_mm, m_rk, vh)
    pb = [x[:HEAD_DIM] for x in pq]
    q_mat = _each(lambda x, y_: x[HEAD_DIM:] + y_, pq, vk)
    rr = _each(lambda r_, x: r_ - x[:, :HEAD_DIM], rb, ry)
    y0 = _each(lambda x, y_: x[:, HEAD_DIM:] + y_, ry, rkv)

    state = [s_scr[bi, h] for bi in range(bb) for h in range(N_HEADS)]
    y_tiles = {}
    for c in range(n_chunks):
        grp = slice(c * group, (c + 1) * group)
        ys = _each(lambda y_, r_, s_: y_ + _mm_nt(r_, s_), y0[grp], rr[grp], state)
        we = [w_end[rs.start:rs.start + 1, sl] for rs, sl in tiles[grp]]
        state = _each(lambda s_, w_, p_, q_: s_ * w_ - _mm(s_, p_) + q_, state, we, pb[grp], q_mat[grp])
        for bi in range(bb):
            y_tiles[bi, c] = jnp.concatenate(ys[bi * N_HEADS:(bi + 1) * N_HEADS], axis=1)
    for bi in range(bb):
        for h in range(N_HEADS):
            s_scr[bi, h] = state[bi * N_HEADS + h]
    y_rows = [y_tiles[bi, c] for bi in range(bb) for c in range(n_chunks)]
    y = y_rows[0] if len(y_rows) == 1 else jnp.concatenate(y_rows, axis=0)

    inv_d = 1.0 / HEAD_DIM
    mean = _head_sum(y, ones) * inv_d
    d = y - mean
    var = _head_sum(d * d, ones) * inv_d
    yn = d * lax.rsqrt(var + RW_LN_EPS) * lnw_ref[...] + lnb_ref[...]
    bonus = _head_sum(r * k2 * rk_ref[...], ones) * v
    o_ref[...] = ((yn + bonus) * gate).reshape(bb, rows, GROUP_WIDTH)

    @pl.when(c_idx == pl.num_programs(1) - 1)
    def _():
        s_out_ref[...] = s_scr[...]


def _rwkv(p, shift, s0, prm, bb, chunk, n_chunks, n_valid):
    bsz, seq, _ = p.shape
    rows = chunk * n_chunks
    assert n_valid == chunk or n_chunks == 1
    vec = _const_spec((1, GROUP_WIDTH))
    mat = _const_spec((GROUP_WIDTH, GROUP_WIDTH))
    state = pl.BlockSpec((bb, N_HEADS, HEAD_DIM, HEAD_DIM), lambda b, c: (b, 0, 0, 0))
    return pl.pallas_call(
        functools.partial(_rwkv_kernel, bb=bb, chunk=chunk, n_chunks=n_chunks, n_valid=n_valid),
        grid=(bsz // bb, seq // rows),
        in_specs=[pl.BlockSpec((bb, rows, RW_PROJ), lambda b, c: (b, c, 0)),
                  pl.BlockSpec((bb, 1, RW_PROJ), lambda b, c: (b, 0, 0)),
                  state, _const_spec((1, RW_PROJ))] + [vec] * 7 + [mat] * 3,
        out_specs=[pl.BlockSpec((bb, rows, GROUP_WIDTH), lambda b, c: (b, c, 0)), state],
        out_shape=[jax.ShapeDtypeStruct((bsz, seq, GROUP_WIDTH), F32),
                   jax.ShapeDtypeStruct(s0.shape, F32)],
        scratch_shapes=[pltpu.VMEM((bb, N_HEADS, HEAD_DIM, HEAD_DIM), F32), pltpu.VMEM((bb, 1, RW_PROJ), F32)],
        compiler_params=pltpu.CompilerParams(dimension_semantics=("arbitrary", "arbitrary")),
        name="rwkv7",
    )(p, shift, s0, prm["mu"], prm["w0"], prm["a0"], prm["kk"], prm["ka"], prm["rk"], prm["ln_w"], prm["ln_b"],
      prm["w2"], prm["a2"], prm["g2"])


def _gdn_kernel(qkv_ref, z_ref, ba_ref, buf_ref, s0_ref, convw_ref, alog_ref, dtb_ref, nw_ref,
                o_ref, s_out_ref, s_scr, x_scr, *, bb, chunk, n_chunks, n_valid):
    c_idx = pl.program_id(1)
    rows = chunk * n_chunks
    total = bb * rows
    pad = SUBLANES

    @pl.when(c_idx == 0)
    def _():
        s_scr[...] = s0_ref[...]
        x_scr[:, 0:pad, :] = jnp.zeros((bb, pad, GDN_QKV), F32)
        x_scr[:, pad - (GDN_CONV - 1):pad, :] = buf_ref[...]

    last = rows - chunk + n_valid
    convs = []
    for bi in range(bb):
        x_scr[bi, pad:pad + rows, :] = qkv_ref[bi]
        conv = None
        for j in range(GDN_CONV):
            term = x_scr[bi, pad - j:pad - j + rows, :] * convw_ref[GDN_CONV - 1 - j:GDN_CONV - j, :]
            conv = term if conv is None else conv + term
        x_scr[bi, 0:pad, :] = x_scr[bi, last:last + pad, :]
        convs.append(conv)
    act = _silu(convs[0] if bb == 1 else jnp.concatenate(convs, axis=0))
    q = act[:, 0:GROUP_WIDTH]
    k = act[:, GROUP_WIDTH:2 * GROUP_WIDTH]
    v = act[:, 2 * GROUP_WIDTH:3 * GROUP_WIDTH]
    ones = _head_ones(GROUP_WIDTH)
    q = q * lax.rsqrt(jnp.maximum(_head_sum(q * q, ones), 1e-12)) * (HEAD_DIM ** -0.5)
    k = k * lax.rsqrt(jnp.maximum(_head_sum(k * k, ones), 1e-12))

    ba = ba_ref[...].reshape(total, LANES)
    row = lax.broadcasted_iota(jnp.int32, (total, LANES), 0)
    beta = jax.nn.sigmoid(ba)
    glog = -jnp.exp(alog_ref[...]) * _softplus(ba + dtb_ref[...])
    if n_valid < chunk:
        valid = (row % chunk) < n_valid
        beta = jnp.where(valid, beta, 0.0)
        glog = jnp.where(valid, glog, 0.0)
    gcum, gend = _chunk_sums(glog, chunk, total)
    gcum_t = gcum.T
    gam_all = jnp.exp(gcum)
    tail_all = jnp.exp(gend - gcum)
    end_all = jnp.exp(gend)

    _, _, strict, incl = _tri_masks(chunk)
    group = bb * N_HEADS
    tiles = [(slice(bi * rows + c * chunk, bi * rows + (c + 1) * chunk), h)
             for c in range(n_chunks) for bi in range(bb) for h in range(N_HEADS)]
    head = lambda x: [x[rs, h * HEAD_DIM:(h + 1) * HEAD_DIM] for rs, h in tiles]
    gcol = lambda x: [x[rs, N_HEADS + h:N_HEADS + h + 1] for rs, h in tiles]
    k_h, q_h, v_h = head(k), head(q), head(v)
    g_col, gam, tail = gcol(gcum), gcol(gam_all), gcol(tail_all)
    g_row = [gcum_t[N_HEADS + h:N_HEADS + h + 1, rs] for rs, h in tiles]
    beta_col = [beta[rs, h:h + 1] for rs, h in tiles]
    decay = _each(lambda gc, gr: jnp.where(incl, jnp.exp(jnp.where(incl, gc - gr, 0.0)), 0.0), g_col, g_row)
    gram = _each(lambda k_, q_: _mm_nt(jnp.concatenate([k_, q_], axis=0), k_), k_h, q_h)
    a_mat = _each(lambda b_, d_, g_: jnp.where(strict, b_ * d_ * g_[:chunk], 0.0), beta_col, decay, gram)
    tinv = _tri_inv(a_mat, chunk)
    tt = _each(lambda t_, b_, g_, k_, v_: _mm(t_, jnp.concatenate([(b_ * g_) * k_, b_ * v_], axis=1)),
               tinv, beta_col, gam, k_h, v_h)
    qo = _each(lambda g_, d_, t_: _mm(g_[chunk:] * d_, t_), gram, decay, tt)
    pq = _each(lambda t_, k_, e_: _mm_tn(t_, k_ * e_), tt, k_h, tail)
    qq = _each(lambda g_, q_, x: g_ * q_ - x[:, :HEAD_DIM], gam, q_h, qo)
    o0 = [x[:, HEAD_DIM:] for x in qo]
    pb = [x[:HEAD_DIM] for x in pq]
    q_mat = [x[HEAD_DIM:] for x in pq]

    state = [s_scr[bi, h] for bi in range(bb) for h in range(N_HEADS)]
    o_tiles = {}
    for c in range(n_chunks):
        grp = slice(c * group, (c + 1) * group)
        os_ = _each(lambda o_, q_, s_: o_ + _mm_nt(q_, s_), o0[grp], qq[grp], state)
        ge = [end_all[rs.start:rs.start + 1, N_HEADS + h:N_HEADS + h + 1] for rs, h in tiles[grp]]
        state = _each(lambda s_, g_, p_, q_: s_ * g_ - _mm(s_, p_) + q_, state, ge, pb[grp], q_mat[grp])
        for bi in range(bb):
            o_tiles[bi, c] = jnp.concatenate(os_[bi * N_HEADS:(bi + 1) * N_HEADS], axis=1)
    for bi in range(bb):
        for h in range(N_HEADS):
            s_scr[bi, h] = state[bi * N_HEADS + h]
    o_rows = [o_tiles[bi, c] for bi in range(bb) for c in range(n_chunks)]
    o = o_rows[0] if len(o_rows) == 1 else jnp.concatenate(o_rows, axis=0)
    ms = _head_sum(o * o, ones) * (1.0 / HEAD_DIM)
    zz = z_ref[...].reshape(total, GROUP_WIDTH)
    o_ref[...] = (o * lax.rsqrt(ms + NORM_EPS) * nw_ref[...] * _silu(zz)).reshape(bb, rows, GROUP_WIDTH)

    @pl.when(c_idx == pl.num_programs(1) - 1)
    def _():
        s_out_ref[...] = s_scr[...]


def _gdn(qkv, z, ba, buf, s0, prm, bb, chunk, n_chunks, n_valid):
    bsz, seq, _ = qkv.shape
    rows = chunk * n_chunks
    assert n_valid == chunk or n_chunks == 1
    state = pl.BlockSpec((bb, N_HEADS, HEAD_DIM, HEAD_DIM), lambda b, c: (b, 0, 0, 0))
    blk = lambda w: pl.BlockSpec((bb, rows, w), lambda b, c: (b, c, 0))
    return pl.pallas_call(
        functools.partial(_gdn_kernel, bb=bb, chunk=chunk, n_chunks=n_chunks, n_valid=n_valid),
        grid=(bsz // bb, seq // rows),
        in_specs=[blk(GDN_QKV), blk(GROUP_WIDTH), blk(LANES),
                  pl.BlockSpec((bb, GDN_CONV - 1, GDN_QKV), lambda b, c: (b, 0, 0)), state,
                  _const_spec((GDN_CONV, GDN_QKV)), _const_spec((1, LANES)), _const_spec((1, LANES)),
                  _const_spec((1, GROUP_WIDTH))],
        out_specs=[blk(GROUP_WIDTH), state],
        out_shape=[jax.ShapeDtypeStruct((bsz, seq, GROUP_WIDTH), F32), jax.ShapeDtypeStruct(s0.shape, F32)],
        scratch_shapes=[pltpu.VMEM((bb, N_HEADS, HEAD_DIM, HEAD_DIM), F32),
                        pltpu.VMEM((bb, SUBLANES + rows, GDN_QKV), F32)],
        compiler_params=pltpu.CompilerParams(dimension_semantics=("arbitrary", "arbitrary")),
        name="gdn",
    )(qkv, z, ba, buf, s0, prm["conv_w"], prm["a_log"], prm["dt_bias"], prm["norm_w"])


def _gelu_tanh(x):
    return 0.5 * x * (1.0 + jnp.tanh(math.sqrt(2.0 / math.pi) * (x + 0.044715 * (x * x * x))))


def _s5_kernel(u_ref, x0re_ref, x0im_ref, are_ref, aim_ref, ldt_ref, bre_ref, bim_ref, cre_ref, cim_ref, d_ref,
               wglu_ref, bglu_ref, o_ref, hre_out_ref, him_out_ref, hre_scr, him_scr, sre_scr, sim_scr, *, steps, rows):
    c_idx = pl.program_id(1)

    @pl.when(c_idx == 0)
    def _():
        sre_scr[...] = x0re_ref[...]
        sim_scr[...] = x0im_ref[...]

    a_re, a_im = are_ref[...], aim_ref[...]
    dt = jnp.exp(ldt_ref[...])
    mag = jnp.exp(dt * a_re)
    ab_re = mag * jnp.cos(dt * a_im)
    ab_im = mag * jnp.sin(dt * a_im)
    den = a_re * a_re + a_im * a_im
    nr = ab_re - 1.0
    cf_re = (nr * a_re + ab_im * a_im) / den
    cf_im = (ab_im * a_re - nr * a_im) / den

    u = u_ref[...].reshape(steps * rows, GROUP_WIDTH)
    ub = u.astype(BF16)
    bu_re = _mm(ub, bre_ref[...])
    bu_im = _mm(ub, bim_ref[...])
    hre_scr[...] = cf_re * bu_re - cf_im * bu_im
    him_scr[...] = cf_re * bu_im + cf_im * bu_re

    abr = jnp.broadcast_to(ab_re, (rows, S5_LANES))
    abi = jnp.broadcast_to(ab_im, (rows, S5_LANES))

    def step(t, carry):
        h_re, h_im = carry
        base = pl.multiple_of(t * rows, rows)
        n_re = abr * h_re - abi * h_im + hre_scr[pl.ds(base, rows), :]
        n_im = abr * h_im + abi * h_re + him_scr[pl.ds(base, rows), :]
        hre_scr[pl.ds(base, rows), :] = n_re
        him_scr[pl.ds(base, rows), :] = n_im
        return n_re, n_im

    h_re, h_im = lax.fori_loop(0, steps, step, (sre_scr[...], sim_scr[...]))
    sre_scr[...] = h_re
    sim_scr[...] = h_im

    y = (_mm(hre_scr[...].astype(BF16), cre_ref[...]) - _mm(him_scr[...].astype(BF16), cim_ref[...])
         + d_ref[...] * u)
    zz = _gelu_tanh(y)
    out = zz * jax.nn.sigmoid(_mm(zz.astype(BF16), wglu_ref[...]) + bglu_ref[...])
    o_ref[...] = out.reshape(steps, rows, GROUP_WIDTH)

    @pl.when(c_idx == pl.num_programs(1) - 1)
    def _():
        hre_out_ref[...] = h_re
        him_out_ref[...] = h_im


def _s5(u_t, x0_re, x0_im, prm, steps, rows):
    seq, bsz, _ = u_t.shape
    st = pl.BlockSpec((rows, S5_LANES), lambda b, c: (b, 0))
    vec = _const_spec((1, S5_LANES))
    return pl.pallas_call(
        functools.partial(_s5_kernel, steps=steps, rows=rows),
        grid=(bsz // rows, seq // steps),
        in_specs=[pl.BlockSpec((steps, rows, GROUP_WIDTH), lambda b, c: (c, b, 0)), st, st, vec, vec, vec,
                  _const_spec((GROUP_WIDTH, S5_LANES)), _const_spec((GROUP_WIDTH, S5_LANES)),
                  _const_spec((S5_LANES, GROUP_WIDTH)), _const_spec((S5_LANES, GROUP_WIDTH)),
                  _const_spec((1, GROUP_WIDTH)), _const_spec((GROUP_WIDTH, GROUP_WIDTH)),
                  _const_spec((1, GROUP_WIDTH))],
        out_specs=[pl.BlockSpec((steps, rows, GROUP_WIDTH), lambda b, c: (c, b, 0)), st, st],
        out_shape=[jax.ShapeDtypeStruct((seq, bsz, GROUP_WIDTH), F32),
                   jax.ShapeDtypeStruct((bsz, S5_LANES), F32), jax.ShapeDtypeStruct((bsz, S5_LANES), F32)],
        scratch_shapes=[pltpu.VMEM((steps * rows, S5_LANES), F32), pltpu.VMEM((steps * rows, S5_LANES), F32),
                        pltpu.VMEM((rows, S5_LANES), F32), pltpu.VMEM((rows, S5_LANES), F32)],
        compiler_params=pltpu.CompilerParams(dimension_semantics=("arbitrary", "arbitrary"),
                                             vmem_limit_bytes=VMEM_LIMIT),
        name="s5",
    )(u_t, x0_re, x0_im, prm["a_re"], prm["a_im"], prm["log_dt"], prm["b_re"], prm["b_im"], prm["c_re"],
      prm["c_im"], prm["d"], prm["w_glu"], prm["b_glu"])


def _rope(x, cos, sin, width):
    fwd = pltpu.roll(x, width - ROPE_DIM // 2, 1)
    bwd = pltpu.roll(x, ROPE_DIM // 2, 1)
    lane = lax.broadcasted_iota(jnp.int32, x.shape, 1) % HEAD_DIM
    return x * cos + jnp.where(lane < ROPE_DIM // 2, fwd, bwd) * sin


def _swa_kernel(p_ref, cos_ref, sin_ref, kbuf_ref, vbuf_ref, sink_ref, o_ref, krot_ref, k_scr, v_scr,
                *, bb, qb, start, carry):
    j = pl.program_id(1)
    wb = WINDOW

    @pl.when(j == 0)
    def _():
        k_scr[:, 0:wb, :] = kbuf_ref[...]
        v_scr[:, 0:wb, :] = vbuf_ref[...]
        if qb < wb:
            k_scr[:, wb:2 * wb, :] = jnp.zeros((bb, wb, SWA_KV_WIDTH), F32)
            v_scr[:, wb:2 * wb, :] = jnp.zeros((bb, wb, SWA_KV_WIDTH), F32)

    p = p_ref[...].reshape(bb * qb, SWA_PROJ)
    cos, sin = cos_ref[...], sin_ref[...]
    if bb > 1:
        cos, sin = jnp.concatenate([cos] * bb, axis=0), jnp.concatenate([sin] * bb, axis=0)
    q = _rope(p[:, 0:GROUP_WIDTH], cos, sin, GROUP_WIDTH)
    k = _rope(p[:, GROUP_WIDTH:GROUP_WIDTH + SWA_KV_WIDTH], cos[:, 0:SWA_KV_WIDTH], sin[:, 0:SWA_KV_WIDTH],
              SWA_KV_WIDTH)
    v = p[:, GROUP_WIDTH + SWA_KV_WIDTH:SWA_PROJ]
    krot_ref[...] = k.reshape(bb, qb, SWA_KV_WIDTH)
    for bi in range(bb):
        k_scr[bi, wb:wb + qb, :] = k[bi * qb:(bi + 1) * qb]
        v_scr[bi, wb:wb + qb, :] = v[bi * qb:(bi + 1) * qb]

    nk = 2 * wb
    rq = lax.broadcasted_iota(jnp.int32, (2 * qb, nk), 0) % qb
    ck = lax.broadcasted_iota(jnp.int32, (2 * qb, nk), 1)
    rel = rq + wb - ck
    kpos = start + j * qb - wb + ck
    valid = (rel >= 0) & (rel < WINDOW) & (kpos >= 0)
    upper = lax.broadcasted_iota(jnp.int32, (2 * qb, 1), 0) < qb
    sinks = sink_ref[...]
    low_half = lax.broadcasted_iota(jnp.int32, (nk, SWA_KV_WIDTH), 1) < HEAD_DIM

    probs = [(bi, h) for bi in range(bb) for h in range(SWA_KV_HEADS)]
    q2 = [jnp.concatenate([q[bi * qb:(bi + 1) * qb, (2 * h) * HEAD_DIM:(2 * h + 1) * HEAD_DIM],
                           q[bi * qb:(bi + 1) * qb, (2 * h + 1) * HEAD_DIM:(2 * h + 2) * HEAD_DIM]], axis=0)
          for bi, h in probs]
    kh = [k_scr[bi, :, h * HEAD_DIM:(h + 1) * HEAD_DIM] for bi, h in probs]
    va = [jnp.where(low_half if h == 0 else ~low_half, v_scr[bi], 1.0) for bi, h in probs]
    sink = [jnp.where(upper, sinks[:, 2 * h:2 * h + 1], sinks[:, 2 * h + 1:2 * h + 2]) for _, h in probs]
    sc = _each(lambda q_, k_: jnp.where(valid, _mm_nt(q_, k_) * (HEAD_DIM ** -0.5), -jnp.inf), q2, kh)
    mx = _each(lambda s_, z_: jnp.maximum(jnp.max(s_, axis=-1, keepdims=True), z_), sc, sink)
    ex = _each(lambda s_, m_: jnp.exp(s_ - m_), sc, mx)
    ov = _each(_mm, ex, va)
    res = {}
    for (bi, h), o_, m_, z_ in zip(probs, ov, mx, sink):
        num = o_[:, 0:HEAD_DIM] if h == 0 else o_[:, HEAD_DIM:]
        den = (o_[:, HEAD_DIM:HEAD_DIM + 1] if h == 0 else o_[:, 0:1]) + jnp.exp(z_ - m_)
        res[bi, h] = num / den
    rows_out = [jnp.concatenate([res[bi, 0][:qb], res[bi, 0][qb:], res[bi, 1][:qb], res[bi, 1][qb:]], axis=1)
                for bi in range(bb)]
    o_ref[...] = (rows_out[0] if bb == 1 else jnp.concatenate(rows_out, axis=0)).reshape(bb, qb, GROUP_WIDTH)

    if carry:
        for bi in range(bb):
            k_scr[bi, 0:wb, :] = k[bi * qb:(bi + 1) * qb]
            v_scr[bi, 0:wb, :] = v[bi * qb:(bi + 1) * qb]


def _swa(p, cos, sin, kbuf, vbuf, sinks, bb, qb, start):
    bsz, seq, _ = p.shape
    nb = seq // qb
    assert nb == 1 or qb == WINDOW
    blk = lambda w: pl.BlockSpec((bb, qb, w), lambda b, j: (b, j, 0))
    tab = pl.BlockSpec((qb, GROUP_WIDTH), lambda b, j: (j, 0))
    buf = pl.BlockSpec((bb, WINDOW, SWA_KV_WIDTH), lambda b, j: (b, 0, 0))
    return pl.pallas_call(
        functools.partial(_swa_kernel, bb=bb, qb=qb, start=start, carry=nb > 1),
        grid=(bsz // bb, nb),
        in_specs=[blk(SWA_PROJ), tab, tab, buf, buf, _const_spec((1, LANES))],
        out_specs=[blk(GROUP_WIDTH), pl.BlockSpec((bb, qb, SWA_KV_WIDTH), lambda b, j: (b, 0, 0))],
        out_shape=[jax.ShapeDtypeStruct((bsz, seq, GROUP_WIDTH), F32),
                   jax.ShapeDtypeStruct((bsz, qb, SWA_KV_WIDTH), F32)],
        scratch_shapes=[pltpu.VMEM((bb, 2 * WINDOW, SWA_KV_WIDTH), F32),
                        pltpu.VMEM((bb, 2 * WINDOW, SWA_KV_WIDTH), F32)],
        compiler_params=pltpu.CompilerParams(dimension_semantics=("arbitrary", "arbitrary")),
        name="swa",
    )(p, cos, sin, kbuf, vbuf, sinks)


def _rope_tables(start, seq):
    half = ROPE_DIM // 2
    inv = ROPE_THETA ** (-jnp.arange(0, ROPE_DIM, 2, dtype=F32) / ROPE_DIM)
    ang = (start + jnp.arange(seq)).astype(F32)[:, None] * inv[None, :]
    cos, sin = jnp.cos(ang), jnp.sin(ang)
    rest = HEAD_DIM - ROPE_DIM
    cos_h = jnp.concatenate([cos, cos, jnp.ones((seq, rest), F32)], axis=1)
    sin_h = jnp.concatenate([-sin, sin, jnp.zeros((seq, rest), F32)], axis=1)
    return jnp.tile(cos_h, (1, N_HEADS)), jnp.tile(sin_h, (1, N_HEADS))


def _pad_rows(w, top, total):
    return jnp.pad(w, ((top, total - top - w.shape[0]), (0, 0)))


def _block_diag_in(b):
    eye = jnp.eye(S5_GROUPS, dtype=b.dtype)
    return jnp.einsum("gnc,gh->gchn", b, eye).reshape(S5_GROUPS * S5_CH, S5_GROUPS * S5_STATE)


def _block_diag_out(c):
    eye = jnp.eye(S5_GROUPS, dtype=c.dtype)
    return jnp.einsum("gcn,gh->gnhc", c, eye).reshape(S5_GROUPS * S5_STATE, S5_GROUPS * S5_CH)


def _layer_params(l, g_mix_pre, g_mix_post, g_mlp_pre, g_mlp_post, w_in, w_out, rw_mu, rw_w0, rw_w2, rw_a0, rw_a2,
                  rw_g2, rw_kk, rw_ka, rw_rk, rw_ln_w, rw_ln_b, s5_a_re, s5_a_im, s5_log_dt, s5_b_re, s5_b_im,
                  s5_c_re, s5_c_im, s5_d, s5_w_glu, s5_b_glu, gdn_conv_w, gdn_a_log, gdn_dt_bias, gdn_norm_w,
                  swa_sinks, w_up, w_down):
    row = lambda a: a[l].astype(F32).reshape(1, -1)
    wi = w_in[l].astype(F32)
    o_s5 = RW_PROJ
    o_gdn = o_s5 + GROUP_WIDTH
    o_ba = o_gdn + GDN_QKV
    o_z = o_ba + 2 * N_HEADS
    o_swa = o_z + GROUP_WIDTH
    w_all = jnp.concatenate([
        wi[:, :o_s5], wi[:, o_s5:o_gdn], wi[:, o_gdn:o_ba], wi[:, o_z:o_swa],
        jnp.pad(wi[:, o_ba:o_z], ((0, 0), (0, LANES - 2 * N_HEADS))), wi[:, o_swa:]], axis=1).astype(BF16)
    lane_pad = lambda a: jnp.pad(a[l].astype(F32), (N_HEADS, LANES - 2 * N_HEADS)).reshape(1, LANES)
    return {
        "g_mix_pre": row(g_mix_pre), "g_mix_post": row(g_mix_post), "g_mlp_pre": row(g_mlp_pre),
        "g_mlp_post": row(g_mlp_post), "w_all": w_all, "w_out": w_out[l].astype(BF16),
        "w_up": w_up[l].astype(BF16), "w_down": w_down[l].astype(BF16),
        "rw": {"mu": row(rw_mu), "w0": row(rw_w0), "a0": row(rw_a0), "kk": row(rw_kk), "ka": row(rw_ka),
               "rk": row(rw_rk), "ln_w": row(rw_ln_w), "ln_b": row(rw_ln_b),
               "w2": _pad_rows(rw_w2[l].astype(F32), 0, GROUP_WIDTH).astype(BF16),
               "a2": _pad_rows(rw_a2[l].astype(F32), RW_DECAY_RANK, GROUP_WIDTH).astype(BF16),
               "g2": _pad_rows(rw_g2[l].astype(F32), RW_DECAY_RANK + RW_ICLR_RANK, GROUP_WIDTH).astype(BF16)},
        "s5": {"a_re": row(s5_a_re), "a_im": row(s5_a_im),
               "log_dt": jnp.repeat(s5_log_dt[l].astype(F32), S5_STATE).reshape(1, S5_LANES),
               "b_re": _block_diag_in(s5_b_re[l].astype(F32)).astype(BF16),
               "b_im": _block_diag_in(s5_b_im[l].astype(F32)).astype(BF16),
               "c_re": _block_diag_out(s5_c_re[l].astype(F32)).astype(BF16),
               "c_im": _block_diag_out(s5_c_im[l].astype(F32)).astype(BF16),
               "d": row(s5_d), "w_glu": s5_w_glu[l].astype(BF16), "b_glu": row(s5_b_glu)},
        "gdn": {"conv_w": gdn_conv_w[l].astype(F32), "a_log": lane_pad(gdn_a_log), "dt_bias": lane_pad(gdn_dt_bias),
                "norm_w": jnp.tile(gdn_norm_w[l].astype(F32), N_HEADS).reshape(1, GROUP_WIDTH)},
        "sinks": jnp.pad(swa_sinks[l].astype(F32), (0, LANES - N_HEADS)).reshape(1, LANES),
    }


def _layer(x, start, st, prm):
    rw_s, rw_shift, s5_re, s5_im, gdn_s, gdn_conv, swa_k, swa_v = st
    bsz, seq, _ = x.shape
    tokens = bsz * seq
    tm = min(512, tokens)
    chunk = CHUNK if seq % CHUNK == 0 else -(-seq // SUBLANES) * SUBLANES
    seq_pad = -(-seq // chunk) * chunk
    n_valid = chunk - (seq_pad - seq)
    n_chunks = math.gcd(seq_pad // chunk, MIX_CHUNKS)
    bb = {name: math.gcd(bsz, pair[seq_pad == chunk]) for name, pair in MIXER_BATCH.items()}
    assert seq_pad == seq or seq_pad == chunk
    assert seq >= GDN_CONV - 1

    x2d = x.reshape(tokens, D_MODEL)
    p_rw, p_s5, p_qkv, p_z, p_ba, p_swa = (
        a.reshape(bsz, seq, -1) for a in _in_proj(x2d, prm["g_mix_pre"], prm["w_all"], tm))
    pad = lambda a: a if seq_pad == seq else jnp.pad(a, ((0, 0), (0, seq_pad - seq), (0, 0)))

    o_rw, rw_s_new = _rwkv(pad(p_rw), rw_shift, rw_s, prm["rw"], bb["rwkv"], chunk, n_chunks, n_valid)
    rw_shift_new = p_rw[:, seq - 1:, :]

    rows = 16
    steps = min(seq, 64)
    o_s5_t, s5_re_new, s5_im_new = _s5(jnp.swapaxes(p_s5, 0, 1), s5_re.reshape(bsz, S5_LANES),
                                       s5_im.reshape(bsz, S5_LANES), prm["s5"], steps, rows)
    o_s5 = jnp.swapaxes(o_s5_t, 0, 1)

    o_gdn, gdn_s_new = _gdn(pad(p_qkv), pad(p_z), pad(p_ba), gdn_conv, gdn_s, prm["gdn"], bb["gdn"],
                            chunk, n_chunks, n_valid)
    gdn_conv_new = p_qkv[:, seq - (GDN_CONV - 1):, :]

    qb = WINDOW if seq % WINDOW == 0 else seq_pad
    cos, sin = _rope_tables(start, seq_pad)
    wbuf = swa_k.shape[1]
    o_swa, k_rot = _swa(pad(p_swa), cos, sin, swa_k.reshape(bsz, wbuf, SWA_KV_WIDTH),
                        swa_v.reshape(bsz, wbuf, SWA_KV_WIDTH), prm["sinks"], bb["swa"], qb, start)
    k_new = k_rot[:, :qb - (seq_pad - seq), :] if seq_pad != seq else k_rot
    v_new = p_swa[:, seq - min(seq, wbuf):, GROUP_WIDTH + SWA_KV_WIDTH:]
    kv_shape = (bsz, -1, SWA_KV_HEADS, HEAD_DIM)
    if k_new.shape[1] >= wbuf:
        swa_k_new = k_new[:, -wbuf:].reshape(kv_shape)
        swa_v_new = v_new[:, -wbuf:].reshape(kv_shape)
    else:
        swa_k_new = jnp.concatenate([swa_k, k_new.reshape(kv_shape)], axis=1)[:, -wbuf:]
        swa_v_new = jnp.concatenate([swa_v, v_new.reshape(kv_shape)], axis=1)[:, -wbuf:]

    flat = lambda a: a[:, :seq, :].reshape(tokens, GROUP_WIDTH)
    y = _post(x2d, (flat(o_rw), flat(o_s5), flat(o_gdn), flat(o_swa)), prm["w_out"], prm["g_mix_post"],
              prm["g_mlp_pre"], prm["w_up"], prm["w_down"], prm["g_mlp_post"], tm)
    new_state = (rw_s_new, rw_shift_new, s5_re_new.reshape(s5_re.shape), s5_im_new.reshape(s5_im.shape),
                 gdn_s_new, gdn_conv_new, swa_k_new, swa_v_new)
    return y.reshape(bsz, seq, D_MODEL), new_state


def kernel(x_prompt, x_sample, state_rwkv, state_rwkv_shift, state_s5_re, state_s5_im, state_gdn, state_gdn_conv,
           cache_swa_k, cache_swa_v, g_mix_pre, g_mix_post, g_mlp_pre, g_mlp_post, w_in, w_out, rw_mu, rw_w0, rw_w2,
           rw_a0, rw_a2, rw_g2, rw_kk, rw_ka, rw_rk, rw_ln_w, rw_ln_b, s5_a_re, s5_a_im, s5_log_dt, s5_b_re, s5_b_im,
           s5_c_re, s5_c_im, s5_d, s5_w_glu, s5_b_glu, gdn_conv_w, gdn_a_log, gdn_dt_bias, gdn_norm_w, swa_sinks,
           w_up, w_down):
    weights = (g_mix_pre, g_mix_post, g_mlp_pre, g_mlp_post, w_in, w_out, rw_mu, rw_w0, rw_w2, rw_a0, rw_a2, rw_g2,
               rw_kk, rw_ka, rw_rk, rw_ln_w, rw_ln_b, s5_a_re, s5_a_im, s5_log_dt, s5_b_re, s5_b_im, s5_c_re, s5_c_im,
               s5_d, s5_w_glu, s5_b_glu, gdn_conv_w, gdn_a_log, gdn_dt_bias, gdn_norm_w, swa_sinks, w_up, w_down)
    depth = w_in.shape[0]
    bp = x_prompt.shape[0]
    past_len = 16384
    xp = x_prompt.astype(F32)
    xs = x_sample.astype(F32)
    zp = lambda *shape: jnp.zeros((bp,) + shape, F32)
    prompt_init = (zp(N_HEADS, HEAD_DIM, HEAD_DIM), zp(1, RW_PROJ), zp(S5_GROUPS, S5_STATE), zp(S5_GROUPS, S5_STATE),
                   zp(N_HEADS, HEAD_DIM, HEAD_DIM), zp(GDN_CONV - 1, GDN_QKV),
                   zp(WINDOW, SWA_KV_HEADS, HEAD_DIM), zp(WINDOW, SWA_KV_HEADS, HEAD_DIM))
    sample_states = (state_rwkv, state_rwkv_shift, state_s5_re, state_s5_im, state_gdn, state_gdn_conv,
                     cache_swa_k, cache_swa_v)
    new_p, new_s = [], []
    for l in range(depth):
        prm = _layer_params(l, *weights)
        xp, st_p = _layer(xp, 0, prompt_init, prm)
        xs, st_s = _layer(xs, past_len, tuple(a[l].astype(F32) for a in sample_states), prm)
        new_p.append(st_p)
        new_s.append(st_s)
    outs = [xp.astype(x_prompt.dtype), xs.astype(x_sample.dtype)]
    for i in range(len(sample_states)):
        outs.append(jnp.stack([st[i] for st in new_p], axis=0))
        outs.append(jnp.stack([st[i] for st in new_s], axis=0))
    return tuple(outs)
```

```python
import functools
import math

import jax
import jax.numpy as jnp
from jax import lax
from jax.experimental import pallas as pl
from jax.experimental.pallas import tpu as pltpu

F32 = jnp.float32
BF16 = jnp.bfloat16

D_MODEL = 1024
HEAD_DIM = 64
N_HEADS = 4
GROUP_WIDTH = N_HEADS * HEAD_DIM
NORM_EPS = 1e-6
RW_LN_EPS = 64e-5
RW_PROJ = 1024
RW_LORA_OFF = 3 * GROUP_WIDTH
RW_DECAY_RANK, RW_ICLR_RANK, RW_GATE_RANK = 64, 64, 128
S5_GROUPS, S5_CH, S5_STATE = 16, 16, 64
S5_LANES = S5_GROUPS * S5_STATE
GDN_CONV = 4
GDN_QKV = 3 * GROUP_WIDTH
SWA_KV_HEADS = 2
SWA_KV_WIDTH = SWA_KV_HEADS * HEAD_DIM
SWA_PROJ = GROUP_WIDTH + 2 * SWA_KV_WIDTH
WINDOW = 128
ROPE_DIM = 16
ROPE_THETA = 500000.0
D_FF = 4096
LANES = 128
SUBLANES = 8
CHUNK = 64
MIX_CHUNKS = 4
MIXER_BATCH = {"rwkv": (2, 16), "gdn": (2, 16), "swa": (4, 32)}
TRI_BLOCK = 16
VMEM_LIMIT = 56 * 1024 * 1024

PROJ_WIDTHS = (RW_PROJ, GROUP_WIDTH, GDN_QKV, GROUP_WIDTH, LANES, SWA_PROJ)


def _dot(a, b, dims):
    return lax.dot_general(a.astype(BF16), b.astype(BF16), (dims, ((), ())), preferred_element_type=F32)


def _mm(a, b):
    return _dot(a, b, ((1,), (0,)))


def _mm_nt(a, b):
    return _dot(a, b, ((1,), (1,)))


def _mm_tn(a, b):
    return _dot(a, b, ((0,), (0,)))


def _split2(x):
    hi = x.astype(BF16)
    lo = (x - hi.astype(F32)).astype(BF16)
    return hi, lo


def _mm_exact_rhs(x, m):
    hi, lo = _split2(x)
    return _mm(hi, m) + _mm(lo, m)


def _mm_exact_lhs(m, x):
    hi, lo = _split2(x)
    return _mm(m, hi) + _mm(m, lo)


def _head_ones(width):
    r = lax.broadcasted_iota(jnp.int32, (width, width), 0)
    c = lax.broadcasted_iota(jnp.int32, (width, width), 1)
    return (r // HEAD_DIM == c // HEAD_DIM).astype(BF16)


def _head_sum(x, ones):
    return _mm_exact_rhs(x, ones)


def _tri_masks(n):
    r = lax.broadcasted_iota(jnp.int32, (n, n), 0)
    c = lax.broadcasted_iota(jnp.int32, (n, n), 1)
    return r, c, r > c, r >= c


def _softplus(x):
    return jnp.maximum(x, 0.0) + jnp.log(1.0 + jnp.exp(-jnp.abs(x)))


def _silu(x):
    return x * jax.nn.sigmoid(x)


def _rms(x, g):
    return x * lax.rsqrt(jnp.mean(x * x, axis=-1, keepdims=True) + NORM_EPS) * g


def _each(fn, *lists):
    return [fn(*xs) for xs in zip(*lists)]


def _tri_inv(lows, n):
    r, c, _, _ = _tri_masks(n)
    eye = (r == c).astype(F32)
    bs = min(TRI_BLOCK, n)
    if n > bs:
        same = (r // bs) == (c // bs)
        diags = [jnp.where(same, low, 0.0) for low in lows]
        rests = _each(lambda low, d: low - d, lows, diags)
    else:
        diags, rests = lows, None
    invs = [eye - d for d in diags]
    pws, p = diags, 1
    while 2 * p < bs:
        pws = _each(lambda x: _mm(x, x), pws)
        invs = _each(lambda i, x: i + _mm(i, x), invs, pws)
        p *= 2
    if rests is None:
        return invs
    nils = _each(_mm, invs, rests)
    outs = [eye - x for x in nils]
    pws, p = nils, 1
    while 2 * p < n // bs:
        pws = _each(lambda x: _mm(x, x), pws)
        outs = _each(lambda o, x: o + _mm(o, x), outs, pws)
        p *= 2
    return _each(_mm, outs, invs)


def _const_spec(shape):
    zeros = (0,) * len(shape)
    return pl.BlockSpec(shape, lambda *_: zeros, pipeline_mode=pl.Buffered(1))


def _in_proj_kernel(x_ref, g_ref, w_ref, *out_refs):
    h = _rms(x_ref[...], g_ref[...]).astype(BF16)
    off = 0
    for o_ref, width in zip(out_refs, PROJ_WIDTHS):
        o_ref[...] = _mm(h, w_ref[:, off:off + width])
        off += width


def _in_proj(x2d, g, w_all, tm):
    t = x2d.shape[0]
    wtot = sum(PROJ_WIDTHS)
    return pl.pallas_call(
        _in_proj_kernel,
        grid=(t // tm,),
        in_specs=[pl.BlockSpec((tm, D_MODEL), lambda i: (i, 0)),
                  _const_spec((1, D_MODEL)),
                  _const_spec((D_MODEL, wtot))],
        out_specs=[pl.BlockSpec((tm, w), lambda i: (i, 0)) for w in PROJ_WIDTHS],
        out_shape=[jax.ShapeDtypeStruct((t, w), F32) for w in PROJ_WIDTHS],
        compiler_params=pltpu.CompilerParams(dimension_semantics=("arbitrary",), vmem_limit_bytes=VMEM_LIMIT),
        name="in_proj",
    )(x2d, g, w_all)


FF_CHUNK = 1024


def _post_kernel(x_ref, o0_ref, o1_ref, o2_ref, o3_ref, wout_ref, gpost_ref, gpre_ref, wup_ref, wdn_ref,
                 gmlp_ref, y_ref):
    mix = None
    for i, o_ref in enumerate((o0_ref, o1_ref, o2_ref, o3_ref)):
        part = _mm(o_ref[...].astype(BF16), wout_ref[i * GROUP_WIDTH:(i + 1) * GROUP_WIDTH, :])
        mix = part if mix is None else mix + part
    x1 = x_ref[...] + _rms(mix, gpost_ref[...])
    h = _rms(x1, gpre_ref[...]).astype(BF16)
    acc = None
    for c in range(D_FF // FF_CHUNK):
        u = _mm(h, wup_ref[:, c * FF_CHUNK:(c + 1) * FF_CHUNK])
        u = jnp.square(jnp.maximum(u, 0.0)).astype(BF16)
        part = _mm(u, wdn_ref[c * FF_CHUNK:(c + 1) * FF_CHUNK, :])
        acc = part if acc is None else acc + part
    y_ref[...] = x1 + _rms(acc, gmlp_ref[...])


def _post(x2d, outs, w_out, g_post, g_pre, w_up, w_dn, g_mlp, tm):
    t = x2d.shape[0]
    row = lambda w: pl.BlockSpec((tm, w), lambda i: (i, 0))
    return pl.pallas_call(
        _post_kernel,
        grid=(t // tm,),
        in_specs=[row(D_MODEL)] + [row(GROUP_WIDTH)] * 4 + [
            _const_spec((D_MODEL, D_MODEL)), _const_spec((1, D_MODEL)), _const_spec((1, D_MODEL)),
            _const_spec((D_MODEL, D_FF)), _const_spec((D_FF, D_MODEL)), _const_spec((1, D_MODEL))],
        out_specs=row(D_MODEL),
        out_shape=jax.ShapeDtypeStruct((t, D_MODEL), F32),
        compiler_params=pltpu.CompilerParams(dimension_semantics=("arbitrary",), vmem_limit_bytes=VMEM_LIMIT),
        name="post_mlp",
    )(x2d, *outs, w_out, g_post, g_pre, w_up, w_dn, g_mlp)


def _chunk_sums(x, chunk, rows):
    _, _, _, incl = _tri_masks(chunk)
    tri = incl.astype(BF16)
    cums = [_mm_exact_lhs(tri, x[c * chunk:(c + 1) * chunk]) for c in range(rows // chunk)]
    if len(cums) == 1:
        return cums[0], cums[0][chunk - 1:chunk, :]
    tots = [jnp.broadcast_to(cu[chunk - 1:chunk, :], cu.shape) for cu in cums]
    return jnp.concatenate(cums, axis=0), jnp.concatenate(tots, axis=0)


def _rwkv_kernel(p_ref, shift_ref, s0_ref, mu_ref, w0_ref, a0_ref, kkp_ref, kap_ref, rk_ref, lnw_ref, lnb_ref,
                 w2_ref, a2_ref, g2_ref, o_ref, s_out_ref, s_scr, carry_scr, *, bb, chunk, n_chunks, n_valid):
    c_idx = pl.program_id(1)
    rows = chunk * n_chunks
    total = bb * rows

    @pl.when(c_idx == 0)
    def _():
        s_scr[...] = s0_ref[...]
        carry_scr[...] = shift_ref[...]

    first = lax.broadcasted_iota(jnp.int32, (rows, 1), 0) == 0
    last = rows - chunk + n_valid
    prevs = []
    for bi in range(bb):
        p_b = p_ref[bi]
        prevs.append(jnp.where(first, carry_scr[bi], pltpu.roll(p_b, 1, 0)))
        carry_scr[bi] = p_b[last - 1:last, :]
    p = p_ref[...].reshape(total, RW_PROJ)
    prev = prevs[0] if bb == 1 else jnp.concatenate(prevs, axis=0)
    row = lax.broadcasted_iota(jnp.int32, (total, 1), 0)
    xs = p + (prev - p) * mu_ref[...]
    r = xs[:, 0:GROUP_WIDTH]
    k = xs[:, GROUP_WIDTH:2 * GROUP_WIDTH]
    v = xs[:, 2 * GROUP_WIDTH:3 * GROUP_WIDTH]
    lora = xs[:, RW_LORA_OFF:RW_PROJ]

    z_w = w0_ref[...] + _mm(jnp.tanh(lora), w2_ref[...])
    logw = -jnp.exp(-_softplus(-z_w) - 0.5)
    a = jax.nn.sigmoid(a0_ref[...] + _mm(lora, a2_ref[...]))
    gate = _mm(jax.nn.sigmoid(lora), g2_ref[...])

    ones = _head_ones(GROUP_WIDTH)
    kk = k * kkp_ref[...]
    kk = kk * lax.rsqrt(jnp.maximum(_head_sum(kk * kk, ones), 1e-12))
    k2 = k * (1.0 + (a - 1.0) * kap_ref[...])
    b = kk * a
    if n_valid < chunk:
        valid = (row % chunk) < n_valid
        logw = jnp.where(valid, logw, 0.0)
        kk = jnp.where(valid, kk, 0.0)
        b = jnp.where(valid, b, 0.0)
        k2v = jnp.where(valid, k2, 0.0)
        vv = jnp.where(valid, v, 0.0)
    else:
        k2v, vv = k2, v

    log_cum, log_end = _chunk_sums(logw, chunk, total)
    w_inv = jnp.exp(-log_cum)
    w_tail = jnp.exp(log_end - log_cum)
    a_bar = kk * jnp.exp(log_cum - logw)
    b_bar = b * w_inv
    k_bar = k2v * w_inv
    r_bar = r * jnp.exp(log_cum)
    b_end = b * w_tail
    k_end = k2v * w_tail
    w_end = jnp.exp(log_end)

    _, _, strict, incl = _tri_masks(chunk)
    group = bb * N_HEADS
    tiles = [(slice(bi * rows + c * chunk, bi * rows + (c + 1) * chunk), slice(h * HEAD_DIM, (h + 1) * HEAD_DIM))
             for c in range(n_chunks) for bi in range(bb) for h in range(N_HEADS)]
    cut = lambda x: [x[rs, sl] for rs, sl in tiles]
    ab, rb, bs, kb, be, ke, vh = map(cut, (a_bar, r_bar, b_bar, k_bar, b_end, k_end, vv))
    gram = _each(lambda a_, r_, b_, k_: _mm_nt(jnp.concatenate([a_, r_], axis=0), jnp.concatenate([b_, k_], axis=0)),
                 ab, rb, bs, kb)
    l_ab = [jnp.where(strict, g[:chunk, :chunk], 0.0) for g in gram]
    l_ak = [jnp.where(strict, g[:chunk, chunk:], 0.0) for g in gram]
    m_rb = [jnp.where(incl, g[chunk:, :chunk], 0.0) for g in gram]
    m_rk = [jnp.where(incl, g[chunk:, chunk:], 0.0) for g in gram]
    tinv = _tri_inv(l_ab, chunk)
    lv = _each(_mm, l_ak, vh)
    tt = _each(lambda t_, a_, l_: _mm(t_, jnp.concatenate([a_, l_], axis=1)), tinv, ab, lv)
    ta_c0 = [jnp.concatenate([t_[:, :HEAD_DIM], -t_[:, HEAD_DIM:]], axis=1) for t_ in tt]
    pq = _each(_mm_tn, ta_c0, be)
    vk = _each(_mm_tn, vh, ke)
    ry = _each(_mm, m_rb, ta_c0)
    rkv = _each(_mm, m_rk, vh)
    pb = [x[:HEAD_DIM] for x in pq]
    q_mat = _each(lambda x, y_: x[HEAD_DIM:] + y_, pq, vk)
    rr = _each(lambda r_, x: r_ - x[:, :HEAD_DIM], rb, ry)
    y0 = _each(lambda x, y_: x[:, HEAD_DIM:] + y_, ry, rkv)

    state = [s_scr[bi, h] for bi in range(bb) for h in range(N_HEADS)]
    y_tiles = {}
    for c in range(n_chunks):
        grp = slice(c * group, (c + 1) * group)
        ys = _each(lambda y_, r_, s_: y_ + _mm_nt(r_, s_), y0[grp], rr[grp], state)
        we = [w_end[rs.start:rs.start + 1, sl] for rs, sl in tiles[grp]]
        state = _each(lambda s_, w_, p_, q_: s_ * w_ - _mm(s_, p_) + q_, state, we, pb[grp], q_mat[grp])
        for bi in range(bb):
            y_tiles[bi, c] = jnp.concatenate(ys[bi * N_HEADS:(bi + 1) * N_HEADS], axis=1)
    for bi in range(bb):
        for h in range(N_HEADS):
            s_scr[bi, h] = state[bi * N_HEADS + h]
    y_rows = [y_tiles[bi, c] for bi in range(bb) for c in range(n_chunks)]
    y = y_rows[0] if len(y_rows) == 1 else jnp.concatenate(y_rows, axis=0)

    inv_d = 1.0 / HEAD_DIM
    mean = _head_sum(y, ones) * inv_d
    d = y - mean
    var = _head_sum(d * d, ones) * inv_d
    yn = d * lax.rsqrt(var + RW_LN_EPS) * lnw_ref[...] + lnb_ref[...]
    bonus = _head_sum(r * k2 * rk_ref[...], ones) * v
    o_ref[...] = ((yn + bonus) * gate).reshape(bb, rows, GROUP_WIDTH)

    @pl.when(c_idx == pl.num_programs(1) - 1)
    def _():
        s_out_ref[...] = s_scr[...]


def _state_specs(bb, layer):
    shape = (bb, N_HEADS, HEAD_DIM, HEAD_DIM)
    return (pl.BlockSpec((None,) + shape, lambda b, c: (layer, b, 0, 0, 0)),
            pl.BlockSpec(shape, lambda b, c: (b, 0, 0, 0)))


def _rwkv(p, shift, s0, layer, prm, bb, chunk, n_chunks, n_valid):
    bsz, seq, _ = p.shape
    rows = chunk * n_chunks
    assert n_valid == chunk or n_chunks == 1
    vec = _const_spec((1, GROUP_WIDTH))
    mat = _const_spec((GROUP_WIDTH, GROUP_WIDTH))
    state_in, state = _state_specs(bb, layer)
    return pl.pallas_call(
        functools.partial(_rwkv_kernel, bb=bb, chunk=chunk, n_chunks=n_chunks, n_valid=n_valid),
        grid=(bsz // bb, seq // rows),
        in_specs=[pl.BlockSpec((bb, rows, RW_PROJ), lambda b, c: (b, c, 0)),
                  pl.BlockSpec((bb, 1, RW_PROJ), lambda b, c: (b, 0, 0)),
                  state_in, _const_spec((1, RW_PROJ))] + [vec] * 7 + [mat] * 3,
        out_specs=[pl.BlockSpec((bb, rows, GROUP_WIDTH), lambda b, c: (b, c, 0)), state],
        out_shape=[jax.ShapeDtypeStruct((bsz, seq, GROUP_WIDTH), F32),
                   jax.ShapeDtypeStruct(s0.shape[1:], F32)],
        scratch_shapes=[pltpu.VMEM((bb, N_HEADS, HEAD_DIM, HEAD_DIM), F32), pltpu.VMEM((bb, 1, RW_PROJ), F32)],
        compiler_params=pltpu.CompilerParams(dimension_semantics=("arbitrary", "arbitrary")),
        name="rwkv7",
    )(p, shift, s0, prm["mu"], prm["w0"], prm["a0"], prm["kk"], prm["ka"], prm["rk"], prm["ln_w"], prm["ln_b"],
      prm["w2"], prm["a2"], prm["g2"])


def _gdn_kernel(qkv_ref, z_ref, ba_ref, buf_ref, s0_ref, convw_ref, alog_ref, dtb_ref, nw_ref,
                o_ref, s_out_ref, s_scr, x_scr, *, bb, chunk, n_chunks, n_valid):
    c_idx = pl.program_id(1)
    rows = chunk * n_chunks
    total = bb * rows
    pad = SUBLANES

    @pl.when(c_idx == 0)
    def _():
        s_scr[...] = s0_ref[...]
        x_scr[:, 0:pad, :] = jnp.zeros((bb, pad, GDN_QKV), F32)
        x_scr[:, pad - (GDN_CONV - 1):pad, :] = buf_ref[...]

    last = rows - chunk + n_valid
    convs = []
    for bi in range(bb):
        x_scr[bi, pad:pad + rows, :] = qkv_ref[bi]
        conv = None
        for j in range(GDN_CONV):
            term = x_scr[bi, pad - j:pad - j + rows, :] * convw_ref[GDN_CONV - 1 - j:GDN_CONV - j, :]
            conv = term if conv is None else conv + term
        x_scr[bi, 0:pad, :] = x_scr[bi, last:last + pad, :]
        convs.append(conv)
    act = _silu(convs[0] if bb == 1 else jnp.concatenate(convs, axis=0))
    q = act[:, 0:GROUP_WIDTH]
    k = act[:, GROUP_WIDTH:2 * GROUP_WIDTH]
    v = act[:, 2 * GROUP_WIDTH:3 * GROUP_WIDTH]
    ones = _head_ones(GROUP_WIDTH)
    q = q * lax.rsqrt(jnp.maximum(_head_sum(q * q, ones), 1e-12)) * (HEAD_DIM ** -0.5)
    k = k * lax.rsqrt(jnp.maximum(_head_sum(k * k, ones), 1e-12))

    ba = ba_ref[...].reshape(total, LANES)
    row = lax.broadcasted_iota(jnp.int32, (total, LANES), 0)
    beta = jax.nn.sigmoid(ba)
    glog = -jnp.exp(alog_ref[...]) * _softplus(ba + dtb_ref[...])
    if n_valid < chunk:
        valid = (row % chunk) < n_valid
        beta = jnp.where(valid, beta, 0.0)
        glog = jnp.where(valid, glog, 0.0)
    gcum, gend = _chunk_sums(glog, chunk, total)
    gcum_t = gcum.T
    er = lax.broadcasted_iota(jnp.int32, (LANES, GROUP_WIDTH), 0)
    ec = lax.broadcasted_iota(jnp.int32, (LANES, GROUP_WIDTH), 1) // HEAD_DIM
    beta_w = _mm_exact_rhs(beta, (er == ec).astype(BF16))
    spread = (er == ec + N_HEADS).astype(BF16)
    gcum_w = _mm_exact_rhs(gcum, spread)
    gend_w = _mm_exact_rhs(gend, spread)
    gam_w = jnp.exp(gcum_w)
    bgk = beta_w * gam_w * k
    bv = beta_w * v
    gq = gam_w * q
    kt = k * jnp.exp(gend_w - gcum_w)
    end_w = jnp.exp(gend_w)

    _, _, strict, incl = _tri_masks(chunk)
    group = bb * N_HEADS
    tiles = [(slice(bi * rows + c * chunk, bi * rows + (c + 1) * chunk), h)
             for c in range(n_chunks) for bi in range(bb) for h in range(N_HEADS)]
    head = lambda x: [x[rs, h * HEAD_DIM:(h + 1) * HEAD_DIM] for rs, h in tiles]
    sq = lambda x: [x[rs, h * HEAD_DIM:h * HEAD_DIM + chunk] for rs, h in tiles]
    k_h, q_h, bgk_h, bv_h, gq_h, kt_h = map(head, (k, q, bgk, bv, gq, kt))
    g_row = [gcum_t[N_HEADS + h:N_HEADS + h + 1, rs] for rs, h in tiles]
    decay = _each(lambda gc, gr: jnp.where(incl, jnp.exp(jnp.where(incl, gc - gr, 0.0)), 0.0), sq(gcum_w), g_row)
    gram = _each(lambda k_, q_: _mm_nt(jnp.concatenate([k_, q_], axis=0), k_), k_h, q_h)
    a_mat = _each(lambda b_, d_, g_: jnp.where(strict, b_ * d_ * g_[:chunk], 0.0), sq(beta_w), decay, gram)
    tinv = _tri_inv(a_mat, chunk)
    tt = _each(lambda t_, x, y_: _mm(t_, jnp.concatenate([x, y_], axis=1)), tinv, bgk_h, bv_h)
    qo = _each(lambda g_, d_, t_: _mm(g_[chunk:] * d_, t_), gram, decay, tt)
    pq = _each(_mm_tn, tt, kt_h)
    qq = _each(lambda g_, x: g_ - x[:, :HEAD_DIM], gq_h, qo)
    o0 = [x[:, HEAD_DIM:] for x in qo]
    pb = [x[:HEAD_DIM] for x in pq]
    q_mat = [x[HEAD_DIM:] for x in pq]

    state = [s_scr[bi, h] for bi in range(bb) for h in range(N_HEADS)]
    o_tiles = {}
    for c in range(n_chunks):
        grp = slice(c * group, (c + 1) * group)
        os_ = _each(lambda o_, q_, s_: o_ + _mm_nt(q_, s_), o0[grp], qq[grp], state)
        ge = [end_w[rs.start:rs.start + 1, h * HEAD_DIM:(h + 1) * HEAD_DIM] for rs, h in tiles[grp]]
        state = _each(lambda s_, g_, p_, q_: s_ * g_ - _mm(s_, p_) + q_, state, ge, pb[grp], q_mat[grp])
        for bi in range(bb):
            o_tiles[bi, c] = jnp.concatenate(os_[bi * N_HEADS:(bi + 1) * N_HEADS], axis=1)
    for bi in range(bb):
        for h in range(N_HEADS):
            s_scr[bi, h] = state[bi * N_HEADS + h]
    o_rows = [o_tiles[bi, c] for bi in range(bb) for c in range(n_chunks)]
    o = o_rows[0] if len(o_rows) == 1 else jnp.concatenate(o_rows, axis=0)
    ms = _head_sum(o * o, ones) * (1.0 / HEAD_DIM)
    zz = z_ref[...].reshape(total, GROUP_WIDTH)
    o_ref[...] = (o * lax.rsqrt(ms + NORM_EPS) * nw_ref[...] * _silu(zz)).reshape(bb, rows, GROUP_WIDTH)

    @pl.when(c_idx == pl.num_programs(1) - 1)
    def _():
        s_out_ref[...] = s_scr[...]


def _gdn(qkv, z, ba, buf, s0, layer, prm, bb, chunk, n_chunks, n_valid):
    bsz, seq, _ = qkv.shape
    rows = chunk * n_chunks
    assert n_valid == chunk or n_chunks == 1
    state_in, state = _state_specs(bb, layer)
    blk = lambda w: pl.BlockSpec((bb, rows, w), lambda b, c: (b, c, 0))
    return pl.pallas_call(
        functools.partial(_gdn_kernel, bb=bb, chunk=chunk, n_chunks=n_chunks, n_valid=n_valid),
        grid=(bsz // bb, seq // rows),
        in_specs=[blk(GDN_QKV), blk(GROUP_WIDTH), blk(LANES),
                  pl.BlockSpec((bb, GDN_CONV - 1, GDN_QKV), lambda b, c: (b, 0, 0)), state_in,
                  _const_spec((GDN_CONV, GDN_QKV)), _const_spec((1, LANES)), _const_spec((1, LANES)),
                  _const_spec((1, GROUP_WIDTH))],
        out_specs=[blk(GROUP_WIDTH), state],
        out_shape=[jax.ShapeDtypeStruct((bsz, seq, GROUP_WIDTH), F32), jax.ShapeDtypeStruct(s0.shape[1:], F32)],
        scratch_shapes=[pltpu.VMEM((bb, N_HEADS, HEAD_DIM, HEAD_DIM), F32),
                        pltpu.VMEM((bb, SUBLANES + rows, GDN_QKV), F32)],
        compiler_params=pltpu.CompilerParams(dimension_semantics=("arbitrary", "arbitrary")),
        name="gdn",
    )(qkv, z, ba, buf, s0, prm["conv_w"], prm["a_log"], prm["dt_bias"], prm["norm_w"])


def _gelu_tanh(x):
    return 0.5 * x * (1.0 + jnp.tanh(math.sqrt(2.0 / math.pi) * (x + 0.044715 * (x * x * x))))


def _s5_kernel(u_ref, x0re_ref, x0im_ref, are_ref, aim_ref, ldt_ref, bre_ref, bim_ref, cre_ref, cim_ref, d_ref,
               wglu_ref, bglu_ref, o_ref, hre_out_ref, him_out_ref, hre_scr, him_scr, sre_scr, sim_scr, *, steps, rows):
    c_idx = pl.program_id(1)

    @pl.when(c_idx == 0)
    def _():
        sre_scr[...] = x0re_ref[...]
        sim_scr[...] = x0im_ref[...]

    a_re, a_im = are_ref[...], aim_ref[...]
    dt = jnp.exp(ldt_ref[...])
    mag = jnp.exp(dt * a_re)
    ab_re = mag * jnp.cos(dt * a_im)
    ab_im = mag * jnp.sin(dt * a_im)
    den = a_re * a_re + a_im * a_im
    nr = ab_re - 1.0
    cf_re = (nr * a_re + ab_im * a_im) / den
    cf_im = (ab_im * a_re - nr * a_im) / den

    u = u_ref[...].reshape(steps * rows, GROUP_WIDTH)
    ub = u.astype(BF16)
    bu_re = _mm(ub, bre_ref[...])
    bu_im = _mm(ub, bim_ref[...])
    hre_scr[...] = cf_re * bu_re - cf_im * bu_im
    him_scr[...] = cf_re * bu_im + cf_im * bu_re

    abr = jnp.broadcast_to(ab_re, (rows, S5_LANES))
    abi = jnp.broadcast_to(ab_im, (rows, S5_LANES))

    def step(t, carry):
        h_re, h_im = carry
        base = pl.multiple_of(t * rows, rows)
        n_re = abr * h_re - abi * h_im + hre_scr[pl.ds(base, rows), :]
        n_im = abr * h_im + abi * h_re + him_scr[pl.ds(base, rows), :]
        hre_scr[pl.ds(base, rows), :] = n_re
        him_scr[pl.ds(base, rows), :] = n_im
        return n_re, n_im

    h_re, h_im = lax.fori_loop(0, steps, step, (sre_scr[...], sim_scr[...]))
    sre_scr[...] = h_re
    sim_scr[...] = h_im

    y = (_mm(hre_scr[...].astype(BF16), cre_ref[...]) - _mm(him_scr[...].astype(BF16), cim_ref[...])
         + d_ref[...] * u)
    zz = _gelu_tanh(y)
    out = zz * jax.nn.sigmoid(_mm(zz.astype(BF16), wglu_ref[...]) + bglu_ref[...])
    o_ref[...] = out.reshape(steps, rows, GROUP_WIDTH)

    @pl.when(c_idx == pl.num_programs(1) - 1)
    def _():
        hre_out_ref[...] = h_re
        him_out_ref[...] = h_im


def _s5(u_t, x0_re, x0_im, prm, steps, rows):
    seq, bsz, _ = u_t.shape
    st = pl.BlockSpec((rows, S5_LANES), lambda b, c: (b, 0))
    vec = _const_spec((1, S5_LANES))
    return pl.pallas_call(
        functools.partial(_s5_kernel, steps=steps, rows=rows),
        grid=(bsz // rows, seq // steps),
        in_specs=[pl.BlockSpec((steps, rows, GROUP_WIDTH), lambda b, c: (c, b, 0)), st, st, vec, vec, vec,
                  _const_spec((GROUP_WIDTH, S5_LANES)), _const_spec((GROUP_WIDTH, S5_LANES)),
                  _const_spec((S5_LANES, GROUP_WIDTH)), _const_spec((S5_LANES, GROUP_WIDTH)),
                  _const_spec((1, GROUP_WIDTH)), _const_spec((GROUP_WIDTH, GROUP_WIDTH)),
                  _const_spec((1, GROUP_WIDTH))],
        out_specs=[pl.BlockSpec((steps, rows, GROUP_WIDTH), lambda b, c: (c, b, 0)), st, st],
        out_shape=[jax.ShapeDtypeStruct((seq, bsz, GROUP_WIDTH), F32),
                   jax.ShapeDtypeStruct((bsz, S5_LANES), F32), jax.ShapeDtypeStruct((bsz, S5_LANES), F32)],
        scratch_shapes=[pltpu.VMEM((steps * rows, S5_LANES), F32), pltpu.VMEM((steps * rows, S5_LANES), F32),
                        pltpu.VMEM((rows, S5_LANES), F32), pltpu.VMEM((rows, S5_LANES), F32)],
        compiler_params=pltpu.CompilerParams(dimension_semantics=("arbitrary", "arbitrary"),
                                             vmem_limit_bytes=VMEM_LIMIT),
        name="s5",
    )(u_t, x0_re, x0_im, prm["a_re"], prm["a_im"], prm["log_dt"], prm["b_re"], prm["b_im"], prm["c_re"],
      prm["c_im"], prm["d"], prm["w_glu"], prm["b_glu"])


def _rope(x, cos, sin, width):
    fwd = pltpu.roll(x, width - ROPE_DIM // 2, 1)
    bwd = pltpu.roll(x, ROPE_DIM // 2, 1)
    lane = lax.broadcasted_iota(jnp.int32, x.shape, 1) % HEAD_DIM
    return x * cos + jnp.where(lane < ROPE_DIM // 2, fwd, bwd) * sin


def _swa_kernel(p_ref, cos_ref, sin_ref, kbuf_ref, vbuf_ref, sink_ref, o_ref, krot_ref, k_scr, v_scr,
                *, bb, qb, start, carry):
    j = pl.program_id(1)
    wb = WINDOW

    @pl.when(j == 0)
    def _():
        k_scr[:, 0:wb, :] = kbuf_ref[...]
        v_scr[:, 0:wb, :] = vbuf_ref[...]
        if qb < wb:
            k_scr[:, wb:2 * wb, :] = jnp.zeros((bb, wb, SWA_KV_WIDTH), F32)
            v_scr[:, wb:2 * wb, :] = jnp.zeros((bb, wb, SWA_KV_WIDTH), F32)

    p = p_ref[...].reshape(bb * qb, SWA_PROJ)
    cos, sin = cos_ref[...], sin_ref[...]
    if bb > 1:
        cos, sin = jnp.concatenate([cos] * bb, axis=0), jnp.concatenate([sin] * bb, axis=0)
    q = _rope(p[:, 0:GROUP_WIDTH], cos, sin, GROUP_WIDTH)
    k = _rope(p[:, GROUP_WIDTH:GROUP_WIDTH + SWA_KV_WIDTH], cos[:, 0:SWA_KV_WIDTH], sin[:, 0:SWA_KV_WIDTH],
              SWA_KV_WIDTH)
    v = p[:, GROUP_WIDTH + SWA_KV_WIDTH:SWA_PROJ]
    krot_ref[...] = k.reshape(bb, qb, SWA_KV_WIDTH)
    for bi in range(bb):
        k_scr[bi, wb:wb + qb, :] = k[bi * qb:(bi + 1) * qb]
        v_scr[bi, wb:wb + qb, :] = v[bi * qb:(bi + 1) * qb]

    nk = 2 * wb
    rq = lax.broadcasted_iota(jnp.int32, (2 * qb, nk), 0) % qb
    ck = lax.broadcasted_iota(jnp.int32, (2 * qb, nk), 1)
    rel = rq + wb - ck
    kpos = start + j * qb - wb + ck
    valid = (rel >= 0) & (rel < WINDOW) & (kpos >= 0)
    upper = lax.broadcasted_iota(jnp.int32, (2 * qb, 1), 0) < qb
    sinks = sink_ref[...]
    low_half = lax.broadcasted_iota(jnp.int32, (nk, SWA_KV_WIDTH), 1) < HEAD_DIM

    probs = [(bi, h) for bi in range(bb) for h in range(SWA_KV_HEADS)]
    q2 = [jnp.concatenate([q[bi * qb:(bi + 1) * qb, (2 * h) * HEAD_DIM:(2 * h + 1) * HEAD_DIM],
                           q[bi * qb:(bi + 1) * qb, (2 * h + 1) * HEAD_DIM:(2 * h + 2) * HEAD_DIM]], axis=0)
          for bi, h in probs]
    kh = [k_scr[bi, :, h * HEAD_DIM:(h + 1) * HEAD_DIM] for bi, h in probs]
    va = [jnp.where(low_half if h == 0 else ~low_half, v_scr[bi], 1.0) for bi, h in probs]
    sink = [jnp.where(upper, sinks[:, 2 * h:2 * h + 1], sinks[:, 2 * h + 1:2 * h + 2]) for _, h in probs]
    sc = _each(lambda q_, k_: jnp.where(valid, _mm_nt(q_, k_) * (HEAD_DIM ** -0.5), -jnp.inf), q2, kh)
    mx = _each(lambda s_, z_: jnp.maximum(jnp.max(s_, axis=-1, keepdims=True), z_), sc, sink)
    ex = _each(lambda s_, m_: jnp.exp(s_ - m_), sc, mx)
    ov = _each(_mm, ex, va)
    res = {}
    for (bi, h), o_, m_, z_ in zip(probs, ov, mx, sink):
        num = o_[:, 0:HEAD_DIM] if h == 0 else o_[:, HEAD_DIM:]
        den = (o_[:, HEAD_DIM:HEAD_DIM + 1] if h == 0 else o_[:, 0:1]) + jnp.exp(z_ - m_)
        res[bi, h] = num / den
    rows_out = [jnp.concatenate([res[bi, 0][:qb], res[bi, 0][qb:], res[bi, 1][:qb], res[bi, 1][qb:]], axis=1)
                for bi in range(bb)]
    o_ref[...] = (rows_out[0] if bb == 1 else jnp.concatenate(rows_out, axis=0)).reshape(bb, qb, GROUP_WIDTH)

    if carry:
        for bi in range(bb):
            k_scr[bi, 0:wb, :] = k[bi * qb:(bi + 1) * qb]
            v_scr[bi, 0:wb, :] = v[bi * qb:(bi + 1) * qb]


def _swa(p, cos, sin, kbuf, vbuf, sinks, bb, qb, start):
    bsz, seq, _ = p.shape
    nb = seq // qb
    assert nb == 1 or qb == WINDOW
    blk = lambda w: pl.BlockSpec((bb, qb, w), lambda b, j: (b, j, 0))
    tab = pl.BlockSpec((qb, GROUP_WIDTH), lambda b, j: (j, 0))
    buf = pl.BlockSpec((bb, WINDOW, SWA_KV_WIDTH), lambda b, j: (b, 0, 0))
    return pl.pallas_call(
        functools.partial(_swa_kernel, bb=bb, qb=qb, start=start, carry=nb > 1),
        grid=(bsz // bb, nb),
        in_specs=[blk(SWA_PROJ), tab, tab, buf, buf, _const_spec((1, LANES))],
        out_specs=[blk(GROUP_WIDTH), pl.BlockSpec((bb, qb, SWA_KV_WIDTH), lambda b, j: (b, 0, 0))],
        out_shape=[jax.ShapeDtypeStruct((bsz, seq, GROUP_WIDTH), F32),
                   jax.ShapeDtypeStruct((bsz, qb, SWA_KV_WIDTH), F32)],
        scratch_shapes=[pltpu.VMEM((bb, 2 * WINDOW, SWA_KV_WIDTH), F32),
                        pltpu.VMEM((bb, 2 * WINDOW, SWA_KV_WIDTH), F32)],
        compiler_params=pltpu.CompilerParams(dimension_semantics=("arbitrary", "arbitrary")),
        name="swa",
    )(p, cos, sin, kbuf, vbuf, sinks)


def _rope_tables(start, seq):
    half = ROPE_DIM // 2
    inv = ROPE_THETA ** (-jnp.arange(0, ROPE_DIM, 2, dtype=F32) / ROPE_DIM)
    ang = (start + jnp.arange(seq)).astype(F32)[:, None] * inv[None, :]
    cos, sin = jnp.cos(ang), jnp.sin(ang)
    rest = HEAD_DIM - ROPE_DIM
    cos_h = jnp.concatenate([cos, cos, jnp.ones((seq, rest), F32)], axis=1)
    sin_h = jnp.concatenate([-sin, sin, jnp.zeros((seq, rest), F32)], axis=1)
    return jnp.tile(cos_h, (1, N_HEADS)), jnp.tile(sin_h, (1, N_HEADS))


def _pad_rows(w, top, total):
    return jnp.pad(w, ((top, total - top - w.shape[0]), (0, 0)))


def _block_diag_in(b):
    eye = jnp.eye(S5_GROUPS, dtype=b.dtype)
    return jnp.einsum("gnc,gh->gchn", b, eye).reshape(S5_GROUPS * S5_CH, S5_GROUPS * S5_STATE)


def _block_diag_out(c):
    eye = jnp.eye(S5_GROUPS, dtype=c.dtype)
    return jnp.einsum("gcn,gh->gnhc", c, eye).reshape(S5_GROUPS * S5_STATE, S5_GROUPS * S5_CH)


def _layer_params(l, g_mix_pre, g_mix_post, g_mlp_pre, g_mlp_post, w_in, w_out, rw_mu, rw_w0, rw_w2, rw_a0, rw_a2,
                  rw_g2, rw_kk, rw_ka, rw_rk, rw_ln_w, rw_ln_b, s5_a_re, s5_a_im, s5_log_dt, s5_b_re, s5_b_im,
                  s5_c_re, s5_c_im, s5_d, s5_w_glu, s5_b_glu, gdn_conv_w, gdn_a_log, gdn_dt_bias, gdn_norm_w,
                  swa_sinks, w_up, w_down):
    row = lambda a: a[l].astype(F32).reshape(1, -1)
    wi = w_in[l].astype(F32)
    o_s5 = RW_PROJ
    o_gdn = o_s5 + GROUP_WIDTH
    o_ba = o_gdn + GDN_QKV
    o_z = o_ba + 2 * N_HEADS
    o_swa = o_z + GROUP_WIDTH
    w_all = jnp.concatenate([
        wi[:, :o_s5], wi[:, o_s5:o_gdn], wi[:, o_gdn:o_ba], wi[:, o_z:o_swa],
        jnp.pad(wi[:, o_ba:o_z], ((0, 0), (0, LANES - 2 * N_HEADS))), wi[:, o_swa:]], axis=1).astype(BF16)
    lane_pad = lambda a: jnp.pad(a[l].astype(F32), (N_HEADS, LANES - 2 * N_HEADS)).reshape(1, LANES)
    return {
        "g_mix_pre": row(g_mix_pre), "g_mix_post": row(g_mix_post), "g_mlp_pre": row(g_mlp_pre),
        "g_mlp_post": row(g_mlp_post), "w_all": w_all, "w_out": w_out[l].astype(BF16),
        "w_up": w_up[l].astype(BF16), "w_down": w_down[l].astype(BF16),
        "rw": {"mu": row(rw_mu), "w0": row(rw_w0), "a0": row(rw_a0), "kk": row(rw_kk), "ka": row(rw_ka),
               "rk": row(rw_rk), "ln_w": row(rw_ln_w), "ln_b": row(rw_ln_b),
               "w2": _pad_rows(rw_w2[l].astype(F32), 0, GROUP_WIDTH).astype(BF16),
               "a2": _pad_rows(rw_a2[l].astype(F32), RW_DECAY_RANK, GROUP_WIDTH).astype(BF16),
               "g2": _pad_rows(rw_g2[l].astype(F32), RW_DECAY_RANK + RW_ICLR_RANK, GROUP_WIDTH).astype(BF16)},
        "s5": {"a_re": row(s5_a_re), "a_im": row(s5_a_im),
               "log_dt": jnp.repeat(s5_log_dt[l].astype(F32), S5_STATE).reshape(1, S5_LANES),
               "b_re": _block_diag_in(s5_b_re[l].astype(F32)).astype(BF16),
               "b_im": _block_diag_in(s5_b_im[l].astype(F32)).astype(BF16),
               "c_re": _block_diag_out(s5_c_re[l].astype(F32)).astype(BF16),
               "c_im": _block_diag_out(s5_c_im[l].astype(F32)).astype(BF16),
               "d": row(s5_d), "w_glu": s5_w_glu[l].astype(BF16), "b_glu": row(s5_b_glu)},
        "gdn": {"conv_w": gdn_conv_w[l].astype(F32), "a_log": lane_pad(gdn_a_log), "dt_bias": lane_pad(gdn_dt_bias),
                "norm_w": jnp.tile(gdn_norm_w[l].astype(F32), N_HEADS).reshape(1, GROUP_WIDTH)},
        "sinks": jnp.pad(swa_sinks[l].astype(F32), (0, LANES - N_HEADS)).reshape(1, LANES),
    }


def _layer(x, start, st, layer, prm):
    rw_s, rw_shift, s5_re, s5_im, gdn_s, gdn_conv, swa_k, swa_v = st
    bsz, seq, _ = x.shape
    tokens = bsz * seq
    tm = min(512, tokens)
    chunk = CHUNK if seq % CHUNK == 0 else -(-seq // SUBLANES) * SUBLANES
    seq_pad = -(-seq // chunk) * chunk
    n_valid = chunk - (seq_pad - seq)
    n_chunks = math.gcd(seq_pad // chunk, MIX_CHUNKS)
    bb = {name: math.gcd(bsz, pair[seq_pad == chunk]) for name, pair in MIXER_BATCH.items()}
    assert seq_pad == seq or seq_pad == chunk
    assert seq >= GDN_CONV - 1

    x2d = x.reshape(tokens, D_MODEL)
    p_rw, p_s5, p_qkv, p_z, p_ba, p_swa = (
        a.reshape(bsz, seq, -1) for a in _in_proj(x2d, prm["g_mix_pre"], prm["w_all"], tm))
    pad = lambda a: a if seq_pad == seq else jnp.pad(a, ((0, 0), (0, seq_pad - seq), (0, 0)))

    o_rw, rw_s_new = _rwkv(pad(p_rw), rw_shift, rw_s, layer, prm["rw"], bb["rwkv"], chunk, n_chunks, n_valid)
    rw_shift_new = p_rw[:, seq - 1:, :]

    rows = 16
    steps = min(seq, 64)
    o_s5_t, s5_re_new, s5_im_new = _s5(jnp.swapaxes(p_s5, 0, 1), s5_re.reshape(bsz, S5_LANES),
                                       s5_im.reshape(bsz, S5_LANES), prm["s5"], steps, rows)
    o_s5 = jnp.swapaxes(o_s5_t, 0, 1)

    o_gdn, gdn_s_new = _gdn(pad(p_qkv), pad(p_z), pad(p_ba), gdn_conv, gdn_s, layer, prm["gdn"],
                            bb["gdn"], chunk, n_chunks, n_valid)
    gdn_conv_new = p_qkv[:, seq - (GDN_CONV - 1):, :]

    qb = WINDOW if seq % WINDOW == 0 else seq_pad
    cos, sin = _rope_tables(start, seq_pad)
    wbuf = swa_k.shape[1]
    o_swa, k_rot = _swa(pad(p_swa), cos, sin, swa_k.reshape(bsz, wbuf, SWA_KV_WIDTH),
                        swa_v.reshape(bsz, wbuf, SWA_KV_WIDTH), prm["sinks"], bb["swa"], qb, start)
    k_new = k_rot[:, :qb - (seq_pad - seq), :] if seq_pad != seq else k_rot
    v_new = p_swa[:, seq - min(seq, wbuf):, GROUP_WIDTH + SWA_KV_WIDTH:]
    kv_shape = (bsz, -1, SWA_KV_HEADS, HEAD_DIM)
    if k_new.shape[1] >= wbuf:
        swa_k_new = k_new[:, -wbuf:].reshape(kv_shape)
        swa_v_new = v_new[:, -wbuf:].reshape(kv_shape)
    else:
        swa_k_new = jnp.concatenate([swa_k, k_new.reshape(kv_shape)], axis=1)[:, -wbuf:]
        swa_v_new = jnp.concatenate([swa_v, v_new.reshape(kv_shape)], axis=1)[:, -wbuf:]

    flat = lambda a: a[:, :seq, :].reshape(tokens, GROUP_WIDTH)
    y = _post(x2d, (flat(o_rw), flat(o_s5), flat(o_gdn), flat(o_swa)), prm["w_out"], prm["g_mix_post"],
              prm["g_mlp_pre"], prm["w_up"], prm["w_down"], prm["g_mlp_post"], tm)
    new_state = (rw_s_new, rw_shift_new, s5_re_new.reshape(s5_re.shape), s5_im_new.reshape(s5_im.shape),
                 gdn_s_new, gdn_conv_new, swa_k_new, swa_v_new)
    return y.reshape(bsz, seq, D_MODEL), new_state


def kernel(x_prompt, x_sample, state_rwkv, state_rwkv_shift, state_s5_re, state_s5_im, state_gdn, state_gdn_conv,
           cache_swa_k, cache_swa_v, g_mix_pre, g_mix_post, g_mlp_pre, g_mlp_post, w_in, w_out, rw_mu, rw_w0, rw_w2,
           rw_a0, rw_a2, rw_g2, rw_kk, rw_ka, rw_rk, rw_ln_w, rw_ln_b, s5_a_re, s5_a_im, s5_log_dt, s5_b_re, s5_b_im,
           s5_c_re, s5_c_im, s5_d, s5_w_glu, s5_b_glu, gdn_conv_w, gdn_a_log, gdn_dt_bias, gdn_norm_w, swa_sinks,
           w_up, w_down):
    weights = (g_mix_pre, g_mix_post, g_mlp_pre, g_mlp_post, w_in, w_out, rw_mu, rw_w0, rw_w2, rw_a0, rw_a2, rw_g2,
               rw_kk, rw_ka, rw_rk, rw_ln_w, rw_ln_b, s5_a_re, s5_a_im, s5_log_dt, s5_b_re, s5_b_im, s5_c_re, s5_c_im,
               s5_d, s5_w_glu, s5_b_glu, gdn_conv_w, gdn_a_log, gdn_dt_bias, gdn_norm_w, swa_sinks, w_up, w_down)
    depth = w_in.shape[0]
    bp = x_prompt.shape[0]
    past_len = 16384
    xp = x_prompt.astype(F32)
    xs = x_sample.astype(F32)
    zp = lambda *shape: jnp.zeros((bp,) + shape, F32)
    zero_s = jnp.zeros((1, bp, N_HEADS, HEAD_DIM, HEAD_DIM), F32)
    prompt_init = (zero_s, zp(1, RW_PROJ), zp(S5_GROUPS, S5_STATE), zp(S5_GROUPS, S5_STATE),
                   zero_s, zp(GDN_CONV - 1, GDN_QKV),
                   zp(WINDOW, SWA_KV_HEADS, HEAD_DIM), zp(WINDOW, SWA_KV_HEADS, HEAD_DIM))
    sample_states = (state_rwkv, state_rwkv_shift, state_s5_re, state_s5_im, state_gdn, state_gdn_conv,
                     cache_swa_k, cache_swa_v)
    new_p, new_s = [], []
    for l in range(depth):
        prm = _layer_params(l, *weights)
        xp, st_p = _layer(xp, 0, prompt_init, 0, prm)
        stacked = (0, 4)
        st_l = tuple(a.astype(F32) if i in stacked else a[l].astype(F32) for i, a in enumerate(sample_states))
        xs, st_s = _layer(xs, past_len, st_l, l, prm)
        new_p.append(st_p)
        new_s.append(st_s)
    outs = [xp.astype(x_prompt.dtype), xs.astype(x_sample.dtype)]
    for i in range(len(sample_states)):
        outs.append(jnp.stack([st[i] for st in new_p], axis=0))
        outs.append(jnp.stack([st[i] for st in new_s], axis=0))
    return tuple(outs)
```

```python
import functools
import math

import jax
import jax.numpy as jnp
from jax import lax
from jax.experimental import pallas as pl
from jax.experimental.pallas import tpu as pltpu

F32 = jnp.float32
BF16 = jnp.bfloat16

D_MODEL = 1024
HEAD_DIM = 64
N_HEADS = 4
GROUP_WIDTH = N_HEADS * HEAD_DIM
NORM_EPS = 1e-6
RW_LN_EPS = 64e-5
RW_PROJ = 1024
RW_LORA_OFF = 3 * GROUP_WIDTH
RW_DECAY_RANK, RW_ICLR_RANK, RW_GATE_RANK = 64, 64, 128
S5_GROUPS, S5_CH, S5_STATE = 16, 16, 64
S5_LANES = S5_GROUPS * S5_STATE
GDN_CONV = 4
GDN_QKV = 3 * GROUP_WIDTH
SWA_KV_HEADS = 2
SWA_KV_WIDTH = SWA_KV_HEADS * HEAD_DIM
SWA_PROJ = GROUP_WIDTH + 2 * SWA_KV_WIDTH
WINDOW = 128
ROPE_DIM = 16
ROPE_THETA = 500000.0
D_FF = 4096
LANES = 128
SUBLANES = 8
TOKEN_ROWS = 1024
CHUNK = 64
MIX_CHUNKS = 4
MIXER_BATCH = {"rwkv": (4, 16), "gdn": (4, 16), "swa": (4, 32)}
TRI_BLOCK = 16
VMEM_LIMIT = 56 * 1024 * 1024

PROJ_WIDTHS = (RW_PROJ, GROUP_WIDTH, GDN_QKV, GROUP_WIDTH, LANES, SWA_PROJ)


def _dot(a, b, dims):
    return lax.dot_general(a.astype(BF16), b.astype(BF16), (dims, ((), ())), preferred_element_type=F32)


def _mm(a, b):
    return _dot(a, b, ((1,), (0,)))


def _mm_nt(a, b):
    return _dot(a, b, ((1,), (1,)))


def _mm_tn(a, b):
    return _dot(a, b, ((0,), (0,)))


def _split2(x):
    hi = x.astype(BF16)
    lo = (x - hi.astype(F32)).astype(BF16)
    return hi, lo


def _mm_exact_rhs(x, m):
    hi, lo = _split2(x)
    return _mm(hi, m) + _mm(lo, m)


def _mm_exact_lhs(m, x):
    hi, lo = _split2(x)
    return _mm(m, hi) + _mm(m, lo)


def _head_ones(width):
    r = lax.broadcasted_iota(jnp.int32, (width, width), 0)
    c = lax.broadcasted_iota(jnp.int32, (width, width), 1)
    return (r // HEAD_DIM == c // HEAD_DIM).astype(BF16)


def _head_sum(x, ones):
    return _mm(x, ones)


def _tri_masks(n):
    r = lax.broadcasted_iota(jnp.int32, (n, n), 0)
    c = lax.broadcasted_iota(jnp.int32, (n, n), 1)
    return r, c, r > c, r >= c


def _softplus(x):
    return jnp.maximum(x, 0.0) + jnp.log(1.0 + jnp.exp(-jnp.abs(x)))


def _silu(x):
    return x * jax.nn.sigmoid(x)


def _rms(x, g):
    return x * lax.rsqrt(jnp.mean(x * x, axis=-1, keepdims=True) + NORM_EPS) * g


def _each(fn, *lists):
    return [fn(*xs) for xs in zip(*lists)]


def _tri_inv(lows, n):
    r, c, _, _ = _tri_masks(n)
    eye = (r == c).astype(F32)
    bs = min(TRI_BLOCK, n)
    if n > bs:
        same = (r // bs) == (c // bs)
        diags = [jnp.where(same, low, 0.0) for low in lows]
        rests = _each(lambda low, d: low - d, lows, diags)
    else:
        diags, rests = lows, None
    invs = [eye - d for d in diags]
    pws, p = _each(lambda x: _mm(x, x), diags), 2
    while 2 * p < bs:
        both = _each(lambda x, i: _mm(jnp.concatenate([x, i], axis=0), x), pws, invs)
        pws = [z[:n] for z in both]
        invs = _each(lambda i, z: i + z[n:], invs, both)
        p *= 2
    invs = _each(lambda i, x: i + _mm(i, x), invs, pws)
    if rests is None:
        return invs
    nils = _each(_mm, invs, rests)
    outs = [eye - x for x in nils]
    for _ in range(n // bs - 2):
        outs = _each(lambda x, o: eye - _mm(x, o), nils, outs)
    return _each(_mm, outs, invs)


def _const_spec(shape):
    zeros = (0,) * len(shape)
    return pl.BlockSpec(shape, lambda *_: zeros, pipeline_mode=pl.Buffered(1))


def _in_proj_kernel(x_ref, g_ref, w_ref, *out_refs):
    h = _rms(x_ref[...], g_ref[...]).astype(BF16)
    off = 0
    for o_ref, width in zip(out_refs, PROJ_WIDTHS):
        o_ref[...] = _mm(h, w_ref[:, off:off + width])
        off += width


def _in_proj(x2d, g, w_all, tm):
    t = x2d.shape[0]
    wtot = sum(PROJ_WIDTHS)
    return pl.pallas_call(
        _in_proj_kernel,
        grid=(t // tm,),
        in_specs=[pl.BlockSpec((tm, D_MODEL), lambda i: (i, 0)),
                  _const_spec((1, D_MODEL)),
                  _const_spec((D_MODEL, wtot))],
        out_specs=[pl.BlockSpec((tm, w), lambda i: (i, 0)) for w in PROJ_WIDTHS],
        out_shape=[jax.ShapeDtypeStruct((t, w), F32) for w in PROJ_WIDTHS],
        compiler_params=pltpu.CompilerParams(dimension_semantics=("arbitrary",), vmem_limit_bytes=VMEM_LIMIT),
        name="in_proj",
    )(x2d, g, w_all)


FF_CHUNK = 1024


def _post_kernel(x_ref, o0_ref, o1_ref, o2_ref, o3_ref, wout_ref, gpost_ref, gpre_ref, wup_ref, wdn_ref,
                 gmlp_ref, y_ref):
    mix = None
    for i, o_ref in enumerate((o0_ref, o1_ref, o2_ref, o3_ref)):
        part = _mm(o_ref[...].astype(BF16), wout_ref[i * GROUP_WIDTH:(i + 1) * GROUP_WIDTH, :])
        mix = part if mix is None else mix + part
    x1 = x_ref[...] + _rms(mix, gpost_ref[...])
    h = _rms(x1, gpre_ref[...]).astype(BF16)
    acc = None
    for c in range(D_FF // FF_CHUNK):
        u = _mm(h, wup_ref[:, c * FF_CHUNK:(c + 1) * FF_CHUNK])
        u = jnp.square(jnp.maximum(u, 0.0)).astype(BF16)
        part = _mm(u, wdn_ref[c * FF_CHUNK:(c + 1) * FF_CHUNK, :])
        acc = part if acc is None else acc + part
    y_ref[...] = x1 + _rms(acc, gmlp_ref[...])


def _post(x2d, outs, w_out, g_post, g_pre, w_up, w_dn, g_mlp, tm):
    t = x2d.shape[0]
    row = lambda w: pl.BlockSpec((tm, w), lambda i: (i, 0))
    return pl.pallas_call(
        _post_kernel,
        grid=(t // tm,),
        in_specs=[row(D_MODEL)] + [row(GROUP_WIDTH)] * 4 + [
            _const_spec((D_MODEL, D_MODEL)), _const_spec((1, D_MODEL)), _const_spec((1, D_MODEL)),
            _const_spec((D_MODEL, D_FF)), _const_spec((D_FF, D_MODEL)), _const_spec((1, D_MODEL))],
        out_specs=row(D_MODEL),
        out_shape=jax.ShapeDtypeStruct((t, D_MODEL), F32),
        compiler_params=pltpu.CompilerParams(dimension_semantics=("arbitrary",), vmem_limit_bytes=VMEM_LIMIT),
        name="post_mlp",
    )(x2d, *outs, w_out, g_post, g_pre, w_up, w_dn, g_mlp)


def _chunk_sums(x, chunk, rows):
    _, _, _, incl = _tri_masks(chunk)
    tri = incl.astype(BF16)
    cums = [_mm_exact_lhs(tri, x[c * chunk:(c + 1) * chunk]) for c in range(rows // chunk)]
    if len(cums) == 1:
        return cums[0], cums[0][chunk - 1:chunk, :]
    tots = [jnp.broadcast_to(cu[chunk - 1:chunk, :], cu.shape) for cu in cums]
    return jnp.concatenate(cums, axis=0), jnp.concatenate(tots, axis=0)


def _rwkv_kernel(p_ref, shift_ref, s0_ref, mu_ref, w0_ref, a0_ref, kkp_ref, kap_ref, rk_ref, lnw_ref, lnb_ref,
                 w2_ref, a2_ref, g2_ref, o_ref, s_out_ref, s_scr, carry_scr, *, bb, chunk, n_chunks, n_valid):
    c_idx = pl.program_id(1)
    rows = chunk * n_chunks
    total = bb * rows

    @pl.when(c_idx == 0)
    def _():
        s_scr[...] = s0_ref[...]
        carry_scr[...] = shift_ref[...]

    first = lax.broadcasted_iota(jnp.int32, (rows, 1), 0) == 0
    last = rows - chunk + n_valid
    prevs = []
    for bi in range(bb):
        p_b = p_ref[bi]
        prevs.append(jnp.where(first, carry_scr[bi], pltpu.roll(p_b, 1, 0)))
        carry_scr[bi] = p_b[last - 1:last, :]
    p = p_ref[...].reshape(total, RW_PROJ)
    prev = prevs[0] if bb == 1 else jnp.concatenate(prevs, axis=0)
    row = lax.broadcasted_iota(jnp.int32, (total, 1), 0)
    xs = p + (prev - p) * mu_ref[...]
    r = xs[:, 0:GROUP_WIDTH]
    k = xs[:, GROUP_WIDTH:2 * GROUP_WIDTH]
    v = xs[:, 2 * GROUP_WIDTH:3 * GROUP_WIDTH]
    lora = xs[:, RW_LORA_OFF:RW_PROJ]

    z_w = w0_ref[...] + _mm(jnp.tanh(lora), w2_ref[...])
    logw = -jnp.exp(-_softplus(-z_w) - 0.5)
    a = jax.nn.sigmoid(a0_ref[...] + _mm(lora, a2_ref[...]))
    gate = _mm(jax.nn.sigmoid(lora), g2_ref[...])

    ones = _head_ones(GROUP_WIDTH)
    kk = k * kkp_ref[...]
    kk = kk * lax.rsqrt(jnp.maximum(_head_sum(kk * kk, ones), 1e-12))
    k2 = k * (1.0 + (a - 1.0) * kap_ref[...])
    b = kk * a
    if n_valid < chunk:
        valid = (row % chunk) < n_valid
        logw = jnp.where(valid, logw, 0.0)
        kk = jnp.where(valid, kk, 0.0)
        b = jnp.where(valid, b, 0.0)
        k2v = jnp.where(valid, k2, 0.0)
        vv = jnp.where(valid, v, 0.0)
    else:
        k2v, vv = k2, v

    log_cum, log_end = _chunk_sums(logw, chunk, total)
    w_inv = jnp.exp(-log_cum)
    w_tail = jnp.exp(log_end - log_cum)
    a_bar = kk * jnp.exp(log_cum - logw)
    b_bar = b * w_inv
    k_bar = k2v * w_inv
    r_bar = r * jnp.exp(log_cum)
    b_end = b * w_tail
    k_end = k2v * w_tail
    w_end = jnp.exp(log_end)

    _, _, strict, incl = _tri_masks(chunk)
    group = bb * N_HEADS
    tiles = [(slice(bi * rows + c * chunk, bi * rows + (c + 1) * chunk), slice(h * HEAD_DIM, (h + 1) * HEAD_DIM))
             for c in range(n_chunks) for bi in range(bb) for h in range(N_HEADS)]
    cut = lambda x: [x[rs, sl] for rs, sl in tiles]
    ab, rb, bs, kb, be, ke, vh = map(cut, (a_bar, r_bar, b_bar, k_bar, b_end, k_end, vv))
    gram = _each(lambda a_, r_, b_, k_: _mm_nt(jnp.concatenate([a_, r_], axis=0), jnp.concatenate([b_, k_], axis=0)),
                 ab, rb, bs, kb)
    l_ab = [jnp.where(strict, g[:chunk, :chunk], 0.0) for g in gram]
    l_ak = [jnp.where(strict, g[:chunk, chunk:], 0.0) for g in gram]
    m_rb = [jnp.where(incl, g[chunk:, :chunk], 0.0) for g in gram]
    m_rk = [jnp.where(incl, g[chunk:, chunk:], 0.0) for g in gram]
    tinv = _tri_inv(l_ab, chunk)
    lv = _each(_mm, l_ak, vh)
    tt = _each(lambda t_, a_, l_: _mm(t_, jnp.concatenate([a_, l_], axis=1)), tinv, ab, lv)
    ta_c0 = [jnp.concatenate([t_[:, :HEAD_DIM], -t_[:, HEAD_DIM:]], axis=1) for t_ in tt]
    pq = _each(_mm_tn, ta_c0, be)
    vk = _each(_mm_tn, vh, ke)
    ry = _each(_mm, m_rb, ta_c0)
    rkv = _each(_mm, m_rk, vh)
    pb = [x[:HEAD_DIM] for x in pq]
    q_mat = _each(lambda x, y_: x[HEAD_DIM:] + y_, pq, vk)
    rr = _each(lambda r_, x: r_ - x[:, :HEAD_DIM], rb, ry)
    y0 = _each(lambda x, y_: x[:, HEAD_DIM:] + y_, ry, rkv)

    state = [s_scr[bi, h] for bi in range(bb) for h in range(N_HEADS)]
    y_tiles = {}
    for c in range(n_chunks):
        grp = slice(c * group, (c + 1) * group)
        ys = _each(lambda y_, r_, s_: y_ + _mm_nt(r_, s_), y0[grp], rr[grp], state)
        we = [w_end[rs.start:rs.start + 1, sl] for rs, sl in tiles[grp]]
        state = _each(lambda s_, w_, p_, q_: s_ * w_ - _mm(s_, p_) + q_, state, we, pb[grp], q_mat[grp])
        for bi in range(bb):
            y_tiles[bi, c] = jnp.concatenate(ys[bi * N_HEADS:(bi + 1) * N_HEADS], axis=1)
    for bi in range(bb):
        for h in range(N_HEADS):
            s_scr[bi, h] = state[bi * N_HEADS + h]
    y_rows = [y_tiles[bi, c] for bi in range(bb) for c in range(n_chunks)]
    y = y_rows[0] if len(y_rows) == 1 else jnp.concatenate(y_rows, axis=0)

    inv_d = 1.0 / HEAD_DIM
    mean = _head_sum(y, ones) * inv_d
    d = y - mean
    var = _head_sum(d * d, ones) * inv_d
    yn = d * lax.rsqrt(var + RW_LN_EPS) * lnw_ref[...] + lnb_ref[...]
    bonus = _head_sum(r * k2 * rk_ref[...], ones) * v
    o_ref[...] = ((yn + bonus) * gate).reshape(bb, rows, GROUP_WIDTH)

    @pl.when(c_idx == pl.num_programs(1) - 1)
    def _():
        s_out_ref[...] = s_scr[...]


def _state_specs(bb, layer):
    shape = (bb, N_HEADS, HEAD_DIM, HEAD_DIM)
    return (pl.BlockSpec((None,) + shape, lambda b, c: (layer, b, 0, 0, 0)),
            pl.BlockSpec(shape, lambda b, c: (b, 0, 0, 0)))


def _rwkv(p, shift, s0, layer, prm, bb, chunk, n_chunks, n_valid):
    bsz, seq, _ = p.shape
    rows = chunk * n_chunks
    assert n_valid == chunk or n_chunks == 1
    vec = _const_spec((1, GROUP_WIDTH))
    mat = _const_spec((GROUP_WIDTH, GROUP_WIDTH))
    state_in, state = _state_specs(bb, layer)
    return pl.pallas_call(
        functools.partial(_rwkv_kernel, bb=bb, chunk=chunk, n_chunks=n_chunks, n_valid=n_valid),
        grid=(bsz // bb, seq // rows),
        in_specs=[pl.BlockSpec((bb, rows, RW_PROJ), lambda b, c: (b, c, 0)),
                  pl.BlockSpec((bb, 1, RW_PROJ), lambda b, c: (b, 0, 0)),
                  state_in, _const_spec((1, RW_PROJ))] + [vec] * 7 + [mat] * 3,
        out_specs=[pl.BlockSpec((bb, rows, GROUP_WIDTH), lambda b, c: (b, c, 0)), state],
        out_shape=[jax.ShapeDtypeStruct((bsz, seq, GROUP_WIDTH), F32),
                   jax.ShapeDtypeStruct(s0.shape[1:], F32)],
        scratch_shapes=[pltpu.VMEM((bb, N_HEADS, HEAD_DIM, HEAD_DIM), F32), pltpu.VMEM((bb, 1, RW_PROJ), F32)],
        compiler_params=pltpu.CompilerParams(dimension_semantics=("arbitrary", "arbitrary")),
        name="rwkv7",
    )(p, shift, s0, prm["mu"], prm["w0"], prm["a0"], prm["kk"], prm["ka"], prm["rk"], prm["ln_w"], prm["ln_b"],
      prm["w2"], prm["a2"], prm["g2"])


def _gdn_kernel(qkv_ref, z_ref, ba_ref, buf_ref, s0_ref, convw_ref, alog_ref, dtb_ref, nw_ref,
                o_ref, s_out_ref, s_scr, x_scr, *, bb, chunk, n_chunks, n_valid):
    c_idx = pl.program_id(1)
    rows = chunk * n_chunks
    total = bb * rows
    pad = SUBLANES

    @pl.when(c_idx == 0)
    def _():
        s_scr[...] = s0_ref[...]
        x_scr[:, 0:pad, :] = jnp.zeros((bb, pad, GDN_QKV), F32)
        x_scr[:, pad - (GDN_CONV - 1):pad, :] = buf_ref[...]

    last = rows - chunk + n_valid
    convs = []
    for bi in range(bb):
        x_scr[bi, pad:pad + rows, :] = qkv_ref[bi]
        conv = None
        for j in range(GDN_CONV):
            term = x_scr[bi, pad - j:pad - j + rows, :] * convw_ref[GDN_CONV - 1 - j:GDN_CONV - j, :]
            conv = term if conv is None else conv + term
        x_scr[bi, 0:pad, :] = x_scr[bi, last:last + pad, :]
        convs.append(conv)
    act = _silu(convs[0] if bb == 1 else jnp.concatenate(convs, axis=0))
    q = act[:, 0:GROUP_WIDTH]
    k = act[:, GROUP_WIDTH:2 * GROUP_WIDTH]
    v = act[:, 2 * GROUP_WIDTH:3 * GROUP_WIDTH]
    ones = _head_ones(GROUP_WIDTH)
    q = q * lax.rsqrt(jnp.maximum(_head_sum(q * q, ones), 1e-12)) * (HEAD_DIM ** -0.5)
    k = k * lax.rsqrt(jnp.maximum(_head_sum(k * k, ones), 1e-12))

    ba = ba_ref[...].reshape(total, LANES)
    row = lax.broadcasted_iota(jnp.int32, (total, LANES), 0)
    beta = jax.nn.sigmoid(ba)
    glog = -jnp.exp(alog_ref[...]) * _softplus(ba + dtb_ref[...])
    if n_valid < chunk:
        valid = (row % chunk) < n_valid
        beta = jnp.where(valid, beta, 0.0)
        glog = jnp.where(valid, glog, 0.0)
    gcum, gend = _chunk_sums(glog, chunk, total)
    gcum_t = gcum.T
    er = lax.broadcasted_iota(jnp.int32, (LANES, GROUP_WIDTH), 0)
    ec = lax.broadcasted_iota(jnp.int32, (LANES, GROUP_WIDTH), 1) // HEAD_DIM
    beta_w = _mm_exact_rhs(beta, (er == ec).astype(BF16))
    spread = (er == ec + N_HEADS).astype(BF16)
    gcum_w = _mm_exact_rhs(gcum, spread)
    gend_w = _mm_exact_rhs(gend, spread)
    gam_w = jnp.exp(gcum_w)
    bgk = beta_w * gam_w * k
    bv = beta_w * v
    gq = gam_w * q
    kt = k * jnp.exp(gend_w - gcum_w)
    end_w = jnp.exp(gend_w)

    _, _, strict, incl = _tri_masks(chunk)
    group = bb * N_HEADS
    tiles = [(slice(bi * rows + c * chunk, bi * rows + (c + 1) * chunk), h)
             for c in range(n_chunks) for bi in range(bb) for h in range(N_HEADS)]
    head = lambda x: [x[rs, h * HEAD_DIM:(h + 1) * HEAD_DIM] for rs, h in tiles]
    sq = lambda x: [x[rs, h * HEAD_DIM:h * HEAD_DIM + chunk] for rs, h in tiles]
    k_h, q_h, bgk_h, bv_h, gq_h, kt_h = map(head, (k, q, bgk, bv, gq, kt))
    g_row = [gcum_t[N_HEADS + h:N_HEADS + h + 1, rs] for rs, h in tiles]
    decay = _each(lambda gc, gr: jnp.where(incl, jnp.exp(jnp.where(incl, gc - gr, 0.0)), 0.0), sq(gcum_w), g_row)
    gram = _each(lambda k_, q_: _mm_nt(jnp.concatenate([k_, q_], axis=0), k_), k_h, q_h)
    a_mat = _each(lambda b_, d_, g_: jnp.where(strict, b_ * d_ * g_[:chunk], 0.0), sq(beta_w), decay, gram)
    tinv = _tri_inv(a_mat, chunk)
    tt = _each(lambda t_, x, y_: _mm(t_, jnp.concatenate([x, y_], axis=1)), tinv, bgk_h, bv_h)
    qo = _each(lambda g_, d_, t_: _mm(g_[chunk:] * d_, t_), gram, decay, tt)
    pq = _each(_mm_tn, tt, kt_h)
    qq = _each(lambda g_, x: g_ - x[:, :HEAD_DIM], gq_h, qo)
    o0 = [x[:, HEAD_DIM:] for x in qo]
    pb = [x[:HEAD_DIM] for x in pq]
    q_mat = [x[HEAD_DIM:] for x in pq]

    state = [s_scr[bi, h] for bi in range(bb) for h in range(N_HEADS)]
    o_tiles = {}
    for c in range(n_chunks):
        grp = slice(c * group, (c + 1) * group)
        os_ = _each(lambda o_, q_, s_: o_ + _mm_nt(q_, s_), o0[grp], qq[grp], state)
        ge = [end_w[rs.start:rs.start + 1, h * HEAD_DIM:(h + 1) * HEAD_DIM] for rs, h in tiles[grp]]
        state = _each(lambda s_, g_, p_, q_: s_ * g_ - _mm(s_, p_) + q_, state, ge, pb[grp], q_mat[grp])
        for bi in range(bb):
            o_tiles[bi, c] = jnp.concatenate(os_[bi * N_HEADS:(bi + 1) * N_HEADS], axis=1)
    for bi in range(bb):
        for h in range(N_HEADS):
            s_scr[bi, h] = state[bi * N_HEADS + h]
    o_rows = [o_tiles[bi, c] for bi in range(bb) for c in range(n_chunks)]
    o = o_rows[0] if len(o_rows) == 1 else jnp.concatenate(o_rows, axis=0)
    ms = _head_sum(o * o, ones) * (1.0 / HEAD_DIM)
    zz = z_ref[...].reshape(total, GROUP_WIDTH)
    o_ref[...] = (o * lax.rsqrt(ms + NORM_EPS) * nw_ref[...] * _silu(zz)).reshape(bb, rows, GROUP_WIDTH)

    @pl.when(c_idx == pl.num_programs(1) - 1)
    def _():
        s_out_ref[...] = s_scr[...]


def _gdn(qkv, z, ba, buf, s0, layer, prm, bb, chunk, n_chunks, n_valid):
    bsz, seq, _ = qkv.shape
    rows = chunk * n_chunks
    assert n_valid == chunk or n_chunks == 1
    state_in, state = _state_specs(bb, layer)
    blk = lambda w: pl.BlockSpec((bb, rows, w), lambda b, c: (b, c, 0))
    return pl.pallas_call(
        functools.partial(_gdn_kernel, bb=bb, chunk=chunk, n_chunks=n_chunks, n_valid=n_valid),
        grid=(bsz // bb, seq // rows),
        in_specs=[blk(GDN_QKV), blk(GROUP_WIDTH), blk(LANES),
                  pl.BlockSpec((bb, GDN_CONV - 1, GDN_QKV), lambda b, c: (b, 0, 0)), state_in,
                  _const_spec((GDN_CONV, GDN_QKV)), _const_spec((1, LANES)), _const_spec((1, LANES)),
                  _const_spec((1, GROUP_WIDTH))],
        out_specs=[blk(GROUP_WIDTH), state],
        out_shape=[jax.ShapeDtypeStruct((bsz, seq, GROUP_WIDTH), F32), jax.ShapeDtypeStruct(s0.shape[1:], F32)],
        scratch_shapes=[pltpu.VMEM((bb, N_HEADS, HEAD_DIM, HEAD_DIM), F32),
                        pltpu.VMEM((bb, SUBLANES + rows, GDN_QKV), F32)],
        compiler_params=pltpu.CompilerParams(dimension_semantics=("arbitrary", "arbitrary")),
        name="gdn",
    )(qkv, z, ba, buf, s0, prm["conv_w"], prm["a_log"], prm["dt_bias"], prm["norm_w"])


def _gelu_tanh(x):
    return 0.5 * x * (1.0 + jnp.tanh(math.sqrt(2.0 / math.pi) * (x + 0.044715 * (x * x * x))))


def _s5_kernel(u_ref, x0re_ref, x0im_ref, are_ref, aim_ref, ldt_ref, bre_ref, bim_ref, cre_ref, cim_ref, d_ref,
               wglu_ref, bglu_ref, o_ref, hre_out_ref, him_out_ref, hre_scr, him_scr, sre_scr, sim_scr, *, steps, rows):
    c_idx = pl.program_id(1)

    @pl.when(c_idx == 0)
    def _():
        sre_scr[...] = x0re_ref[...]
        sim_scr[...] = x0im_ref[...]

    a_re, a_im = are_ref[...], aim_ref[...]
    dt = jnp.exp(ldt_ref[...])
    mag = jnp.exp(dt * a_re)
    ab_re = mag * jnp.cos(dt * a_im)
    ab_im = mag * jnp.sin(dt * a_im)
    den = a_re * a_re + a_im * a_im
    nr = ab_re - 1.0
    cf_re = (nr * a_re + ab_im * a_im) / den
    cf_im = (ab_im * a_re - nr * a_im) / den

    u = u_ref[...].reshape(steps * rows, GROUP_WIDTH)
    ub = u.astype(BF16)
    bu_re = _mm(ub, bre_ref[...])
    bu_im = _mm(ub, bim_ref[...])
    hre_scr[...] = cf_re * bu_re - cf_im * bu_im
    him_scr[...] = cf_re * bu_im + cf_im * bu_re

    abr = jnp.broadcast_to(ab_re, (rows, S5_LANES))
    abi = jnp.broadcast_to(ab_im, (rows, S5_LANES))

    def step(t, carry):
        h_re, h_im = carry
        base = pl.multiple_of(t * rows, rows)
        n_re = abr * h_re - abi * h_im + hre_scr[pl.ds(base, rows), :]
        n_im = abr * h_im + abi * h_re + him_scr[pl.ds(base, rows), :]
        hre_scr[pl.ds(base, rows), :] = n_re
        him_scr[pl.ds(base, rows), :] = n_im
        return n_re, n_im

    h_re, h_im = lax.fori_loop(0, steps, step, (sre_scr[...], sim_scr[...]))
    sre_scr[...] = h_re
    sim_scr[...] = h_im

    y = (_mm(hre_scr[...].astype(BF16), cre_ref[...]) - _mm(him_scr[...].astype(BF16), cim_ref[...])
         + d_ref[...] * u)
    zz = _gelu_tanh(y)
    out = zz * jax.nn.sigmoid(_mm(zz.astype(BF16), wglu_ref[...]) + bglu_ref[...])
    o_ref[...] = out.reshape(steps, rows, GROUP_WIDTH)

    @pl.when(c_idx == pl.num_programs(1) - 1)
    def _():
        hre_out_ref[...] = h_re
        him_out_ref[...] = h_im


def _s5(u_t, x0_re, x0_im, prm, steps, rows):
    seq, bsz, _ = u_t.shape
    st = pl.BlockSpec((rows, S5_LANES), lambda b, c: (b, 0))
    vec = _const_spec((1, S5_LANES))
    return pl.pallas_call(
        functools.partial(_s5_kernel, steps=steps, rows=rows),
        grid=(bsz // rows, seq // steps),
        in_specs=[pl.BlockSpec((steps, rows, GROUP_WIDTH), lambda b, c: (c, b, 0)), st, st, vec, vec, vec,
                  _const_spec((GROUP_WIDTH, S5_LANES)), _const_spec((GROUP_WIDTH, S5_LANES)),
                  _const_spec((S5_LANES, GROUP_WIDTH)), _const_spec((S5_LANES, GROUP_WIDTH)),
                  _const_spec((1, GROUP_WIDTH)), _const_spec((GROUP_WIDTH, GROUP_WIDTH)),
                  _const_spec((1, GROUP_WIDTH))],
        out_specs=[pl.BlockSpec((steps, rows, GROUP_WIDTH), lambda b, c: (c, b, 0)), st, st],
        out_shape=[jax.ShapeDtypeStruct((seq, bsz, GROUP_WIDTH), F32),
                   jax.ShapeDtypeStruct((bsz, S5_LANES), F32), jax.ShapeDtypeStruct((bsz, S5_LANES), F32)],
        scratch_shapes=[pltpu.VMEM((steps * rows, S5_LANES), F32), pltpu.VMEM((steps * rows, S5_LANES), F32),
                        pltpu.VMEM((rows, S5_LANES), F32), pltpu.VMEM((rows, S5_LANES), F32)],
        compiler_params=pltpu.CompilerParams(dimension_semantics=("arbitrary", "arbitrary"),
                                             vmem_limit_bytes=VMEM_LIMIT),
        name="s5",
    )(u_t, x0_re, x0_im, prm["a_re"], prm["a_im"], prm["log_dt"], prm["b_re"], prm["b_im"], prm["c_re"],
      prm["c_im"], prm["d"], prm["w_glu"], prm["b_glu"])


def _rope(x, cos, sin, width):
    fwd = pltpu.roll(x, width - ROPE_DIM // 2, 1)
    bwd = pltpu.roll(x, ROPE_DIM // 2, 1)
    lane = lax.broadcasted_iota(jnp.int32, x.shape, 1) % HEAD_DIM
    return x * cos + jnp.where(lane < ROPE_DIM // 2, fwd, bwd) * sin


def _swa_kernel(p_ref, cos_ref, sin_ref, kbuf_ref, vbuf_ref, sink_ref, o_ref, krot_ref, k_scr, v_scr,
                *, bb, qb, start, carry):
    j = pl.program_id(1)
    wb = WINDOW

    @pl.when(j == 0)
    def _():
        k_scr[:, 0:wb, :] = kbuf_ref[...]
        v_scr[:, 0:wb, :] = vbuf_ref[...]
        if qb < wb:
            k_scr[:, wb:2 * wb, :] = jnp.zeros((bb, wb, SWA_KV_WIDTH), F32)
            v_scr[:, wb:2 * wb, :] = jnp.zeros((bb, wb, SWA_KV_WIDTH), F32)

    p = p_ref[...].reshape(bb * qb, SWA_PROJ)
    cos, sin = cos_ref[...], sin_ref[...]
    if bb > 1:
        cos, sin = jnp.concatenate([cos] * bb, axis=0), jnp.concatenate([sin] * bb, axis=0)
    q = _rope(p[:, 0:GROUP_WIDTH], cos, sin, GROUP_WIDTH)
    k = _rope(p[:, GROUP_WIDTH:GROUP_WIDTH + SWA_KV_WIDTH], cos[:, 0:SWA_KV_WIDTH], sin[:, 0:SWA_KV_WIDTH],
              SWA_KV_WIDTH)
    v = p[:, GROUP_WIDTH + SWA_KV_WIDTH:SWA_PROJ]
    krot_ref[...] = k.reshape(bb, qb, SWA_KV_WIDTH)
    for bi in range(bb):
        k_scr[bi, wb:wb + qb, :] = k[bi * qb:(bi + 1) * qb]
        v_scr[bi, wb:wb + qb, :] = v[bi * qb:(bi + 1) * qb]

    nk = 2 * wb
    rq = lax.broadcasted_iota(jnp.int32, (2 * qb, nk), 0) % qb
    ck = lax.broadcasted_iota(jnp.int32, (2 * qb, nk), 1)
    rel = rq + wb - ck
    kpos = start + j * qb - wb + ck
    valid = (rel >= 0) & (rel < WINDOW) & (kpos >= 0)
    upper = lax.broadcasted_iota(jnp.int32, (2 * qb, 1), 0) < qb
    sinks = sink_ref[...]
    low_half = lax.broadcasted_iota(jnp.int32, (nk, SWA_KV_WIDTH), 1) < HEAD_DIM

    probs = [(bi, h) for bi in range(bb) for h in range(SWA_KV_HEADS)]
    q2 = [jnp.concatenate([q[bi * qb:(bi + 1) * qb, (2 * h) * HEAD_DIM:(2 * h + 1) * HEAD_DIM],
                           q[bi * qb:(bi + 1) * qb, (2 * h + 1) * HEAD_DIM:(2 * h + 2) * HEAD_DIM]], axis=0)
          for bi, h in probs]
    kh = [k_scr[bi, :, h * HEAD_DIM:(h + 1) * HEAD_DIM] for bi, h in probs]
    va = [jnp.where(low_half if h == 0 else ~low_half, v_scr[bi], 1.0) for bi, h in probs]
    sink = [jnp.where(upper, sinks[:, 2 * h:2 * h + 1], sinks[:, 2 * h + 1:2 * h + 2]) for _, h in probs]
    sc = _each(lambda q_, k_: jnp.where(valid, _mm_nt(q_, k_) * (HEAD_DIM ** -0.5), -jnp.inf), q2, kh)
    mx = _each(lambda s_, z_: jnp.maximum(jnp.max(s_, axis=-1, keepdims=True), z_), sc, sink)
    ex = _each(lambda s_, m_: jnp.exp(s_ - m_), sc, mx)
    ov = _each(_mm, ex, va)
    res = {}
    for (bi, h), o_, m_, z_ in zip(probs, ov, mx, sink):
        num = o_[:, 0:HEAD_DIM] if h == 0 else o_[:, HEAD_DIM:]
        den = (o_[:, HEAD_DIM:HEAD_DIM + 1] if h == 0 else o_[:, 0:1]) + jnp.exp(z_ - m_)
        res[bi, h] = num / den
    rows_out = [jnp.concatenate([res[bi, 0][:qb], res[bi, 0][qb:], res[bi, 1][:qb], res[bi, 1][qb:]], axis=1)
                for bi in range(bb)]
    o_ref[...] = (rows_out[0] if bb == 1 else jnp.concatenate(rows_out, axis=0)).reshape(bb, qb, GROUP_WIDTH)

    if carry:
        for bi in range(bb):
            k_scr[bi, 0:wb, :] = k[bi * qb:(bi + 1) * qb]
            v_scr[bi, 0:wb, :] = v[bi * qb:(bi + 1) * qb]


def _swa(p, cos, sin, kbuf, vbuf, sinks, bb, qb, start):
    bsz, seq, _ = p.shape
    nb = seq // qb
    assert nb == 1 or qb == WINDOW
    blk = lambda w: pl.BlockSpec((bb, qb, w), lambda b, j: (b, j, 0))
    tab = pl.BlockSpec((qb, GROUP_WIDTH), lambda b, j: (j, 0))
    buf = pl.BlockSpec((bb, WINDOW, SWA_KV_WIDTH), lambda b, j: (b, 0, 0))
    return pl.pallas_call(
        functools.partial(_swa_kernel, bb=bb, qb=qb, start=start, carry=nb > 1),
        grid=(bsz // bb, nb),
        in_specs=[blk(SWA_PROJ), tab, tab, buf, buf, _const_spec((1, LANES))],
        out_specs=[blk(GROUP_WIDTH), pl.BlockSpec((bb, qb, SWA_KV_WIDTH), lambda b, j: (b, 0, 0))],
        out_shape=[jax.ShapeDtypeStruct((bsz, seq, GROUP_WIDTH), F32),
                   jax.ShapeDtypeStruct((bsz, qb, SWA_KV_WIDTH), F32)],
        scratch_shapes=[pltpu.VMEM((bb, 2 * WINDOW, SWA_KV_WIDTH), F32),
                        pltpu.VMEM((bb, 2 * WINDOW, SWA_KV_WIDTH), F32)],
        compiler_params=pltpu.CompilerParams(dimension_semantics=("arbitrary", "arbitrary")),
        name="swa",
    )(p, cos, sin, kbuf, vbuf, sinks)


def _rope_tables(start, seq):
    half = ROPE_DIM // 2
    inv = ROPE_THETA ** (-jnp.arange(0, ROPE_DIM, 2, dtype=F32) / ROPE_DIM)
    ang = (start + jnp.arange(seq)).astype(F32)[:, None] * inv[None, :]
    cos, sin = jnp.cos(ang), jnp.sin(ang)
    rest = HEAD_DIM - ROPE_DIM
    cos_h = jnp.concatenate([cos, cos, jnp.ones((seq, rest), F32)], axis=1)
    sin_h = jnp.concatenate([-sin, sin, jnp.zeros((seq, rest), F32)], axis=1)
    return jnp.tile(cos_h, (1, N_HEADS)), jnp.tile(sin_h, (1, N_HEADS))


def _pad_rows(w, top, total):
    return jnp.pad(w, ((top, total - top - w.shape[0]), (0, 0)))


def _block_diag_in(b):
    eye = jnp.eye(S5_GROUPS, dtype=b.dtype)
    return jnp.einsum("gnc,gh->gchn", b, eye).reshape(S5_GROUPS * S5_CH, S5_GROUPS * S5_STATE)


def _block_diag_out(c):
    eye = jnp.eye(S5_GROUPS, dtype=c.dtype)
    return jnp.einsum("gcn,gh->gnhc", c, eye).reshape(S5_GROUPS * S5_STATE, S5_GROUPS * S5_CH)


def _layer_params(l, g_mix_pre, g_mix_post, g_mlp_pre, g_mlp_post, w_in, w_out, rw_mu, rw_w0, rw_w2, rw_a0, rw_a2,
                  rw_g2, rw_kk, rw_ka, rw_rk, rw_ln_w, rw_ln_b, s5_a_re, s5_a_im, s5_log_dt, s5_b_re, s5_b_im,
                  s5_c_re, s5_c_im, s5_d, s5_w_glu, s5_b_glu, gdn_conv_w, gdn_a_log, gdn_dt_bias, gdn_norm_w,
                  swa_sinks, w_up, w_down):
    row = lambda a: a[l].astype(F32).reshape(1, -1)
    wi = w_in[l].astype(F32)
    o_s5 = RW_PROJ
    o_gdn = o_s5 + GROUP_WIDTH
    o_ba = o_gdn + GDN_QKV
    o_z = o_ba + 2 * N_HEADS
    o_swa = o_z + GROUP_WIDTH
    w_all = jnp.concatenate([
        wi[:, :o_s5], wi[:, o_s5:o_gdn], wi[:, o_gdn:o_ba], wi[:, o_z:o_swa],
        jnp.pad(wi[:, o_ba:o_z], ((0, 0), (0, LANES - 2 * N_HEADS))), wi[:, o_swa:]], axis=1).astype(BF16)
    lane_pad = lambda a: jnp.pad(a[l].astype(F32), (N_HEADS, LANES - 2 * N_HEADS)).reshape(1, LANES)
    return {
        "g_mix_pre": row(g_mix_pre), "g_mix_post": row(g_mix_post), "g_mlp_pre": row(g_mlp_pre),
        "g_mlp_post": row(g_mlp_post), "w_all": w_all, "w_out": w_out[l].astype(BF16),
        "w_up": w_up[l].astype(BF16), "w_down": w_down[l].astype(BF16),
        "rw": {"mu": row(rw_mu), "w0": row(rw_w0), "a0": row(rw_a0), "kk": row(rw_kk), "ka": row(rw_ka),
               "rk": row(rw_rk), "ln_w": row(rw_ln_w), "ln_b": row(rw_ln_b),
               "w2": _pad_rows(rw_w2[l].astype(F32), 0, GROUP_WIDTH).astype(BF16),
               "a2": _pad_rows(rw_a2[l].astype(F32), RW_DECAY_RANK, GROUP_WIDTH).astype(BF16),
               "g2": _pad_rows(rw_g2[l].astype(F32), RW_DECAY_RANK + RW_ICLR_RANK, GROUP_WIDTH).astype(BF16)},
        "s5": {"a_re": row(s5_a_re), "a_im": row(s5_a_im),
               "log_dt": jnp.repeat(s5_log_dt[l].astype(F32), S5_STATE).reshape(1, S5_LANES),
               "b_re": _block_diag_in(s5_b_re[l].astype(F32)).astype(BF16),
               "b_im": _block_diag_in(s5_b_im[l].astype(F32)).astype(BF16),
               "c_re": _block_diag_out(s5_c_re[l].astype(F32)).astype(BF16),
               "c_im": _block_diag_out(s5_c_im[l].astype(F32)).astype(BF16),
               "d": row(s5_d), "w_glu": s5_w_glu[l].astype(BF16), "b_glu": row(s5_b_glu)},
        "gdn": {"conv_w": gdn_conv_w[l].astype(F32), "a_log": lane_pad(gdn_a_log), "dt_bias": lane_pad(gdn_dt_bias),
                "norm_w": jnp.tile(gdn_norm_w[l].astype(F32), N_HEADS).reshape(1, GROUP_WIDTH)},
        "sinks": jnp.pad(swa_sinks[l].astype(F32), (0, LANES - N_HEADS)).reshape(1, LANES),
    }


def _layer(x, start, st, layer, prm):
    rw_s, rw_shift, s5_re, s5_im, gdn_s, gdn_conv, swa_k, swa_v = st
    bsz, seq, _ = x.shape
    tokens = bsz * seq
    tm = min(TOKEN_ROWS, tokens)
    chunk = CHUNK if seq % CHUNK == 0 else -(-seq // SUBLANES) * SUBLANES
    seq_pad = -(-seq // chunk) * chunk
    n_valid = chunk - (seq_pad - seq)
    n_chunks = math.gcd(seq_pad // chunk, MIX_CHUNKS)
    bb = {name: math.gcd(bsz, pair[seq_pad == chunk]) for name, pair in MIXER_BATCH.items()}
    assert seq_pad == seq or seq_pad == chunk
    assert seq >= GDN_CONV - 1

    x2d = x.reshape(tokens, D_MODEL)
    p_rw, p_s5, p_qkv, p_z, p_ba, p_swa = (
        a.reshape(bsz, seq, -1) for a in _in_proj(x2d, prm["g_mix_pre"], prm["w_all"], tm))
    pad = lambda a: a if seq_pad == seq else jnp.pad(a, ((0, 0), (0, seq_pad - seq), (0, 0)))

    o_rw, rw_s_new = _rwkv(pad(p_rw), rw_shift, rw_s, layer, prm["rw"], bb["rwkv"], chunk, n_chunks, n_valid)
    rw_shift_new = p_rw[:, seq - 1:, :]

    rows = 16
    steps = min(seq, 64)
    o_s5_t, s5_re_new, s5_im_new = _s5(jnp.swapaxes(p_s5, 0, 1), s5_re.reshape(bsz, S5_LANES),
                                       s5_im.reshape(bsz, S5_LANES), prm["s5"], steps, rows)
    o_s5 = jnp.swapaxes(o_s5_t, 0, 1)

    o_gdn, gdn_s_new = _gdn(pad(p_qkv), pad(p_z), pad(p_ba), gdn_conv, gdn_s, layer, prm["gdn"],
                            bb["gdn"], chunk, n_chunks, n_valid)
    gdn_conv_new = p_qkv[:, seq - (GDN_CONV - 1):, :]

    qb = WINDOW if seq % WINDOW == 0 else seq_pad
    cos, sin = _rope_tables(start, seq_pad)
    wbuf = swa_k.shape[1]
    o_swa, k_rot = _swa(pad(p_swa), cos, sin, swa_k.reshape(bsz, wbuf, SWA_KV_WIDTH),
                        swa_v.reshape(bsz, wbuf, SWA_KV_WIDTH), prm["sinks"], bb["swa"], qb, start)
    k_new = k_rot[:, :qb - (seq_pad - seq), :] if seq_pad != seq else k_rot
    v_new = p_swa[:, seq - min(seq, wbuf):, GROUP_WIDTH + SWA_KV_WIDTH:]
    kv_shape = (bsz, -1, SWA_KV_HEADS, HEAD_DIM)
    if k_new.shape[1] >= wbuf:
        swa_k_new = k_new[:, -wbuf:].reshape(kv_shape)
        swa_v_new = v_new[:, -wbuf:].reshape(kv_shape)
    else:
        swa_k_new = jnp.concatenate([swa_k, k_new.reshape(kv_shape)], axis=1)[:, -wbuf:]
        swa_v_new = jnp.concatenate([swa_v, v_new.reshape(kv_shape)], axis=1)[:, -wbuf:]

    flat = lambda a: a[:, :seq, :].reshape(tokens, GROUP_WIDTH)
    y = _post(x2d, (flat(o_rw), flat(o_s5), flat(o_gdn), flat(o_swa)), prm["w_out"], prm["g_mix_post"],
              prm["g_mlp_pre"], prm["w_up"], prm["w_down"], prm["g_mlp_post"], tm)
    new_state = (rw_s_new, rw_shift_new, s5_re_new.reshape(s5_re.shape), s5_im_new.reshape(s5_im.shape),
                 gdn_s_new, gdn_conv_new, swa_k_new, swa_v_new)
    return y.reshape(bsz, seq, D_MODEL), new_state


def kernel(x_prompt, x_sample, state_rwkv, state_rwkv_shift, state_s5_re, state_s5_im, state_gdn, state_gdn_conv,
           cache_swa_k, cache_swa_v, g_mix_pre, g_mix_post, g_mlp_pre, g_mlp_post, w_in, w_out, rw_mu, rw_w0, rw_w2,
           rw_a0, rw_a2, rw_g2, rw_kk, rw_ka, rw_rk, rw_ln_w, rw_ln_b, s5_a_re, s5_a_im, s5_log_dt, s5_b_re, s5_b_im,
           s5_c_re, s5_c_im, s5_d, s5_w_glu, s5_b_glu, gdn_conv_w, gdn_a_log, gdn_dt_bias, gdn_norm_w, swa_sinks,
           w_up, w_down):
    weights = (g_mix_pre, g_mix_post, g_mlp_pre, g_mlp_post, w_in, w_out, rw_mu, rw_w0, rw_w2, rw_a0, rw_a2, rw_g2,
               rw_kk, rw_ka, rw_rk, rw_ln_w, rw_ln_b, s5_a_re, s5_a_im, s5_log_dt, s5_b_re, s5_b_im, s5_c_re, s5_c_im,
               s5_d, s5_w_glu, s5_b_glu, gdn_conv_w, gdn_a_log, gdn_dt_bias, gdn_norm_w, swa_sinks, w_up, w_down)
    depth = w_in.shape[0]
    bp = x_prompt.shape[0]
    past_len = 16384
    xp = x_prompt.astype(F32)
    xs = x_sample.astype(F32)
    zp = lambda *shape: jnp.zeros((bp,) + shape, F32)
    zero_s = jnp.zeros((1, bp, N_HEADS, HEAD_DIM, HEAD_DIM), F32)
    prompt_init = (zero_s, zp(1, RW_PROJ), zp(S5_GROUPS, S5_STATE), zp(S5_GROUPS, S5_STATE),
                   zero_s, zp(GDN_CONV - 1, GDN_QKV),
                   zp(WINDOW, SWA_KV_HEADS, HEAD_DIM), zp(WINDOW, SWA_KV_HEADS, HEAD_DIM))
    sample_states = (state_rwkv, state_rwkv_shift, state_s5_re, state_s5_im, state_gdn, state_gdn_conv,
                     cache_swa_k, cache_swa_v)
    new_p, new_s = [], []
    for l in range(depth):
        prm = _layer_params(l, *weights)
        xp, st_p = _layer(xp, 0, prompt_init, 0, prm)
        stacked = (0, 4)
        st_l = tuple(a.astype(F32) if i in stacked else a[l].astype(F32) for i, a in enumerate(sample_states))
        xs, st_s = _layer(xs, past_len, st_l, l, prm)
        new_p.append(st_p)
        new_s.append(st_s)
    outs = [xp.astype(x_prompt.dtype), xs.astype(x_sample.dtype)]
    for i in range(len(sample_states)):
        outs.append(jnp.stack([st[i] for st in new_p], axis=0))
        outs.append(jnp.stack([st[i] for st in new_s], axis=0))
    return tuple(outs)
```

```python
import functools
import math

import jax
import jax.numpy as jnp
from jax import lax
from jax.experimental import pallas as pl
from jax.experimental.pallas import tpu as pltpu

F32 = jnp.float32
BF16 = jnp.bfloat16

D_MODEL = 1024
HEAD_DIM = 64
N_HEADS = 4
GROUP_WIDTH = N_HEADS * HEAD_DIM
NORM_EPS = 1e-6
RW_LN_EPS = 64e-5
RW_PROJ = 1024
RW_LORA_OFF = 3 * GROUP_WIDTH
RW_DECAY_RANK, RW_ICLR_RANK, RW_GATE_RANK = 64, 64, 128
S5_GROUPS, S5_CH, S5_STATE = 16, 16, 64
S5_LANES = S5_GROUPS * S5_STATE
GDN_CONV = 4
GDN_QKV = 3 * GROUP_WIDTH
SWA_KV_HEADS = 2
SWA_KV_WIDTH = SWA_KV_HEADS * HEAD_DIM
SWA_PROJ = GROUP_WIDTH + 2 * SWA_KV_WIDTH
WINDOW = 128
ROPE_DIM = 16
ROPE_THETA = 500000.0
D_FF = 4096
LANES = 128
SUBLANES = 8
TOKEN_ROWS = 1024
CHUNK = 64
MIX_CHUNKS = 4
MIXER_BATCH = {"rwkv": (4, 16), "gdn": (4, 16), "swa": (4, 32)}
TRI_BLOCK = 16
VMEM_LIMIT = 56 * 1024 * 1024

PROJ_WIDTHS = (RW_PROJ, GROUP_WIDTH, GDN_QKV, GROUP_WIDTH, LANES, SWA_PROJ)


def _dot(a, b, dims):
    return lax.dot_general(a.astype(BF16), b.astype(BF16), (dims, ((), ())), preferred_element_type=F32)


def _mm(a, b):
    return _dot(a, b, ((1,), (0,)))


def _mm_nt(a, b):
    return _dot(a, b, ((1,), (1,)))


def _mm_tn(a, b):
    return _dot(a, b, ((0,), (0,)))


def _split2(x):
    hi = x.astype(BF16)
    lo = (x - hi.astype(F32)).astype(BF16)
    return hi, lo


def _mm_exact_rhs(x, m):
    hi, lo = _split2(x)
    return _mm(hi, m) + _mm(lo, m)


def _mm_exact_lhs(m, x):
    hi, lo = _split2(x)
    return _mm(m, hi) + _mm(m, lo)


def _head_ones(width):
    r = lax.broadcasted_iota(jnp.int32, (width, width), 0)
    c = lax.broadcasted_iota(jnp.int32, (width, width), 1)
    return (r // HEAD_DIM == c // HEAD_DIM).astype(BF16)


def _head_sum(x, ones):
    return _mm(x, ones)


def _tri_masks(n):
    r = lax.broadcasted_iota(jnp.int32, (n, n), 0)
    c = lax.broadcasted_iota(jnp.int32, (n, n), 1)
    return r, c, r > c, r >= c


def _softplus(x):
    return jnp.maximum(x, 0.0) + jnp.log(1.0 + jnp.exp(-jnp.abs(x)))


def _silu(x):
    return x * jax.nn.sigmoid(x)


def _rms(x, g):
    return x * lax.rsqrt(jnp.mean(x * x, axis=-1, keepdims=True) + NORM_EPS) * g


def _per_head(y, heads):
    if heads == 1:
        return y
    yb = y.astype(BF16)
    tall = jnp.concatenate([yb] * heads, axis=0)
    r = lax.broadcasted_iota(jnp.int32, tall.shape, 0) // y.shape[0]
    c = lax.broadcasted_iota(jnp.int32, tall.shape, 1) // (y.shape[1] // heads)
    return jnp.where(r == c, tall, jnp.zeros((), BF16))


def _each(fn, *lists):
    return [fn(*xs) for xs in zip(*lists)]


def _tri_inv(lows, n, paired=False):
    pair = paired or (2 * n == LANES and len(lows) % 2 == 0)
    if pair and not paired:
        lows = [jnp.concatenate(lows[j:j + 2], axis=1) for j in range(0, len(lows), 2)]
    width = 2 * n if pair else n
    r = lax.broadcasted_iota(jnp.int32, (n, width), 0)
    c = lax.broadcasted_iota(jnp.int32, (n, width), 1) % n
    eye = (r == c).astype(F32)
    mul = lambda x, y: _mm(x, _per_head(y, 2 if pair else 1))
    bs = min(TRI_BLOCK, n)
    if n > bs:
        same = (r // bs) == (c // bs)
        diags = [jnp.where(same, low, 0.0) for low in lows]
        rests = _each(lambda low, d: low - d, lows, diags)
    else:
        diags, rests = lows, None
    invs = [eye - d for d in diags]
    pws, p = _each(lambda x: mul(x, x), diags), 2
    while 2 * p < bs:
        both = _each(lambda x, i: mul(jnp.concatenate([x, i], axis=0), x), pws, invs)
        pws = [z[:n] for z in both]
        invs = _each(lambda i, z: i + z[n:], invs, both)
        p *= 2
    invs = _each(lambda i, x: i + mul(i, x), invs, pws)
    if rests is not None:
        nils = _each(mul, invs, rests)
        outs = [eye - x for x in nils]
        for _ in range(n // bs - 2):
            outs = _each(lambda x, o: eye - mul(x, o), nils, outs)
        invs = _each(mul, outs, invs)
    if pair and not paired:
        invs = [half for x in invs for half in (x[:, :n], x[:, n:])]
    return invs


def _const_spec(shape):
    zeros = (0,) * len(shape)
    return pl.BlockSpec(shape, lambda *_: zeros, pipeline_mode=pl.Buffered(1))


def _in_proj_kernel(x_ref, g_ref, w_ref, *out_refs):
    h = _rms(x_ref[...], g_ref[...]).astype(BF16)
    off = 0
    for o_ref, width in zip(out_refs, PROJ_WIDTHS):
        o_ref[...] = _mm(h, w_ref[:, off:off + width])
        off += width


def _in_proj(x2d, g, w_all, tm):
    t = x2d.shape[0]
    wtot = sum(PROJ_WIDTHS)
    return pl.pallas_call(
        _in_proj_kernel,
        grid=(t // tm,),
        in_specs=[pl.BlockSpec((tm, D_MODEL), lambda i: (i, 0)),
                  _const_spec((1, D_MODEL)),
                  _const_spec((D_MODEL, wtot))],
        out_specs=[pl.BlockSpec((tm, w), lambda i: (i, 0)) for w in PROJ_WIDTHS],
        out_shape=[jax.ShapeDtypeStruct((t, w), F32) for w in PROJ_WIDTHS],
        compiler_params=pltpu.CompilerParams(dimension_semantics=("arbitrary",), vmem_limit_bytes=VMEM_LIMIT),
        name="in_proj",
    )(x2d, g, w_all)


FF_CHUNK = 1024


def _post_kernel(x_ref, o0_ref, o1_ref, o2_ref, o3_ref, wout_ref, gpost_ref, gpre_ref, wup_ref, wdn_ref,
                 gmlp_ref, y_ref):
    mix = None
    for i, o_ref in enumerate((o0_ref, o1_ref, o2_ref, o3_ref)):
        part = _mm(o_ref[...].astype(BF16), wout_ref[i * GROUP_WIDTH:(i + 1) * GROUP_WIDTH, :])
        mix = part if mix is None else mix + part
    x1 = x_ref[...] + _rms(mix, gpost_ref[...])
    h = _rms(x1, gpre_ref[...]).astype(BF16)
    acc = None
    for c in range(D_FF // FF_CHUNK):
        u = _mm(h, wup_ref[:, c * FF_CHUNK:(c + 1) * FF_CHUNK])
        u = jnp.square(jnp.maximum(u, 0.0)).astype(BF16)
        part = _mm(u, wdn_ref[c * FF_CHUNK:(c + 1) * FF_CHUNK, :])
        acc = part if acc is None else acc + part
    y_ref[...] = x1 + _rms(acc, gmlp_ref[...])


def _post(x2d, outs, w_out, g_post, g_pre, w_up, w_dn, g_mlp, tm):
    t = x2d.shape[0]
    row = lambda w: pl.BlockSpec((tm, w), lambda i: (i, 0))
    return pl.pallas_call(
        _post_kernel,
        grid=(t // tm,),
        in_specs=[row(D_MODEL)] + [row(GROUP_WIDTH)] * 4 + [
            _const_spec((D_MODEL, D_MODEL)), _const_spec((1, D_MODEL)), _const_spec((1, D_MODEL)),
            _const_spec((D_MODEL, D_FF)), _const_spec((D_FF, D_MODEL)), _const_spec((1, D_MODEL))],
        out_specs=row(D_MODEL),
        out_shape=jax.ShapeDtypeStruct((t, D_MODEL), F32),
        compiler_params=pltpu.CompilerParams(dimension_semantics=("arbitrary",), vmem_limit_bytes=VMEM_LIMIT),
        name="post_mlp",
    )(x2d, *outs, w_out, g_post, g_pre, w_up, w_dn, g_mlp)


def _chunk_sums(x, chunk, rows):
    _, _, _, incl = _tri_masks(chunk)
    tri = incl.astype(BF16)
    cums = [_mm_exact_lhs(tri, x[c * chunk:(c + 1) * chunk]) for c in range(rows // chunk)]
    if len(cums) == 1:
        return cums[0], cums[0][chunk - 1:chunk, :]
    tots = [jnp.broadcast_to(cu[chunk - 1:chunk, :], cu.shape) for cu in cums]
    return jnp.concatenate(cums, axis=0), jnp.concatenate(tots, axis=0)


def _rwkv_kernel(p_ref, shift_ref, s0_ref, mu_ref, w0_ref, a0_ref, kkp_ref, kap_ref, rk_ref, lnw_ref, lnb_ref,
                 w2_ref, a2_ref, g2_ref, o_ref, s_out_ref, s_scr, carry_scr, *, bb, chunk, n_chunks, n_valid):
    c_idx = pl.program_id(1)
    rows = chunk * n_chunks
    total = bb * rows

    @pl.when(c_idx == 0)
    def _():
        s_scr[...] = s0_ref[...]
        carry_scr[...] = shift_ref[...]

    first = lax.broadcasted_iota(jnp.int32, (rows, 1), 0) == 0
    last = rows - chunk + n_valid
    prevs = []
    for bi in range(bb):
        p_b = p_ref[bi]
        prevs.append(jnp.where(first, carry_scr[bi], pltpu.roll(p_b, 1, 0)))
        carry_scr[bi] = p_b[last - 1:last, :]
    p = p_ref[...].reshape(total, RW_PROJ)
    prev = prevs[0] if bb == 1 else jnp.concatenate(prevs, axis=0)
    row = lax.broadcasted_iota(jnp.int32, (total, 1), 0)
    xs = p + (prev - p) * mu_ref[...]
    r = xs[:, 0:GROUP_WIDTH]
    k = xs[:, GROUP_WIDTH:2 * GROUP_WIDTH]
    v = xs[:, 2 * GROUP_WIDTH:3 * GROUP_WIDTH]
    lora = xs[:, RW_LORA_OFF:RW_PROJ]

    z_w = w0_ref[...] + _mm(jnp.tanh(lora), w2_ref[...])
    logw = -jnp.exp(-_softplus(-z_w) - 0.5)
    a = jax.nn.sigmoid(a0_ref[...] + _mm(lora, a2_ref[...]))
    gate = _mm(jax.nn.sigmoid(lora), g2_ref[...])

    ones = _head_ones(GROUP_WIDTH)
    kk = k * kkp_ref[...]
    kk = kk * lax.rsqrt(jnp.maximum(_head_sum(kk * kk, ones), 1e-12))
    k2 = k * (1.0 + (a - 1.0) * kap_ref[...])
    b = kk * a
    if n_valid < chunk:
        valid = (row % chunk) < n_valid
        logw = jnp.where(valid, logw, 0.0)
        kk = jnp.where(valid, kk, 0.0)
        b = jnp.where(valid, b, 0.0)
        k2v = jnp.where(valid, k2, 0.0)
        vv = jnp.where(valid, v, 0.0)
    else:
        k2v, vv = k2, v

    log_cum, log_end = _chunk_sums(logw, chunk, total)
    w_inv = jnp.exp(-log_cum)
    w_tail = jnp.exp(log_end - log_cum)
    a_bar = kk * jnp.exp(log_cum - logw)
    b_bar = b * w_inv
    k_bar = k2v * w_inv
    r_bar = r * jnp.exp(log_cum)
    b_end = b * w_tail
    k_end = k2v * w_tail
    w_end = jnp.exp(log_end)

    _, _, strict, incl = _tri_masks(chunk)
    group = bb * N_HEADS
    tiles = [(slice(bi * rows + c * chunk, bi * rows + (c + 1) * chunk), slice(h * HEAD_DIM, (h + 1) * HEAD_DIM))
             for c in range(n_chunks) for bi in range(bb) for h in range(N_HEADS)]
    cut = lambda x: [x[rs, sl] for rs, sl in tiles]
    ab, rb, bs, kb, be, ke, vh = map(cut, (a_bar, r_bar, b_bar, k_bar, b_end, k_end, vv))
    gram = _each(lambda a_, r_, b_, k_: _mm_nt(jnp.concatenate([a_, r_], axis=0), jnp.concatenate([b_, k_], axis=0)),
                 ab, rb, bs, kb)
    l_ab = [jnp.where(strict, g[:chunk, :chunk], 0.0) for g in gram]
    l_ak = [jnp.where(strict, g[:chunk, chunk:], 0.0) for g in gram]
    m_rb = [jnp.where(incl, g[chunk:, :chunk], 0.0) for g in gram]
    m_rk = [jnp.where(incl, g[chunk:, chunk:], 0.0) for g in gram]
    tinv = _tri_inv(l_ab, chunk)
    lv = _each(_mm, l_ak, vh)
    tt = _each(lambda t_, a_, l_: _mm(t_, jnp.concatenate([a_, l_], axis=1)), tinv, ab, lv)
    ta_c0 = [jnp.concatenate([t_[:, :HEAD_DIM], -t_[:, HEAD_DIM:]], axis=1) for t_ in tt]
    pq = _each(_mm_tn, ta_c0, be)
    vk = _each(_mm_tn, vh, ke)
    ry = _each(_mm, m_rb, ta_c0)
    rkv = _each(_mm, m_rk, vh)
    pb = [x[:HEAD_DIM] for x in pq]
    q_mat = _each(lambda x, y_: x[HEAD_DIM:] + y_, pq, vk)
    rr = _each(lambda r_, x: r_ - x[:, :HEAD_DIM], rb, ry)
    y0 = _each(lambda x, y_: x[:, HEAD_DIM:] + y_, ry, rkv)

    state = [s_scr[bi, h] for bi in range(bb) for h in range(N_HEADS)]
    y_tiles = {}
    for c in range(n_chunks):
        grp = slice(c * group, (c + 1) * group)
        ys = _each(lambda y_, r_, s_: y_ + _mm_nt(r_, s_), y0[grp], rr[grp], state)
        we = [w_end[rs.start:rs.start + 1, sl] for rs, sl in tiles[grp]]
        state = _each(lambda s_, w_, p_, q_: s_ * w_ - _mm(s_, p_) + q_, state, we, pb[grp], q_mat[grp])
        for bi in range(bb):
            y_tiles[bi, c] = jnp.concatenate(ys[bi * N_HEADS:(bi + 1) * N_HEADS], axis=1)
    for bi in range(bb):
        for h in range(N_HEADS):
            s_scr[bi, h] = state[bi * N_HEADS + h]
    y_rows = [y_tiles[bi, c] for bi in range(bb) for c in range(n_chunks)]
    y = y_rows[0] if len(y_rows) == 1 else jnp.concatenate(y_rows, axis=0)

    inv_d = 1.0 / HEAD_DIM
    mean = _head_sum(y, ones) * inv_d
    d = y - mean
    var = _head_sum(d * d, ones) * inv_d
    yn = d * lax.rsqrt(var + RW_LN_EPS) * lnw_ref[...] + lnb_ref[...]
    bonus = _head_sum(r * k2 * rk_ref[...], ones) * v
    o_ref[...] = ((yn + bonus) * gate).reshape(bb, rows, GROUP_WIDTH)

    @pl.when(c_idx == pl.num_programs(1) - 1)
    def _():
        s_out_ref[...] = s_scr[...]


def _state_specs(bb, layer):
    shape = (bb, N_HEADS, HEAD_DIM, HEAD_DIM)
    return (pl.BlockSpec((None,) + shape, lambda b, c: (layer, b, 0, 0, 0)),
            pl.BlockSpec(shape, lambda b, c: (b, 0, 0, 0)))


def _rwkv(p, shift, s0, layer, prm, bb, chunk, n_chunks, n_valid):
    bsz, seq, _ = p.shape
    rows = chunk * n_chunks
    assert n_valid == chunk or n_chunks == 1
    vec = _const_spec((1, GROUP_WIDTH))
    mat = _const_spec((GROUP_WIDTH, GROUP_WIDTH))
    state_in, state = _state_specs(bb, layer)
    return pl.pallas_call(
        functools.partial(_rwkv_kernel, bb=bb, chunk=chunk, n_chunks=n_chunks, n_valid=n_valid),
        grid=(bsz // bb, seq // rows),
        in_specs=[pl.BlockSpec((bb, rows, RW_PROJ), lambda b, c: (b, c, 0)),
                  pl.BlockSpec((bb, 1, RW_PROJ), lambda b, c: (b, 0, 0)),
                  state_in, _const_spec((1, RW_PROJ))] + [vec] * 7 + [mat] * 3,
        out_specs=[pl.BlockSpec((bb, rows, GROUP_WIDTH), lambda b, c: (b, c, 0)), state],
        out_shape=[jax.ShapeDtypeStruct((bsz, seq, GROUP_WIDTH), F32),
                   jax.ShapeDtypeStruct(s0.shape[1:], F32)],
        scratch_shapes=[pltpu.VMEM((bb, N_HEADS, HEAD_DIM, HEAD_DIM), F32), pltpu.VMEM((bb, 1, RW_PROJ), F32)],
        compiler_params=pltpu.CompilerParams(dimension_semantics=("arbitrary", "arbitrary")),
        name="rwkv7",
    )(p, shift, s0, prm["mu"], prm["w0"], prm["a0"], prm["kk"], prm["ka"], prm["rk"], prm["ln_w"], prm["ln_b"],
      prm["w2"], prm["a2"], prm["g2"])


def _gdn_kernel(qkv_ref, z_ref, ba_ref, buf_ref, s0_ref, convw_ref, alog_ref, dtb_ref, nw_ref,
                o_ref, s_out_ref, s_scr, x_scr, *, bb, chunk, n_chunks, n_valid):
    c_idx = pl.program_id(1)
    rows = chunk * n_chunks
    total = bb * rows
    pad = SUBLANES

    @pl.when(c_idx == 0)
    def _():
        s_scr[...] = s0_ref[...]
        x_scr[:, 0:pad, :] = jnp.zeros((bb, pad, GDN_QKV), F32)
        x_scr[:, pad - (GDN_CONV - 1):pad, :] = buf_ref[...]

    last = rows - chunk + n_valid
    convs = []
    for bi in range(bb):
        x_scr[bi, pad:pad + rows, :] = qkv_ref[bi]
        conv = None
        for j in range(GDN_CONV):
            term = x_scr[bi, pad - j:pad - j + rows, :] * convw_ref[GDN_CONV - 1 - j:GDN_CONV - j, :]
            conv = term if conv is None else conv + term
        x_scr[bi, 0:pad, :] = x_scr[bi, last:last + pad, :]
        convs.append(conv)
    act = _silu(convs[0] if bb == 1 else jnp.concatenate(convs, axis=0))
    q = act[:, 0:GROUP_WIDTH]
    k = act[:, GROUP_WIDTH:2 * GROUP_WIDTH]
    v = act[:, 2 * GROUP_WIDTH:3 * GROUP_WIDTH]
    ones = _head_ones(GROUP_WIDTH)
    q = q * lax.rsqrt(jnp.maximum(_head_sum(q * q, ones), 1e-12)) * (HEAD_DIM ** -0.5)
    k = k * lax.rsqrt(jnp.maximum(_head_sum(k * k, ones), 1e-12))

    ba = ba_ref[...].reshape(total, LANES)
    row = lax.broadcasted_iota(jnp.int32, (total, LANES), 0)
    beta = jax.nn.sigmoid(ba)
    glog = -jnp.exp(alog_ref[...]) * _softplus(ba + dtb_ref[...])
    if n_valid < chunk:
        valid = (row % chunk) < n_valid
        beta = jnp.where(valid, beta, 0.0)
        glog = jnp.where(valid, glog, 0.0)
    gcum, gend = _chunk_sums(glog, chunk, total)
    gcum_t = gcum.T
    er = lax.broadcasted_iota(jnp.int32, (LANES, GROUP_WIDTH), 0)
    ec = lax.broadcasted_iota(jnp.int32, (LANES, GROUP_WIDTH), 1) // HEAD_DIM
    beta_w = _mm_exact_rhs(beta, (er == ec).astype(BF16))
    spread = (er == ec + N_HEADS).astype(BF16)
    gcum_w = _mm_exact_rhs(gcum, spread)
    gend_w = _mm_exact_rhs(gend, spread)
    gam_w = jnp.exp(gcum_w)
    bgk = beta_w * gam_w * k
    bv = beta_w * v
    gq = gam_w * q
    kt = k * jnp.exp(gend_w - gcum_w)
    end_w = jnp.exp(gend_w)

    hp = 2 if chunk == HEAD_DIM else 1
    wid, swid, n_grp = hp * HEAD_DIM, hp * chunk, N_HEADS // hp
    sr = lax.broadcasted_iota(jnp.int32, (chunk, swid), 0)
    sc = lax.broadcasted_iota(jnp.int32, (chunk, swid), 1) % chunk
    strict, incl = sr > sc, sr >= sc
    group = bb * n_grp
    tiles = [(slice(bi * rows + c * chunk, bi * rows + (c + 1) * chunk), g)
             for c in range(n_chunks) for bi in range(bb) for g in range(n_grp)]
    lanes = lambda x: [x[rs, g * wid:(g + 1) * wid] for rs, g in tiles]
    slab = lambda x: [jnp.concatenate([x[rs, h * HEAD_DIM:h * HEAD_DIM + chunk] for h in range(g * hp, (g + 1) * hp)],
                                      axis=1) if hp * chunk != wid else x[rs, g * wid:(g + 1) * wid]
                      for rs, g in tiles]
    k_g, q_g, bgk_g, bv_g, gq_g, kt_g = map(lanes, (k, q, bgk, bv, gq, kt))
    g_row = [jnp.concatenate([gcum_t[N_HEADS + h:N_HEADS + h + 1, rs] for h in range(g * hp, (g + 1) * hp)], axis=1)
             for rs, g in tiles]
    decay = _each(lambda gc, gr: jnp.where(incl, jnp.exp(jnp.where(incl, gc - gr, 0.0)), 0.0), slab(gcum_w), g_row)
    gram = _each(lambda k_, q_: _mm_nt(jnp.concatenate([k_, q_], axis=0), _per_head(k_, hp)), k_g, q_g)
    a_mat = _each(lambda b_, d_, g_: jnp.where(strict, b_ * d_ * g_[:chunk], 0.0), slab(beta_w), decay, gram)
    tinv = _tri_inv(a_mat, chunk, paired=True) if hp == 2 else _tri_inv(a_mat, chunk)
    side = lambda x, y_: jnp.concatenate([_per_head(x, hp), _per_head(y_, hp)], axis=1)
    tt = _each(lambda t_, x, y_: _mm(t_, side(x, y_)), tinv, bgk_g, bv_g)
    qo = _each(lambda g_, d_, t_: _mm(g_[chunk:] * d_, side(t_[:, :wid], t_[:, wid:])), gram, decay, tt)
    pq = _each(_mm_tn, tt, kt_g)
    qq = _each(lambda g_, x: g_ - x[:, :wid], gq_g, qo)
    o0 = [x[:, wid:] for x in qo]
    if hp == 1:
        pb = [x[:wid] for x in pq]
        q_mat = [x[wid:] for x in pq]
    else:
        pr = lax.broadcasted_iota(jnp.int32, (wid, wid), 0) // HEAD_DIM
        pc = lax.broadcasted_iota(jnp.int32, (wid, wid), 1) // HEAD_DIM
        low = lax.broadcasted_iota(jnp.int32, (HEAD_DIM, wid), 1) < HEAD_DIM
        pb = [jnp.where(pr == pc, x[:wid], 0.0) for x in pq]
        q_mat = [jnp.where(low, x[wid:wid + HEAD_DIM], x[wid + HEAD_DIM:]) for x in pq]

    state = [s_scr[bi, g] if hp == 1 else jnp.concatenate([s_scr[bi, g * hp + i] for i in range(hp)], axis=1)
             for bi in range(bb) for g in range(n_grp)]
    o_tiles = {}
    for c in range(n_chunks):
        grp = slice(c * group, (c + 1) * group)
        os_ = _each(lambda o_, q_, s_: o_ + _mm_nt(q_, _per_head(s_, hp)), o0[grp], qq[grp], state)
        ge = [end_w[rs.start:rs.start + 1, g * wid:(g + 1) * wid] for rs, g in tiles[grp]]
        state = _each(lambda s_, g_, p_, q_: s_ * g_ - _mm(s_, p_) + q_, state, ge, pb[grp], q_mat[grp])
        for bi in range(bb):
            o_tiles[bi, c] = jnp.concatenate(os_[bi * n_grp:(bi + 1) * n_grp], axis=1)
    for bi in range(bb):
        for g in range(n_grp):
            for i in range(hp):
                s_scr[bi, g * hp + i] = state[bi * n_grp + g][:, i * HEAD_DIM:(i + 1) * HEAD_DIM]
    o_rows = [o_tiles[bi, c] for bi in range(bb) for c in range(n_chunks)]
    o = o_rows[0] if len(o_rows) == 1 else jnp.concatenate(o_rows, axis=0)
    ms = _head_sum(o * o, ones) * (1.0 / HEAD_DIM)
    zz = z_ref[...].reshape(total, GROUP_WIDTH)
    o_ref[...] = (o * lax.rsqrt(ms + NORM_EPS) * nw_ref[...] * _silu(zz)).reshape(bb, rows, GROUP_WIDTH)

    @pl.when(c_idx == pl.num_programs(1) - 1)
    def _():
        s_out_ref[...] = s_scr[...]


def _gdn(qkv, z, ba, buf, s0, layer, prm, bb, chunk, n_chunks, n_valid):
    bsz, seq, _ = qkv.shape
    rows = chunk * n_chunks
    assert n_valid == chunk or n_chunks == 1
    state_in, state = _state_specs(bb, layer)
    blk = lambda w: pl.BlockSpec((bb, rows, w), lambda b, c: (b, c, 0))
    return pl.pallas_call(
        functools.partial(_gdn_kernel, bb=bb, chunk=chunk, n_chunks=n_chunks, n_valid=n_valid),
        grid=(bsz // bb, seq // rows),
        in_specs=[blk(GDN_QKV), blk(GROUP_WIDTH), blk(LANES),
                  pl.BlockSpec((bb, GDN_CONV - 1, GDN_QKV), lambda b, c: (b, 0, 0)), state_in,
                  _const_spec((GDN_CONV, GDN_QKV)), _const_spec((1, LANES)), _const_spec((1, LANES)),
                  _const_spec((1, GROUP_WIDTH))],
        out_specs=[blk(GROUP_WIDTH), state],
        out_shape=[jax.ShapeDtypeStruct((bsz, seq, GROUP_WIDTH), F32), jax.ShapeDtypeStruct(s0.shape[1:], F32)],
        scratch_shapes=[pltpu.VMEM((bb, N_HEADS, HEAD_DIM, HEAD_DIM), F32),
                        pltpu.VMEM((bb, SUBLANES + rows, GDN_QKV), F32)],
        compiler_params=pltpu.CompilerParams(dimension_semantics=("arbitrary", "arbitrary")),
        name="gdn",
    )(qkv, z, ba, buf, s0, prm["conv_w"], prm["a_log"], prm["dt_bias"], prm["norm_w"])


def _gelu_tanh(x):
    return 0.5 * x * (1.0 + jnp.tanh(math.sqrt(2.0 / math.pi) * (x + 0.044715 * (x * x * x))))


def _s5_kernel(u_ref, x0re_ref, x0im_ref, are_ref, aim_ref, ldt_ref, bre_ref, bim_ref, cre_ref, cim_ref, d_ref,
               wglu_ref, bglu_ref, o_ref, hre_out_ref, him_out_ref, hre_scr, him_scr, sre_scr, sim_scr, *, steps, rows):
    c_idx = pl.program_id(1)

    @pl.when(c_idx == 0)
    def _():
        sre_scr[...] = x0re_ref[...]
        sim_scr[...] = x0im_ref[...]

    a_re, a_im = are_ref[...], aim_ref[...]
    dt = jnp.exp(ldt_ref[...])
    mag = jnp.exp(dt * a_re)
    ab_re = mag * jnp.cos(dt * a_im)
    ab_im = mag * jnp.sin(dt * a_im)
    den = a_re * a_re + a_im * a_im
    nr = ab_re - 1.0
    cf_re = (nr * a_re + ab_im * a_im) / den
    cf_im = (ab_im * a_re - nr * a_im) / den

    u = u_ref[...].reshape(steps * rows, GROUP_WIDTH)
    ub = u.astype(BF16)
    bu_re = _mm(ub, bre_ref[...])
    bu_im = _mm(ub, bim_ref[...])
    hre_scr[...] = cf_re * bu_re - cf_im * bu_im
    him_scr[...] = cf_re * bu_im + cf_im * bu_re

    abr = jnp.broadcast_to(ab_re, (rows, S5_LANES))
    abi = jnp.broadcast_to(ab_im, (rows, S5_LANES))

    def step(t, carry):
        h_re, h_im = carry
        base = pl.multiple_of(t * rows, rows)
        n_re = abr * h_re - abi * h_im + hre_scr[pl.ds(base, rows), :]
        n_im = abr * h_im + abi * h_re + him_scr[pl.ds(base, rows), :]
        hre_scr[pl.ds(base, rows), :] = n_re
        him_scr[pl.ds(base, rows), :] = n_im
        return n_re, n_im

    h_re, h_im = lax.fori_loop(0, steps, step, (sre_scr[...], sim_scr[...]))
    sre_scr[...] = h_re
    sim_scr[...] = h_im

    y = (_mm(hre_scr[...].astype(BF16), cre_ref[...]) - _mm(him_scr[...].astype(BF16), cim_ref[...])
         + d_ref[...] * u)
    zz = _gelu_tanh(y)
    out = zz * jax.nn.sigmoid(_mm(zz.astype(BF16), wglu_ref[...]) + bglu_ref[...])
    o_ref[...] = out.reshape(steps, rows, GROUP_WIDTH)

    @pl.when(c_idx == pl.num_programs(1) - 1)
    def _():
        hre_out_ref[...] = h_re
        him_out_ref[...] = h_im


def _s5(u_t, x0_re, x0_im, prm, steps, rows):
    seq, bsz, _ = u_t.shape
    st = pl.BlockSpec((rows, S5_LANES), lambda b, c: (b, 0))
    vec = _const_spec((1, S5_LANES))
    return pl.pallas_call(
        functools.partial(_s5_kernel, steps=steps, rows=rows),
        grid=(bsz // rows, seq // steps),
        in_specs=[pl.BlockSpec((steps, rows, GROUP_WIDTH), lambda b, c: (c, b, 0)), st, st, vec, vec, vec,
                  _const_spec((GROUP_WIDTH, S5_LANES)), _const_spec((GROUP_WIDTH, S5_LANES)),
                  _const_spec((S5_LANES, GROUP_WIDTH)), _const_spec((S5_LANES, GROUP_WIDTH)),
                  _const_spec((1, GROUP_WIDTH)), _const_spec((GROUP_WIDTH, GROUP_WIDTH)),
                  _const_spec((1, GROUP_WIDTH))],
        out_specs=[pl.BlockSpec((steps, rows, GROUP_WIDTH), lambda b, c: (c, b, 0)), st, st],
        out_shape=[jax.ShapeDtypeStruct((seq, bsz, GROUP_WIDTH), F32),
                   jax.ShapeDtypeStruct((bsz, S5_LANES), F32), jax.ShapeDtypeStruct((bsz, S5_LANES), F32)],
        scratch_shapes=[pltpu.VMEM((steps * rows, S5_LANES), F32), pltpu.VMEM((steps * rows, S5_LANES), F32),
                        pltpu.VMEM((rows, S5_LANES), F32), pltpu.VMEM((rows, S5_LANES), F32)],
        compiler_params=pltpu.CompilerParams(dimension_semantics=("arbitrary", "arbitrary"),
                                             vmem_limit_bytes=VMEM_LIMIT),
        name="s5",
    )(u_t, x0_re, x0_im, prm["a_re"], prm["a_im"], prm["log_dt"], prm["b_re"], prm["b_im"], prm["c_re"],
      prm["c_im"], prm["d"], prm["w_glu"], prm["b_glu"])


def _rope(x, cos, sin, width):
    fwd = pltpu.roll(x, width - ROPE_DIM // 2, 1)
    bwd = pltpu.roll(x, ROPE_DIM // 2, 1)
    lane = lax.broadcasted_iota(jnp.int32, x.shape, 1) % HEAD_DIM
    return x * cos + jnp.where(lane < ROPE_DIM // 2, fwd, bwd) * sin


def _swa_kernel(p_ref, cos_ref, sin_ref, kbuf_ref, vbuf_ref, sink_ref, o_ref, krot_ref, k_scr, v_scr,
                *, bb, qb, start, carry):
    j = pl.program_id(1)
    wb = WINDOW

    @pl.when(j == 0)
    def _():
        k_scr[:, 0:wb, :] = kbuf_ref[...]
        v_scr[:, 0:wb, :] = vbuf_ref[...]
        if qb < wb:
            k_scr[:, wb:2 * wb, :] = jnp.zeros((bb, wb, SWA_KV_WIDTH), F32)
            v_scr[:, wb:2 * wb, :] = jnp.zeros((bb, wb, SWA_KV_WIDTH), F32)

    p = p_ref[...].reshape(bb * qb, SWA_PROJ)
    cos, sin = cos_ref[...], sin_ref[...]
    if bb > 1:
        cos, sin = jnp.concatenate([cos] * bb, axis=0), jnp.concatenate([sin] * bb, axis=0)
    q = _rope(p[:, 0:GROUP_WIDTH], cos, sin, GROUP_WIDTH)
    k = _rope(p[:, GROUP_WIDTH:GROUP_WIDTH + SWA_KV_WIDTH], cos[:, 0:SWA_KV_WIDTH], sin[:, 0:SWA_KV_WIDTH],
              SWA_KV_WIDTH)
    v = p[:, GROUP_WIDTH + SWA_KV_WIDTH:SWA_PROJ]
    krot_ref[...] = k.reshape(bb, qb, SWA_KV_WIDTH)
    for bi in range(bb):
        k_scr[bi, wb:wb + qb, :] = k[bi * qb:(bi + 1) * qb]
        v_scr[bi, wb:wb + qb, :] = v[bi * qb:(bi + 1) * qb]

    nk = 2 * wb
    rq = lax.broadcasted_iota(jnp.int32, (2 * qb, nk), 0) % qb
    ck = lax.broadcasted_iota(jnp.int32, (2 * qb, nk), 1)
    rel = rq + wb - ck
    kpos = start + j * qb - wb + ck
    valid = (rel >= 0) & (rel < WINDOW) & (kpos >= 0)
    sink_col = ck == 0
    upper = lax.broadcasted_iota(jnp.int32, (2 * qb, nk), 0) < qb
    sinks = sink_ref[...]
    v_lane = lax.broadcasted_iota(jnp.int32, (nk, SWA_KV_WIDTH), 1)
    v_live = lax.broadcasted_iota(jnp.int32, (nk, SWA_KV_WIDTH), 0) > 0
    low, high = v_live & (v_lane < HEAD_DIM), v_live & (v_lane >= HEAD_DIM)
    ones = jnp.ones((nk, SWA_KV_WIDTH), BF16)

    probs = [(bi, h) for bi in range(bb) for h in range(SWA_KV_HEADS)]
    q2 = [jnp.concatenate([q[bi * qb:(bi + 1) * qb, (2 * h) * HEAD_DIM:(2 * h + 1) * HEAD_DIM],
                           q[bi * qb:(bi + 1) * qb, (2 * h + 1) * HEAD_DIM:(2 * h + 2) * HEAD_DIM]], axis=0)
          for bi, h in probs]
    kh = [k_scr[bi, :, h * HEAD_DIM:(h + 1) * HEAD_DIM] for bi, h in probs]
    vals = [v_scr[bi] for bi in range(bb)]
    swapped = [pltpu.roll(x, HEAD_DIM, 1) for x in vals]
    v_lo = [jnp.where(low, vals[bi] if h == 0 else swapped[bi], 0.0) for bi, h in probs]
    v_hi = [jnp.where(high, swapped[bi] if h == 0 else vals[bi], 0.0) for bi, h in probs]
    sink = [jnp.where(upper, sinks[:, 2 * h:2 * h + 1], sinks[:, 2 * h + 1:2 * h + 2]) for _, h in probs]
    sc = _each(lambda q_, k_, z_: jnp.where(sink_col, z_, jnp.where(valid, _mm_nt(q_, k_) * (HEAD_DIM ** -0.5),
                                                                     -jnp.inf)), q2, kh, sink)
    ex = _each(lambda s_: jnp.exp(s_ - jnp.max(s_, axis=-1, keepdims=True)).astype(BF16), sc)
    den = _each(lambda e_: _mm(e_, ones), ex)
    o_lo = _each(lambda e_, v_: _mm(e_[:qb], v_), ex, v_lo)
    o_hi = _each(lambda e_, v_: _mm(e_[qb:], v_), ex, v_hi)
    res = _each(lambda a_, b_, d_: a_ / d_[:qb] + b_ / d_[qb:], o_lo, o_hi, den)
    rows_out = [jnp.concatenate(res[bi * SWA_KV_HEADS:(bi + 1) * SWA_KV_HEADS], axis=1) for bi in range(bb)]
    o_ref[...] = (rows_out[0] if bb == 1 else jnp.concatenate(rows_out, axis=0)).reshape(bb, qb, GROUP_WIDTH)

    if carry:
        for bi in range(bb):
            k_scr[bi, 0:wb, :] = k[bi * qb:(bi + 1) * qb]
            v_scr[bi, 0:wb, :] = v[bi * qb:(bi + 1) * qb]


def _swa(p, cos, sin, kbuf, vbuf, sinks, bb, qb, start):
    bsz, seq, _ = p.shape
    nb = seq // qb
    assert nb == 1 or qb == WINDOW
    blk = lambda w: pl.BlockSpec((bb, qb, w), lambda b, j: (b, j, 0))
    tab = pl.BlockSpec((qb, GROUP_WIDTH), lambda b, j: (j, 0))
    buf = pl.BlockSpec((bb, WINDOW, SWA_KV_WIDTH), lambda b, j: (b, 0, 0))
    return pl.pallas_call(
        functools.partial(_swa_kernel, bb=bb, qb=qb, start=start, carry=nb > 1),
        grid=(bsz // bb, nb),
        in_specs=[blk(SWA_PROJ), tab, tab, buf, buf, _const_spec((1, LANES))],
        out_specs=[blk(GROUP_WIDTH), pl.BlockSpec((bb, qb, SWA_KV_WIDTH), lambda b, j: (b, 0, 0))],
        out_shape=[jax.ShapeDtypeStruct((bsz, seq, GROUP_WIDTH), F32),
                   jax.ShapeDtypeStruct((bsz, qb, SWA_KV_WIDTH), F32)],
        scratch_shapes=[pltpu.VMEM((bb, 2 * WINDOW, SWA_KV_WIDTH), F32),
                        pltpu.VMEM((bb, 2 * WINDOW, SWA_KV_WIDTH), F32)],
        compiler_params=pltpu.CompilerParams(dimension_semantics=("arbitrary", "arbitrary")),
        name="swa",
    )(p, cos, sin, kbuf, vbuf, sinks)


def _rope_tables(start, seq):
    half = ROPE_DIM // 2
    inv = ROPE_THETA ** (-jnp.arange(0, ROPE_DIM, 2, dtype=F32) / ROPE_DIM)
    ang = (start + jnp.arange(seq)).astype(F32)[:, None] * inv[None, :]
    cos, sin = jnp.cos(ang), jnp.sin(ang)
    rest = HEAD_DIM - ROPE_DIM
    cos_h = jnp.concatenate([cos, cos, jnp.ones((seq, rest), F32)], axis=1)
    sin_h = jnp.concatenate([-sin, sin, jnp.zeros((seq, rest), F32)], axis=1)
    return jnp.tile(cos_h, (1, N_HEADS)), jnp.tile(sin_h, (1, N_HEADS))


def _pad_rows(w, top, total):
    return jnp.pad(w, ((top, total - top - w.shape[0]), (0, 0)))


def _block_diag_in(b):
    eye = jnp.eye(S5_GROUPS, dtype=b.dtype)
    return jnp.einsum("gnc,gh->gchn", b, eye).reshape(S5_GROUPS * S5_CH, S5_GROUPS * S5_STATE)


def _block_diag_out(c):
    eye = jnp.eye(S5_GROUPS, dtype=c.dtype)
    return jnp.einsum("gcn,gh->gnhc", c, eye).reshape(S5_GROUPS * S5_STATE, S5_GROUPS * S5_CH)


def _layer_params(l, g_mix_pre, g_mix_post, g_mlp_pre, g_mlp_post, w_in, w_out, rw_mu, rw_w0, rw_w2, rw_a0, rw_a2,
                  rw_g2, rw_kk, rw_ka, rw_rk, rw_ln_w, rw_ln_b, s5_a_re, s5_a_im, s5_log_dt, s5_b_re, s5_b_im,
                  s5_c_re, s5_c_im, s5_d, s5_w_glu, s5_b_glu, gdn_conv_w, gdn_a_log, gdn_dt_bias, gdn_norm_w,
                  swa_sinks, w_up, w_down):
    row = lambda a: a[l].astype(F32).reshape(1, -1)
    wi = w_in[l].astype(F32)
    o_s5 = RW_PROJ
    o_gdn = o_s5 + GROUP_WIDTH
    o_ba = o_gdn + GDN_QKV
    o_z = o_ba + 2 * N_HEADS
    o_swa = o_z + GROUP_WIDTH
    w_all = jnp.concatenate([
        wi[:, :o_s5], wi[:, o_s5:o_gdn], wi[:, o_gdn:o_ba], wi[:, o_z:o_swa],
        jnp.pad(wi[:, o_ba:o_z], ((0, 0), (0, LANES - 2 * N_HEADS))), wi[:, o_swa:]], axis=1).astype(BF16)
    lane_pad = lambda a: jnp.pad(a[l].astype(F32), (N_HEADS, LANES - 2 * N_HEADS)).reshape(1, LANES)
    return {
        "g_mix_pre": row(g_mix_pre), "g_mix_post": row(g_mix_post), "g_mlp_pre": row(g_mlp_pre),
        "g_mlp_post": row(g_mlp_post), "w_all": w_all, "w_out": w_out[l].astype(BF16),
        "w_up": w_up[l].astype(BF16), "w_down": w_down[l].astype(BF16),
        "rw": {"mu": row(rw_mu), "w0": row(rw_w0), "a0": row(rw_a0), "kk": row(rw_kk), "ka": row(rw_ka),
               "rk": row(rw_rk), "ln_w": row(rw_ln_w), "ln_b": row(rw_ln_b),
               "w2": _pad_rows(rw_w2[l].astype(F32), 0, GROUP_WIDTH).astype(BF16),
               "a2": _pad_rows(rw_a2[l].astype(F32), RW_DECAY_RANK, GROUP_WIDTH).astype(BF16),
               "g2": _pad_rows(rw_g2[l].astype(F32), RW_DECAY_RANK + RW_ICLR_RANK, GROUP_WIDTH).astype(BF16)},
        "s5": {"a_re": row(s5_a_re), "a_im": row(s5_a_im),
               "log_dt": jnp.repeat(s5_log_dt[l].astype(F32), S5_STATE).reshape(1, S5_LANES),
               "b_re": _block_diag_in(s5_b_re[l].astype(F32)).astype(BF16),
               "b_im": _block_diag_in(s5_b_im[l].astype(F32)).astype(BF16),
               "c_re": _block_diag_out(s5_c_re[l].astype(F32)).astype(BF16),
               "c_im": _block_diag_out(s5_c_im[l].astype(F32)).astype(BF16),
               "d": row(s5_d), "w_glu": s5_w_glu[l].astype(BF16), "b_glu": row(s5_b_glu)},
        "gdn": {"conv_w": gdn_conv_w[l].astype(F32), "a_log": lane_pad(gdn_a_log), "dt_bias": lane_pad(gdn_dt_bias),
                "norm_w": jnp.tile(gdn_norm_w[l].astype(F32), N_HEADS).reshape(1, GROUP_WIDTH)},
        "sinks": jnp.pad(swa_sinks[l].astype(F32), (0, LANES - N_HEADS)).reshape(1, LANES),
    }


def _layer(x, start, st, layer, prm):
    rw_s, rw_shift, s5_re, s5_im, gdn_s, gdn_conv, swa_k, swa_v = st
    bsz, seq, _ = x.shape
    tokens = bsz * seq
    tm = min(TOKEN_ROWS, tokens)
    chunk = CHUNK if seq % CHUNK == 0 else -(-seq // SUBLANES) * SUBLANES
    seq_pad = -(-seq // chunk) * chunk
    n_valid = chunk - (seq_pad - seq)
    n_chunks = math.gcd(seq_pad // chunk, MIX_CHUNKS)
    bb = {name: math.gcd(bsz, pair[seq_pad == chunk]) for name, pair in MIXER_BATCH.items()}
    assert seq_pad == seq or seq_pad == chunk
    assert seq >= GDN_CONV - 1

    x2d = x.reshape(tokens, D_MODEL)
    p_rw, p_s5, p_qkv, p_z, p_ba, p_swa = (
        a.reshape(bsz, seq, -1) for a in _in_proj(x2d, prm["g_mix_pre"], prm["w_all"], tm))
    pad = lambda a: a if seq_pad == seq else jnp.pad(a, ((0, 0), (0, seq_pad - seq), (0, 0)))

    o_rw, rw_s_new = _rwkv(pad(p_rw), rw_shift, rw_s, layer, prm["rw"], bb["rwkv"], chunk, n_chunks, n_valid)
    rw_shift_new = p_rw[:, seq - 1:, :]

    rows = 16
    steps = min(seq, 64)
    o_s5_t, s5_re_new, s5_im_new = _s5(jnp.swapaxes(p_s5, 0, 1), s5_re.reshape(bsz, S5_LANES),
                                       s5_im.reshape(bsz, S5_LANES), prm["s5"], steps, rows)
    o_s5 = jnp.swapaxes(o_s5_t, 0, 1)

    o_gdn, gdn_s_new = _gdn(pad(p_qkv), pad(p_z), pad(p_ba), gdn_conv, gdn_s, layer, prm["gdn"],
                            bb["gdn"], chunk, n_chunks, n_valid)
    gdn_conv_new = p_qkv[:, seq - (GDN_CONV - 1):, :]

    qb = WINDOW if seq % WINDOW == 0 else seq_pad
    cos, sin = _rope_tables(start, seq_pad)
    wbuf = swa_k.shape[1]
    o_swa, k_rot = _swa(pad(p_swa), cos, sin, swa_k.reshape(bsz, wbuf, SWA_KV_WIDTH),
                        swa_v.reshape(bsz, wbuf, SWA_KV_WIDTH), prm["sinks"], bb["swa"], qb, start)
    k_new = k_rot[:, :qb - (seq_pad - seq), :] if seq_pad != seq else k_rot
    v_new = p_swa[:, seq - min(seq, wbuf):, GROUP_WIDTH + SWA_KV_WIDTH:]
    kv_shape = (bsz, -1, SWA_KV_HEADS, HEAD_DIM)
    if k_new.shape[1] >= wbuf:
        swa_k_new = k_new[:, -wbuf:].reshape(kv_shape)
        swa_v_new = v_new[:, -wbuf:].reshape(kv_shape)
    else:
        swa_k_new = jnp.concatenate([swa_k, k_new.reshape(kv_shape)], axis=1)[:, -wbuf:]
        swa_v_new = jnp.concatenate([swa_v, v_new.reshape(kv_shape)], axis=1)[:, -wbuf:]

    flat = lambda a: a[:, :seq, :].reshape(tokens, GROUP_WIDTH)
    y = _post(x2d, (flat(o_rw), flat(o_s5), flat(o_gdn), flat(o_swa)), prm["w_out"], prm["g_mix_post"],
              prm["g_mlp_pre"], prm["w_up"], prm["w_down"], prm["g_mlp_post"], tm)
    new_state = (rw_s_new, rw_shift_new, s5_re_new.reshape(s5_re.shape), s5_im_new.reshape(s5_im.shape),
                 gdn_s_new, gdn_conv_new, swa_k_new, swa_v_new)
    return y.reshape(bsz, seq, D_MODEL), new_state


def kernel(x_prompt, x_sample, state_rwkv, state_rwkv_shift, state_s5_re, state_s5_im, state_gdn, state_gdn_conv,
           cache_swa_k, cache_swa_v, g_mix_pre, g_mix_post, g_mlp_pre, g_mlp_post, w_in, w_out, rw_mu, rw_w0, rw_w2,
           rw_a0, rw_a2, rw_g2, rw_kk, rw_ka, rw_rk, rw_ln_w, rw_ln_b, s5_a_re, s5_a_im, s5_log_dt, s5_b_re, s5_b_im,
           s5_c_re, s5_c_im, s5_d, s5_w_glu, s5_b_glu, gdn_conv_w, gdn_a_log, gdn_dt_bias, gdn_norm_w, swa_sinks,
           w_up, w_down):
    weights = (g_mix_pre, g_mix_post, g_mlp_pre, g_mlp_post, w_in, w_out, rw_mu, rw_w0, rw_w2, rw_a0, rw_a2, rw_g2,
               rw_kk, rw_ka, rw_rk, rw_ln_w, rw_ln_b, s5_a_re, s5_a_im, s5_log_dt, s5_b_re, s5_b_im, s5_c_re, s5_c_im,
               s5_d, s5_w_glu, s5_b_glu, gdn_conv_w, gdn_a_log, gdn_dt_bias, gdn_norm_w, swa_sinks, w_up, w_down)
    depth = w_in.shape[0]
    bp = x_prompt.shape[0]
    past_len = 16384
    xp = x_prompt.astype(F32)
    xs = x_sample.astype(F32)
    zp = lambda *shape: jnp.zeros((bp,) + shape, F32)
    zero_s = jnp.zeros((1, bp, N_HEADS, HEAD_DIM, HEAD_DIM), F32)
    prompt_init = (zero_s, zp(1, RW_PROJ), zp(S5_GROUPS, S5_STATE), zp(S5_GROUPS, S5_STATE),
                   zero_s, zp(GDN_CONV - 1, GDN_QKV),
                   zp(WINDOW, SWA_KV_HEADS, HEAD_DIM), zp(WINDOW, SWA_KV_HEADS, HEAD_DIM))
    sample_states = (state_rwkv, state_rwkv_shift, state_s5_re, state_s5_im, state_gdn, state_gdn_conv,
                     cache_swa_k, cache_swa_v)
    new_p, new_s = [], []
    for l in range(depth):
        prm = _layer_params(l, *weights)
        xp, st_p = _layer(xp, 0, prompt_init, 0, prm)
        stacked = (0, 4)
        st_l = tuple(a.astype(F32) if i in stacked else a[l].astype(F32) for i, a in enumerate(sample_states))
        xs, st_s = _layer(xs, past_len, st_l, l, prm)
        new_p.append(st_p)
        new_s.append(st_s)
    outs = [xp.astype(x_prompt.dtype), xs.astype(x_sample.dtype)]
    for i in range(len(sample_states)):
        outs.append(jnp.stack([st[i] for st in new_p], axis=0))
        outs.append(jnp.stack([st[i] for st in new_s], axis=0))
    return tuple(outs)
```

```python
import functools
import math

import jax
import jax.numpy as jnp
from jax import lax
from jax.experimental import pallas as pl
from jax.experimental.pallas import tpu as pltpu

F32 = jnp.float32
BF16 = jnp.bfloat16

D_MODEL = 1024
HEAD_DIM = 64
N_HEADS = 4
GROUP_WIDTH = N_HEADS * HEAD_DIM
NORM_EPS = 1e-6
RW_LN_EPS = 64e-5
RW_PROJ = 1024
RW_LORA_OFF = 3 * GROUP_WIDTH
RW_DECAY_RANK, RW_ICLR_RANK, RW_GATE_RANK = 64, 64, 128
S5_GROUPS, S5_CH, S5_STATE = 16, 16, 64
S5_LANES = S5_GROUPS * S5_STATE
GDN_CONV = 4
GDN_QKV = 3 * GROUP_WIDTH
SWA_KV_HEADS = 2
SWA_KV_WIDTH = SWA_KV_HEADS * HEAD_DIM
SWA_PROJ = GROUP_WIDTH + 2 * SWA_KV_WIDTH
WINDOW = 128
ROPE_DIM = 16
ROPE_THETA = 500000.0
D_FF = 4096
LANES = 128
SUBLANES = 8
TOKEN_ROWS = 1024
CHUNK = 64
MIX_CHUNKS = 4
MIXER_BATCH = {"rwkv": (8, 16), "gdn": (8, 16), "swa": (4, 32)}
TRI_BLOCK = 16
VMEM_LIMIT = 56 * 1024 * 1024

PROJ_WIDTHS = (RW_PROJ, GROUP_WIDTH, GDN_QKV, GROUP_WIDTH, LANES, SWA_PROJ)


def _dot(a, b, dims):
    return lax.dot_general(a.astype(BF16), b.astype(BF16), (dims, ((), ())), preferred_element_type=F32)


def _mm(a, b):
    return _dot(a, b, ((1,), (0,)))


def _mm_nt(a, b):
    return _dot(a, b, ((1,), (1,)))


def _mm_tn(a, b):
    return _dot(a, b, ((0,), (0,)))


def _split2(x):
    hi = x.astype(BF16)
    lo = (x - hi.astype(F32)).astype(BF16)
    return hi, lo


def _mm_exact_rhs(x, m):
    hi, lo = _split2(x)
    return _mm(hi, m) + _mm(lo, m)


def _mm_exact_lhs(m, x):
    hi, lo = _split2(x)
    return _mm(m, hi) + _mm(m, lo)


def _head_ones(width):
    r = lax.broadcasted_iota(jnp.int32, (width, width), 0)
    c = lax.broadcasted_iota(jnp.int32, (width, width), 1)
    return (r // HEAD_DIM == c // HEAD_DIM).astype(BF16)


def _head_sum(x, ones):
    return _mm(x, ones)


def _tri_masks(n):
    r = lax.broadcasted_iota(jnp.int32, (n, n), 0)
    c = lax.broadcasted_iota(jnp.int32, (n, n), 1)
    return r, c, r > c, r >= c


def _softplus(x):
    return jnp.maximum(x, 0.0) + jnp.log(1.0 + jnp.exp(-jnp.abs(x)))


def _silu(x):
    return x * jax.nn.sigmoid(x)


def _rms(x, g):
    return x * lax.rsqrt(jnp.mean(x * x, axis=-1, keepdims=True) + NORM_EPS) * g


def _per_head(y, heads):
    if heads == 1:
        return y
    yb = y.astype(BF16)
    tall = jnp.concatenate([yb] * heads, axis=0)
    r = lax.broadcasted_iota(jnp.int32, tall.shape, 0) // y.shape[0]
    c = lax.broadcasted_iota(jnp.int32, tall.shape, 1) // (y.shape[1] // heads)
    return jnp.where(r == c, tall, jnp.zeros((), BF16))


def _each(fn, *lists):
    return [fn(*xs) for xs in zip(*lists)]


def _tri_inv(lows, n, paired=False):
    pair = paired or (2 * n == LANES and len(lows) % 2 == 0)
    if pair and not paired:
        lows = [jnp.concatenate(lows[j:j + 2], axis=1) for j in range(0, len(lows), 2)]
    width = 2 * n if pair else n
    r = lax.broadcasted_iota(jnp.int32, (n, width), 0)
    c = lax.broadcasted_iota(jnp.int32, (n, width), 1) % n
    eye = (r == c).astype(F32)
    mul = lambda x, y: _mm(x, _per_head(y, 2 if pair else 1))
    bs = min(TRI_BLOCK, n)
    if n > bs:
        same = (r // bs) == (c // bs)
        diags = [jnp.where(same, low, 0.0) for low in lows]
        rests = _each(lambda low, d: low - d, lows, diags)
    else:
        diags, rests = lows, None
    invs = [eye - d for d in diags]
    pws, p = _each(lambda x: mul(x, x), diags), 2
    while 2 * p < bs:
        both = _each(lambda x, i: mul(jnp.concatenate([x, i], axis=0), x), pws, invs)
        pws = [z[:n] for z in both]
        invs = _each(lambda i, z: i + z[n:], invs, both)
        p *= 2
    invs = _each(lambda i, x: i + mul(i, x), invs, pws)
    if rests is not None:
        nils = _each(mul, invs, rests)
        outs = [eye - x for x in nils]
        for _ in range(n // bs - 2):
            outs = _each(lambda x, o: eye - mul(x, o), nils, outs)
        invs = _each(mul, outs, invs)
    if pair and not paired:
        invs = [half for x in invs for half in (x[:, :n], x[:, n:])]
    return invs


def _head_groups(chunk):
    hp = 2 if chunk == HEAD_DIM else 1
    return hp, hp * HEAD_DIM, hp * chunk, N_HEADS // hp


def _own_blocks(x, hp):
    if hp == 1:
        return x
    r = lax.broadcasted_iota(jnp.int32, x.shape, 0) // HEAD_DIM
    c = lax.broadcasted_iota(jnp.int32, x.shape, 1) // HEAD_DIM
    return jnp.where(r == c, x, 0.0)


def _own_side(x, hp):
    if hp == 1:
        return x
    low = lax.broadcasted_iota(jnp.int32, (HEAD_DIM, x.shape[1]), 1) < HEAD_DIM
    return jnp.where(low, x[:HEAD_DIM], x[HEAD_DIM:])


def _carry_chunks(s_scr, bb, n_chunks, hp, out0, left, decay, pb, q_mat):
    n_grp = N_HEADS // hp
    group = bb * n_grp
    state = [s_scr[bi, g] if hp == 1 else jnp.concatenate([s_scr[bi, g * hp + i] for i in range(hp)], axis=1)
             for bi in range(bb) for g in range(n_grp)]
    out = {}
    for c in range(n_chunks):
        grp = slice(c * group, (c + 1) * group)
        outs = _each(lambda o_, l_, s_: o_ + _mm_nt(l_, _per_head(s_, hp)), out0[grp], left[grp], state)
        state = _each(lambda s_, d_, p_, q_: s_ * d_ - _mm(s_, p_) + q_, state, decay[grp], pb[grp], q_mat[grp])
        for bi in range(bb):
            out[bi, c] = jnp.concatenate(outs[bi * n_grp:(bi + 1) * n_grp], axis=1)
    for bi in range(bb):
        for g in range(n_grp):
            for i in range(hp):
                s_scr[bi, g * hp + i] = state[bi * n_grp + g][:, i * HEAD_DIM:(i + 1) * HEAD_DIM]
    pieces = [out[bi, c] for bi in range(bb) for c in range(n_chunks)]
    return pieces[0] if len(pieces) == 1 else jnp.concatenate(pieces, axis=0)


def _const_spec(shape):
    zeros = (0,) * len(shape)
    return pl.BlockSpec(shape, lambda *_: zeros, pipeline_mode=pl.Buffered(1))


def _in_proj_kernel(x_ref, g_ref, w_ref, *out_refs):
    h = _rms(x_ref[...], g_ref[...]).astype(BF16)
    off = 0
    for o_ref, width in zip(out_refs, PROJ_WIDTHS):
        o_ref[...] = _mm(h, w_ref[:, off:off + width])
        off += width


def _in_proj(x2d, g, w_all, tm):
    t = x2d.shape[0]
    wtot = sum(PROJ_WIDTHS)
    return pl.pallas_call(
        _in_proj_kernel,
        grid=(t // tm,),
        in_specs=[pl.BlockSpec((tm, D_MODEL), lambda i: (i, 0)),
                  _const_spec((1, D_MODEL)),
                  _const_spec((D_MODEL, wtot))],
        out_specs=[pl.BlockSpec((tm, w), lambda i: (i, 0)) for w in PROJ_WIDTHS],
        out_shape=[jax.ShapeDtypeStruct((t, w), F32) for w in PROJ_WIDTHS],
        compiler_params=pltpu.CompilerParams(dimension_semantics=("arbitrary",), vmem_limit_bytes=VMEM_LIMIT),
        name="in_proj",
    )(x2d, g, w_all)


FF_CHUNK = 1024


def _post_kernel(x_ref, o0_ref, o1_ref, o2_ref, o3_ref, wout_ref, gpost_ref, gpre_ref, wup_ref, wdn_ref,
                 gmlp_ref, y_ref):
    mix = None
    for i, o_ref in enumerate((o0_ref, o1_ref, o2_ref, o3_ref)):
        part = _mm(o_ref[...].astype(BF16), wout_ref[i * GROUP_WIDTH:(i + 1) * GROUP_WIDTH, :])
        mix = part if mix is None else mix + part
    x1 = x_ref[...] + _rms(mix, gpost_ref[...])
    h = _rms(x1, gpre_ref[...]).astype(BF16)
    acc = None
    for c in range(D_FF // FF_CHUNK):
        u = _mm(h, wup_ref[:, c * FF_CHUNK:(c + 1) * FF_CHUNK])
        u = jnp.square(jnp.maximum(u, 0.0)).astype(BF16)
        part = _mm(u, wdn_ref[c * FF_CHUNK:(c + 1) * FF_CHUNK, :])
        acc = part if acc is None else acc + part
    y_ref[...] = x1 + _rms(acc, gmlp_ref[...])


def _post(x2d, outs, w_out, g_post, g_pre, w_up, w_dn, g_mlp, tm):
    t = x2d.shape[0]
    row = lambda w: pl.BlockSpec((tm, w), lambda i: (i, 0))
    return pl.pallas_call(
        _post_kernel,
        grid=(t // tm,),
        in_specs=[row(D_MODEL)] + [row(GROUP_WIDTH)] * 4 + [
            _const_spec((D_MODEL, D_MODEL)), _const_spec((1, D_MODEL)), _const_spec((1, D_MODEL)),
            _const_spec((D_MODEL, D_FF)), _const_spec((D_FF, D_MODEL)), _const_spec((1, D_MODEL))],
        out_specs=row(D_MODEL),
        out_shape=jax.ShapeDtypeStruct((t, D_MODEL), F32),
        compiler_params=pltpu.CompilerParams(dimension_semantics=("arbitrary",), vmem_limit_bytes=VMEM_LIMIT),
        name="post_mlp",
    )(x2d, *outs, w_out, g_post, g_pre, w_up, w_dn, g_mlp)


def _chunk_sums(x, chunk, rows):
    _, _, _, incl = _tri_masks(chunk)
    tri = incl.astype(BF16)
    cums = [_mm_exact_lhs(tri, x[c * chunk:(c + 1) * chunk]) for c in range(rows // chunk)]
    if len(cums) == 1:
        return cums[0], cums[0][chunk - 1:chunk, :]
    tots = [jnp.broadcast_to(cu[chunk - 1:chunk, :], cu.shape) for cu in cums]
    return jnp.concatenate(cums, axis=0), jnp.concatenate(tots, axis=0)


def _rwkv_kernel(p_ref, shift_ref, s0_ref, mu_ref, w0_ref, a0_ref, kkp_ref, kap_ref, rk_ref, lnw_ref, lnb_ref,
                 w2_ref, a2_ref, g2_ref, o_ref, s_out_ref, s_scr, carry_scr, *, bb, chunk, n_chunks, n_valid):
    c_idx = pl.program_id(1)
    rows = chunk * n_chunks
    total = bb * rows

    @pl.when(c_idx == 0)
    def _():
        s_scr[...] = s0_ref[...]
        carry_scr[...] = shift_ref[...]

    first = lax.broadcasted_iota(jnp.int32, (rows, 1), 0) == 0
    last = rows - chunk + n_valid
    prevs = []
    for bi in range(bb):
        p_b = p_ref[bi]
        prevs.append(jnp.where(first, carry_scr[bi], pltpu.roll(p_b, 1, 0)))
        carry_scr[bi] = p_b[last - 1:last, :]
    p = p_ref[...].reshape(total, RW_PROJ)
    prev = prevs[0] if bb == 1 else jnp.concatenate(prevs, axis=0)
    row = lax.broadcasted_iota(jnp.int32, (total, 1), 0)
    xs = p + (prev - p) * mu_ref[...]
    r = xs[:, 0:GROUP_WIDTH]
    k = xs[:, GROUP_WIDTH:2 * GROUP_WIDTH]
    v = xs[:, 2 * GROUP_WIDTH:3 * GROUP_WIDTH]
    lora = xs[:, RW_LORA_OFF:RW_PROJ]

    z_w = w0_ref[...] + _mm(jnp.tanh(lora), w2_ref[...])
    logw = -math.exp(-0.5) * jax.nn.sigmoid(z_w)
    a = jax.nn.sigmoid(a0_ref[...] + _mm(lora, a2_ref[...]))
    gate = _mm(jax.nn.sigmoid(lora), g2_ref[...])

    ones = _head_ones(GROUP_WIDTH)
    kk = k * kkp_ref[...]
    kk = kk * lax.rsqrt(jnp.maximum(_head_sum(kk * kk, ones), 1e-12))
    k2 = k * (1.0 + (a - 1.0) * kap_ref[...])
    b = kk * a
    if n_valid < chunk:
        valid = (row % chunk) < n_valid
        logw = jnp.where(valid, logw, 0.0)
        kk = jnp.where(valid, kk, 0.0)
        b = jnp.where(valid, b, 0.0)
        k2v = jnp.where(valid, k2, 0.0)
        vv = jnp.where(valid, v, 0.0)
    else:
        k2v, vv = k2, v

    log_cum, log_end = _chunk_sums(logw, chunk, total)
    w_inv = jnp.exp(-log_cum)
    w_end = jnp.exp(log_end)
    w_tail = w_end * w_inv
    a_bar = kk * jnp.exp(log_cum - logw)
    b_bar = b * w_inv
    k_bar = k2v * w_inv
    r_bar = r * jnp.exp(log_cum)
    b_end = b * w_tail
    k_end = k2v * w_tail

    hp, wid, swid, n_grp = _head_groups(chunk)
    sr = lax.broadcasted_iota(jnp.int32, (chunk, swid), 0)
    sc = lax.broadcasted_iota(jnp.int32, (chunk, swid), 1) % chunk
    strict, incl = sr > sc, sr >= sc
    tiles = [(slice(bi * rows + c * chunk, bi * rows + (c + 1) * chunk), slice(g * wid, (g + 1) * wid))
             for c in range(n_chunks) for bi in range(bb) for g in range(n_grp)]
    cut = lambda x: [x[rs, sl] for rs, sl in tiles]
    ab, rb, bs, kb, be, ke, vh = map(cut, (a_bar, r_bar, b_bar, k_bar, b_end, k_end, vv))
    ar = _each(lambda a_, r_: jnp.concatenate([a_, r_], axis=0), ab, rb)
    gb = _each(lambda x, b_: _mm_nt(x, _per_head(b_, hp)), ar, bs)
    gk = _each(lambda x, k_: _mm_nt(x, _per_head(k_, hp)), ar, kb)
    l_ab = [jnp.where(strict, g[:chunk], 0.0) for g in gb]
    m_rb = [jnp.where(incl, g[chunk:], 0.0) for g in gb]
    l_ak = [jnp.where(strict, g[:chunk], 0.0) for g in gk]
    m_rk = [jnp.where(incl, g[chunk:], 0.0) for g in gk]
    tinv = _tri_inv(l_ab, chunk, paired=True) if hp == 2 else _tri_inv(l_ab, chunk)
    side = lambda x, y_: jnp.concatenate([_per_head(x, hp), _per_head(y_, hp)], axis=1)
    lv = _each(lambda l_, v_: _mm(l_, _per_head(v_, hp)), l_ak, vh)
    tt = _each(lambda t_, a_, l_: _mm(t_, side(a_, l_)), tinv, ab, lv)
    ta = [t_[:, :wid] for t_ in tt]
    c0 = [-t_[:, wid:] for t_ in tt]
    pq = _each(lambda a_, c_, b_: _mm_tn(jnp.concatenate([a_, c_], axis=1), b_), ta, c0, be)
    vk = _each(_mm_tn, vh, ke)
    ry = _each(lambda m_, a_, c_: _mm(m_, side(a_, c_)), m_rb, ta, c0)
    rkv = _each(lambda m_, v_: _mm(m_, _per_head(v_, hp)), m_rk, vh)
    pb = [_own_blocks(x[:wid], hp) for x in pq]
    q_mat = _each(lambda x, y_: _own_side(x[wid:] + y_, hp), pq, vk)
    rr = _each(lambda r_, x: r_ - x[:, :wid], rb, ry)
    y0 = _each(lambda x, y_: x[:, wid:] + y_, ry, rkv)
    we = [w_end[rs.start:rs.start + 1, sl] for rs, sl in tiles]
    y = _carry_chunks(s_scr, bb, n_chunks, hp, y0, rr, we, pb, q_mat)

    inv_d = 1.0 / HEAD_DIM
    mean = _head_sum(y, ones) * inv_d
    d = y - mean
    var = _head_sum(d * d, ones) * inv_d
    yn = d * lax.rsqrt(var + RW_LN_EPS) * lnw_ref[...] + lnb_ref[...]
    bonus = _head_sum(r * k2 * rk_ref[...], ones) * v
    o_ref[...] = ((yn + bonus) * gate).reshape(bb, rows, GROUP_WIDTH)

    @pl.when(c_idx == pl.num_programs(1) - 1)
    def _():
        s_out_ref[...] = s_scr[...]


def _state_specs(bb, layer):
    shape = (bb, N_HEADS, HEAD_DIM, HEAD_DIM)
    return (pl.BlockSpec((None,) + shape, lambda b, c: (layer, b, 0, 0, 0)),
            pl.BlockSpec(shape, lambda b, c: (b, 0, 0, 0)))


def _rwkv(p, shift, s0, layer, prm, bb, chunk, n_chunks, n_valid):
    bsz, seq, _ = p.shape
    rows = chunk * n_chunks
    assert n_valid == chunk or n_chunks == 1
    vec = _const_spec((1, GROUP_WIDTH))
    mat = _const_spec((GROUP_WIDTH, GROUP_WIDTH))
    state_in, state = _state_specs(bb, layer)
    return pl.pallas_call(
        functools.partial(_rwkv_kernel, bb=bb, chunk=chunk, n_chunks=n_chunks, n_valid=n_valid),
        grid=(bsz // bb, seq // rows),
        in_specs=[pl.BlockSpec((bb, rows, RW_PROJ), lambda b, c: (b, c, 0)),
                  pl.BlockSpec((bb, 1, RW_PROJ), lambda b, c: (b, 0, 0)),
                  state_in, _const_spec((1, RW_PROJ))] + [vec] * 7 + [mat] * 3,
        out_specs=[pl.BlockSpec((bb, rows, GROUP_WIDTH), lambda b, c: (b, c, 0)), state],
        out_shape=[jax.ShapeDtypeStruct((bsz, seq, GROUP_WIDTH), F32),
                   jax.ShapeDtypeStruct(s0.shape[1:], F32)],
        scratch_shapes=[pltpu.VMEM((bb, N_HEADS, HEAD_DIM, HEAD_DIM), F32), pltpu.VMEM((bb, 1, RW_PROJ), F32)],
        compiler_params=pltpu.CompilerParams(dimension_semantics=("arbitrary", "arbitrary")),
        name="rwkv7",
    )(p, shift, s0, prm["mu"], prm["w0"], prm["a0"], prm["kk"], prm["ka"], prm["rk"], prm["ln_w"], prm["ln_b"],
      prm["w2"], prm["a2"], prm["g2"])


def _gdn_kernel(qkv_ref, z_ref, ba_ref, buf_ref, s0_ref, convw_ref, alog_ref, dtb_ref, nw_ref,
                o_ref, s_out_ref, s_scr, x_scr, *, bb, chunk, n_chunks, n_valid):
    c_idx = pl.program_id(1)
    rows = chunk * n_chunks
    total = bb * rows
    pad = SUBLANES

    @pl.when(c_idx == 0)
    def _():
        s_scr[...] = s0_ref[...]
        x_scr[:, 0:pad, :] = jnp.zeros((bb, pad, GDN_QKV), F32)
        x_scr[:, pad - (GDN_CONV - 1):pad, :] = buf_ref[...]

    last = rows - chunk + n_valid
    convs = []
    for bi in range(bb):
        x_scr[bi, pad:pad + rows, :] = qkv_ref[bi]
        conv = None
        for j in range(GDN_CONV):
            term = x_scr[bi, pad - j:pad - j + rows, :] * convw_ref[GDN_CONV - 1 - j:GDN_CONV - j, :]
            conv = term if conv is None else conv + term
        x_scr[bi, 0:pad, :] = x_scr[bi, last:last + pad, :]
        convs.append(conv)
    act = _silu(convs[0] if bb == 1 else jnp.concatenate(convs, axis=0))
    q = act[:, 0:GROUP_WIDTH]
    k = act[:, GROUP_WIDTH:2 * GROUP_WIDTH]
    v = act[:, 2 * GROUP_WIDTH:3 * GROUP_WIDTH]
    ones = _head_ones(GROUP_WIDTH)
    q = q * lax.rsqrt(jnp.maximum(_head_sum(q * q, ones), 1e-12)) * (HEAD_DIM ** -0.5)
    k = k * lax.rsqrt(jnp.maximum(_head_sum(k * k, ones), 1e-12))

    ba = ba_ref[...].reshape(total, LANES)
    row = lax.broadcasted_iota(jnp.int32, (total, LANES), 0)
    beta = jax.nn.sigmoid(ba)
    glog = -jnp.exp(alog_ref[...]) * _softplus(ba + dtb_ref[...])
    if n_valid < chunk:
        valid = (row % chunk) < n_valid
        beta = jnp.where(valid, beta, 0.0)
        glog = jnp.where(valid, glog, 0.0)
    gcum, gend = _chunk_sums(glog, chunk, total)
    gcum_t = gcum.T
    er = lax.broadcasted_iota(jnp.int32, (LANES, GROUP_WIDTH), 0)
    ec = lax.broadcasted_iota(jnp.int32, (LANES, GROUP_WIDTH), 1) // HEAD_DIM
    beta_w = _mm_exact_rhs(beta, (er == ec).astype(BF16))
    spread = (er == ec + N_HEADS).astype(BF16)
    gcum_w = _mm_exact_rhs(gcum, spread)
    gend_w = _mm_exact_rhs(gend, spread)
    gam_w = jnp.exp(gcum_w)
    bgk = beta_w * gam_w * k
    bv = beta_w * v
    gq = gam_w * q
    kt = k * jnp.exp(gend_w - gcum_w)
    end_w = jnp.exp(gend_w)

    hp, wid, swid, n_grp = _head_groups(chunk)
    sr = lax.broadcasted_iota(jnp.int32, (chunk, swid), 0)
    sc = lax.broadcasted_iota(jnp.int32, (chunk, swid), 1) % chunk
    strict, incl = sr > sc, sr >= sc
    tiles = [(slice(bi * rows + c * chunk, bi * rows + (c + 1) * chunk), g)
             for c in range(n_chunks) for bi in range(bb) for g in range(n_grp)]
    lanes = lambda x: [x[rs, g * wid:(g + 1) * wid] for rs, g in tiles]
    slab = lambda x: [jnp.concatenate([x[rs, h * HEAD_DIM:h * HEAD_DIM + chunk] for h in range(g * hp, (g + 1) * hp)],
                                      axis=1) if hp * chunk != wid else x[rs, g * wid:(g + 1) * wid]
                      for rs, g in tiles]
    k_g, q_g, bgk_g, bv_g, gq_g, kt_g = map(lanes, (k, q, bgk, bv, gq, kt))
    g_row = [jnp.concatenate([gcum_t[N_HEADS + h:N_HEADS + h + 1, rs] for h in range(g * hp, (g + 1) * hp)], axis=1)
             for rs, g in tiles]
    decay = _each(lambda gc, gr: jnp.where(incl, jnp.exp(jnp.where(incl, gc - gr, 0.0)), 0.0), slab(gcum_w), g_row)
    gram = _each(lambda k_, q_: _mm_nt(jnp.concatenate([k_, q_], axis=0), _per_head(k_, hp)), k_g, q_g)
    a_mat = _each(lambda b_, d_, g_: jnp.where(strict, b_ * d_ * g_[:chunk], 0.0), slab(beta_w), decay, gram)
    tinv = _tri_inv(a_mat, chunk, paired=True) if hp == 2 else _tri_inv(a_mat, chunk)
    side = lambda x, y_: jnp.concatenate([_per_head(x, hp), _per_head(y_, hp)], axis=1)
    tt = _each(lambda t_, x, y_: _mm(t_, side(x, y_)), tinv, bgk_g, bv_g)
    qo = _each(lambda g_, d_, t_: _mm(g_[chunk:] * d_, side(t_[:, :wid], t_[:, wid:])), gram, decay, tt)
    pq = _each(_mm_tn, tt, kt_g)
    qq = _each(lambda g_, x: g_ - x[:, :wid], gq_g, qo)
    o0 = [x[:, wid:] for x in qo]
    pb = [_own_blocks(x[:wid], hp) for x in pq]
    q_mat = [_own_side(x[wid:], hp) for x in pq]
    ge = [end_w[rs.start:rs.start + 1, g * wid:(g + 1) * wid] for rs, g in tiles]
    o = _carry_chunks(s_scr, bb, n_chunks, hp, o0, qq, ge, pb, q_mat)
    ms = _head_sum(o * o, ones) * (1.0 / HEAD_DIM)
    zz = z_ref[...].reshape(total, GROUP_WIDTH)
    o_ref[...] = (o * lax.rsqrt(ms + NORM_EPS) * nw_ref[...] * _silu(zz)).reshape(bb, rows, GROUP_WIDTH)

    @pl.when(c_idx == pl.num_programs(1) - 1)
    def _():
        s_out_ref[...] = s_scr[...]


def _gdn(qkv, z, ba, buf, s0, layer, prm, bb, chunk, n_chunks, n_valid):
    bsz, seq, _ = qkv.shape
    rows = chunk * n_chunks
    assert n_valid == chunk or n_chunks == 1
    state_in, state = _state_specs(bb, layer)
    blk = lambda w: pl.BlockSpec((bb, rows, w), lambda b, c: (b, c, 0))
    return pl.pallas_call(
        functools.partial(_gdn_kernel, bb=bb, chunk=chunk, n_chunks=n_chunks, n_valid=n_valid),
        grid=(bsz // bb, seq // rows),
        in_specs=[blk(GDN_QKV), blk(GROUP_WIDTH), blk(LANES),
                  pl.BlockSpec((bb, GDN_CONV - 1, GDN_QKV), lambda b, c: (b, 0, 0)), state_in,
                  _const_spec((GDN_CONV, GDN_QKV)), _const_spec((1, LANES)), _const_spec((1, LANES)),
                  _const_spec((1, GROUP_WIDTH))],
        out_specs=[blk(GROUP_WIDTH), state],
        out_shape=[jax.ShapeDtypeStruct((bsz, seq, GROUP_WIDTH), F32), jax.ShapeDtypeStruct(s0.shape[1:], F32)],
        scratch_shapes=[pltpu.VMEM((bb, N_HEADS, HEAD_DIM, HEAD_DIM), F32),
                        pltpu.VMEM((bb, SUBLANES + rows, GDN_QKV), F32)],
        compiler_params=pltpu.CompilerParams(dimension_semantics=("arbitrary", "arbitrary")),
        name="gdn",
    )(qkv, z, ba, buf, s0, prm["conv_w"], prm["a_log"], prm["dt_bias"], prm["norm_w"])


def _gelu_tanh(x):
    return 0.5 * x * (1.0 + jnp.tanh(math.sqrt(2.0 / math.pi) * (x + 0.044715 * (x * x * x))))


def _s5_kernel(u_ref, x0re_ref, x0im_ref, are_ref, aim_ref, ldt_ref, bre_ref, bim_ref, cre_ref, cim_ref, d_ref,
               wglu_ref, bglu_ref, o_ref, hre_out_ref, him_out_ref, hre_scr, him_scr, sre_scr, sim_scr, *, steps, rows):
    c_idx = pl.program_id(1)

    @pl.when(c_idx == 0)
    def _():
        sre_scr[...] = x0re_ref[...]
        sim_scr[...] = x0im_ref[...]

    a_re, a_im = are_ref[...], aim_ref[...]
    dt = jnp.exp(ldt_ref[...])
    mag = jnp.exp(dt * a_re)
    ab_re = mag * jnp.cos(dt * a_im)
    ab_im = mag * jnp.sin(dt * a_im)
    den = a_re * a_re + a_im * a_im
    nr = ab_re - 1.0
    cf_re = (nr * a_re + ab_im * a_im) / den
    cf_im = (ab_im * a_re - nr * a_im) / den

    u = u_ref[...].reshape(steps * rows, GROUP_WIDTH)
    ub = u.astype(BF16)
    bu_re = _mm(ub, bre_ref[...])
    bu_im = _mm(ub, bim_ref[...])
    hre_scr[...] = cf_re * bu_re - cf_im * bu_im
    him_scr[...] = cf_re * bu_im + cf_im * bu_re

    abr = jnp.broadcast_to(ab_re, (rows, S5_LANES))
    abi = jnp.broadcast_to(ab_im, (rows, S5_LANES))

    def step(t, carry):
        h_re, h_im = carry
        base = pl.multiple_of(t * rows, rows)
        n_re = abr * h_re - abi * h_im + hre_scr[pl.ds(base, rows), :]
        n_im = abr * h_im + abi * h_re + him_scr[pl.ds(base, rows), :]
        hre_scr[pl.ds(base, rows), :] = n_re
        him_scr[pl.ds(base, rows), :] = n_im
        return n_re, n_im

    h_re, h_im = lax.fori_loop(0, steps, step, (sre_scr[...], sim_scr[...]))
    sre_scr[...] = h_re
    sim_scr[...] = h_im

    y = (_mm(hre_scr[...].astype(BF16), cre_ref[...]) - _mm(him_scr[...].astype(BF16), cim_ref[...])
         + d_ref[...] * u)
    zz = _gelu_tanh(y)
    out = zz * jax.nn.sigmoid(_mm(zz.astype(BF16), wglu_ref[...]) + bglu_ref[...])
    o_ref[...] = out.reshape(steps, rows, GROUP_WIDTH)

    @pl.when(c_idx == pl.num_programs(1) - 1)
    def _():
        hre_out_ref[...] = h_re
        him_out_ref[...] = h_im


def _s5(u_t, x0_re, x0_im, prm, steps, rows):
    seq, bsz, _ = u_t.shape
    st = pl.BlockSpec((rows, S5_LANES), lambda b, c: (b, 0))
    vec = _const_spec((1, S5_LANES))
    return pl.pallas_call(
        functools.partial(_s5_kernel, steps=steps, rows=rows),
        grid=(bsz // rows, seq // steps),
        in_specs=[pl.BlockSpec((steps, rows, GROUP_WIDTH), lambda b, c: (c, b, 0)), st, st, vec, vec, vec,
                  _const_spec((GROUP_WIDTH, S5_LANES)), _const_spec((GROUP_WIDTH, S5_LANES)),
                  _const_spec((S5_LANES, GROUP_WIDTH)), _const_spec((S5_LANES, GROUP_WIDTH)),
                  _const_spec((1, GROUP_WIDTH)), _const_spec((GROUP_WIDTH, GROUP_WIDTH)),
                  _const_spec((1, GROUP_WIDTH))],
        out_specs=[pl.BlockSpec((steps, rows, GROUP_WIDTH), lambda b, c: (c, b, 0)), st, st],
        out_shape=[jax.ShapeDtypeStruct((seq, bsz, GROUP_WIDTH), F32),
                   jax.ShapeDtypeStruct((bsz, S5_LANES), F32), jax.ShapeDtypeStruct((bsz, S5_LANES), F32)],
        scratch_shapes=[pltpu.VMEM((steps * rows, S5_LANES), F32), pltpu.VMEM((steps * rows, S5_LANES), F32),
                        pltpu.VMEM((rows, S5_LANES), F32), pltpu.VMEM((rows, S5_LANES), F32)],
        compiler_params=pltpu.CompilerParams(dimension_semantics=("arbitrary", "arbitrary"),
                                             vmem_limit_bytes=VMEM_LIMIT),
        name="s5",
    )(u_t, x0_re, x0_im, prm["a_re"], prm["a_im"], prm["log_dt"], prm["b_re"], prm["b_im"], prm["c_re"],
      prm["c_im"], prm["d"], prm["w_glu"], prm["b_glu"])


def _rope(x, cos, sin, width):
    fwd = pltpu.roll(x, width - ROPE_DIM // 2, 1)
    bwd = pltpu.roll(x, ROPE_DIM // 2, 1)
    lane = lax.broadcasted_iota(jnp.int32, x.shape, 1) % HEAD_DIM
    return x * cos + jnp.where(lane < ROPE_DIM // 2, fwd, bwd) * sin


def _swa_kernel(p_ref, cos_ref, sin_ref, kbuf_ref, vbuf_ref, sink_ref, o_ref, krot_ref, k_scr, v_scr,
                *, bb, qb, start, carry):
    j = pl.program_id(1)
    wb = WINDOW

    @pl.when(j == 0)
    def _():
        k_scr[:, 0:wb, :] = kbuf_ref[...]
        v_scr[:, 0:wb, :] = vbuf_ref[...]
        if qb < wb:
            k_scr[:, wb:2 * wb, :] = jnp.zeros((bb, wb, SWA_KV_WIDTH), F32)
            v_scr[:, wb:2 * wb, :] = jnp.zeros((bb, wb, SWA_KV_WIDTH), F32)

    p = p_ref[...].reshape(bb * qb, SWA_PROJ)
    cos, sin = cos_ref[...], sin_ref[...]
    if bb > 1:
        cos, sin = jnp.concatenate([cos] * bb, axis=0), jnp.concatenate([sin] * bb, axis=0)
    q = _rope(p[:, 0:GROUP_WIDTH], cos, sin, GROUP_WIDTH)
    k = _rope(p[:, GROUP_WIDTH:GROUP_WIDTH + SWA_KV_WIDTH], cos[:, 0:SWA_KV_WIDTH], sin[:, 0:SWA_KV_WIDTH],
              SWA_KV_WIDTH)
    v = p[:, GROUP_WIDTH + SWA_KV_WIDTH:SWA_PROJ]
    krot_ref[...] = k.reshape(bb, qb, SWA_KV_WIDTH)
    for bi in range(bb):
        k_scr[bi, wb:wb + qb, :] = k[bi * qb:(bi + 1) * qb]
        v_scr[bi, wb:wb + qb, :] = v[bi * qb:(bi + 1) * qb]

    nk = 2 * wb
    rq = lax.broadcasted_iota(jnp.int32, (2 * qb, nk), 0) % qb
    ck = lax.broadcasted_iota(jnp.int32, (2 * qb, nk), 1)
    rel = rq + wb - ck
    kpos = start + j * qb - wb + ck
    valid = (rel >= 0) & (rel < WINDOW) & (kpos >= 0)
    sink_col = ck == 0
    upper = lax.broadcasted_iota(jnp.int32, (2 * qb, nk), 0) < qb
    sinks = sink_ref[...]
    v_lane = lax.broadcasted_iota(jnp.int32, (nk, SWA_KV_WIDTH), 1)
    v_live = lax.broadcasted_iota(jnp.int32, (nk, SWA_KV_WIDTH), 0) > 0
    low, high = v_live & (v_lane < HEAD_DIM), v_live & (v_lane >= HEAD_DIM)
    ones = jnp.ones((nk, SWA_KV_WIDTH), BF16)

    probs = [(bi, h) for bi in range(bb) for h in range(SWA_KV_HEADS)]
    q2 = [jnp.concatenate([q[bi * qb:(bi + 1) * qb, (2 * h) * HEAD_DIM:(2 * h + 1) * HEAD_DIM],
                           q[bi * qb:(bi + 1) * qb, (2 * h + 1) * HEAD_DIM:(2 * h + 2) * HEAD_DIM]], axis=0)
          for bi, h in probs]
    kh = [k_scr[bi, :, h * HEAD_DIM:(h + 1) * HEAD_DIM] for bi, h in probs]
    vals = [v_scr[bi] for bi in range(bb)]
    swapped = [pltpu.roll(x, HEAD_DIM, 1) for x in vals]
    v_lo = [jnp.where(low, vals[bi] if h == 0 else swapped[bi], 0.0) for bi, h in probs]
    v_hi = [jnp.where(high, swapped[bi] if h == 0 else vals[bi], 0.0) for bi, h in probs]
    sink = [jnp.where(upper, sinks[:, 2 * h:2 * h + 1], sinks[:, 2 * h + 1:2 * h + 2]) for _, h in probs]
    sc = _each(lambda q_, k_, z_: jnp.where(sink_col, z_, jnp.where(valid, _mm_nt(q_, k_) * (HEAD_DIM ** -0.5),
                                                                     -jnp.inf)), q2, kh, sink)
    ex = _each(lambda s_: jnp.exp(s_ - jnp.max(s_, axis=-1, keepdims=True)).astype(BF16), sc)
    den = _each(lambda e_: _mm(e_, ones), ex)
    o_lo = _each(lambda e_, v_: _mm(e_[:qb], v_), ex, v_lo)
    o_hi = _each(lambda e_, v_: _mm(e_[qb:], v_), ex, v_hi)
    res = _each(lambda a_, b_, d_: a_ / d_[:qb] + b_ / d_[qb:], o_lo, o_hi, den)
    rows_out = [jnp.concatenate(res[bi * SWA_KV_HEADS:(bi + 1) * SWA_KV_HEADS], axis=1) for bi in range(bb)]
    o_ref[...] = (rows_out[0] if bb == 1 else jnp.concatenate(rows_out, axis=0)).reshape(bb, qb, GROUP_WIDTH)

    if carry:
        for bi in range(bb):
            k_scr[bi, 0:wb, :] = k[bi * qb:(bi + 1) * qb]
            v_scr[bi, 0:wb, :] = v[bi * qb:(bi + 1) * qb]


def _swa(p, cos, sin, kbuf, vbuf, sinks, bb, qb, start):
    bsz, seq, _ = p.shape
    nb = seq // qb
    assert nb == 1 or qb == WINDOW
    blk = lambda w: pl.BlockSpec((bb, qb, w), lambda b, j: (b, j, 0))
    tab = pl.BlockSpec((qb, GROUP_WIDTH), lambda b, j: (j, 0))
    buf = pl.BlockSpec((bb, WINDOW, SWA_KV_WIDTH), lambda b, j: (b, 0, 0))
    return pl.pallas_call(
        functools.partial(_swa_kernel, bb=bb, qb=qb, start=start, carry=nb > 1),
        grid=(bsz // bb, nb),
        in_specs=[blk(SWA_PROJ), tab, tab, buf, buf, _const_spec((1, LANES))],
        out_specs=[blk(GROUP_WIDTH), pl.BlockSpec((bb, qb, SWA_KV_WIDTH), lambda b, j: (b, 0, 0))],
        out_shape=[jax.ShapeDtypeStruct((bsz, seq, GROUP_WIDTH), F32),
                   jax.ShapeDtypeStruct((bsz, qb, SWA_KV_WIDTH), F32)],
        scratch_shapes=[pltpu.VMEM((bb, 2 * WINDOW, SWA_KV_WIDTH), F32),
                        pltpu.VMEM((bb, 2 * WINDOW, SWA_KV_WIDTH), F32)],
        compiler_params=pltpu.CompilerParams(dimension_semantics=("arbitrary", "arbitrary")),
        name="swa",
    )(p, cos, sin, kbuf, vbuf, sinks)


def _rope_tables(start, seq):
    half = ROPE_DIM // 2
    inv = ROPE_THETA ** (-jnp.arange(0, ROPE_DIM, 2, dtype=F32) / ROPE_DIM)
    ang = (start + jnp.arange(seq)).astype(F32)[:, None] * inv[None, :]
    cos, sin = jnp.cos(ang), jnp.sin(ang)
    rest = HEAD_DIM - ROPE_DIM
    cos_h = jnp.concatenate([cos, cos, jnp.ones((seq, rest), F32)], axis=1)
    sin_h = jnp.concatenate([-sin, sin, jnp.zeros((seq, rest), F32)], axis=1)
    return jnp.tile(cos_h, (1, N_HEADS)), jnp.tile(sin_h, (1, N_HEADS))


def _pad_rows(w, top, total):
    return jnp.pad(w, ((top, total - top - w.shape[0]), (0, 0)))


def _block_diag_in(b):
    eye = jnp.eye(S5_GROUPS, dtype=b.dtype)
    return jnp.einsum("gnc,gh->gchn", b, eye).reshape(S5_GROUPS * S5_CH, S5_GROUPS * S5_STATE)


def _block_diag_out(c):
    eye = jnp.eye(S5_GROUPS, dtype=c.dtype)
    return jnp.einsum("gcn,gh->gnhc", c, eye).reshape(S5_GROUPS * S5_STATE, S5_GROUPS * S5_CH)


def _layer_params(l, g_mix_pre, g_mix_post, g_mlp_pre, g_mlp_post, w_in, w_out, rw_mu, rw_w0, rw_w2, rw_a0, rw_a2,
                  rw_g2, rw_kk, rw_ka, rw_rk, rw_ln_w, rw_ln_b, s5_a_re, s5_a_im, s5_log_dt, s5_b_re, s5_b_im,
                  s5_c_re, s5_c_im, s5_d, s5_w_glu, s5_b_glu, gdn_conv_w, gdn_a_log, gdn_dt_bias, gdn_norm_w,
                  swa_sinks, w_up, w_down):
    row = lambda a: a[l].astype(F32).reshape(1, -1)
    wi = w_in[l].astype(F32)
    o_s5 = RW_PROJ
    o_gdn = o_s5 + GROUP_WIDTH
    o_ba = o_gdn + GDN_QKV
    o_z = o_ba + 2 * N_HEADS
    o_swa = o_z + GROUP_WIDTH
    w_all = jnp.concatenate([
        wi[:, :o_s5], wi[:, o_s5:o_gdn], wi[:, o_gdn:o_ba], wi[:, o_z:o_swa],
        jnp.pad(wi[:, o_ba:o_z], ((0, 0), (0, LANES - 2 * N_HEADS))), wi[:, o_swa:]], axis=1).astype(BF16)
    lane_pad = lambda a: jnp.pad(a[l].astype(F32), (N_HEADS, LANES - 2 * N_HEADS)).reshape(1, LANES)
    return {
        "g_mix_pre": row(g_mix_pre), "g_mix_post": row(g_mix_post), "g_mlp_pre": row(g_mlp_pre),
        "g_mlp_post": row(g_mlp_post), "w_all": w_all, "w_out": w_out[l].astype(BF16),
        "w_up": w_up[l].astype(BF16), "w_down": w_down[l].astype(BF16),
        "rw": {"mu": row(rw_mu), "w0": row(rw_w0), "a0": row(rw_a0), "kk": row(rw_kk), "ka": row(rw_ka),
               "rk": row(rw_rk), "ln_w": row(rw_ln_w), "ln_b": row(rw_ln_b),
               "w2": _pad_rows(rw_w2[l].astype(F32), 0, GROUP_WIDTH).astype(BF16),
               "a2": _pad_rows(rw_a2[l].astype(F32), RW_DECAY_RANK, GROUP_WIDTH).astype(BF16),
               "g2": _pad_rows(rw_g2[l].astype(F32), RW_DECAY_RANK + RW_ICLR_RANK, GROUP_WIDTH).astype(BF16)},
        "s5": {"a_re": row(s5_a_re), "a_im": row(s5_a_im),
               "log_dt": jnp.repeat(s5_log_dt[l].astype(F32), S5_STATE).reshape(1, S5_LANES),
               "b_re": _block_diag_in(s5_b_re[l].astype(F32)).astype(BF16),
               "b_im": _block_diag_in(s5_b_im[l].astype(F32)).astype(BF16),
               "c_re": _block_diag_out(s5_c_re[l].astype(F32)).astype(BF16),
               "c_im": _block_diag_out(s5_c_im[l].astype(F32)).astype(BF16),
               "d": row(s5_d), "w_glu": s5_w_glu[l].astype(BF16), "b_glu": row(s5_b_glu)},
        "gdn": {"conv_w": gdn_conv_w[l].astype(F32), "a_log": lane_pad(gdn_a_log), "dt_bias": lane_pad(gdn_dt_bias),
                "norm_w": jnp.tile(gdn_norm_w[l].astype(F32), N_HEADS).reshape(1, GROUP_WIDTH)},
        "sinks": jnp.pad(swa_sinks[l].astype(F32), (0, LANES - N_HEADS)).reshape(1, LANES),
    }


def _layer(x, start, st, layer, prm):
    rw_s, rw_shift, s5_re, s5_im, gdn_s, gdn_conv, swa_k, swa_v = st
    bsz, seq, _ = x.shape
    tokens = bsz * seq
    tm = min(TOKEN_ROWS, tokens)
    chunk = CHUNK if seq % CHUNK == 0 else -(-seq // SUBLANES) * SUBLANES
    seq_pad = -(-seq // chunk) * chunk
    n_valid = chunk - (seq_pad - seq)
    n_chunks = math.gcd(seq_pad // chunk, MIX_CHUNKS)
    bb = {name: math.gcd(bsz, pair[seq_pad == chunk]) for name, pair in MIXER_BATCH.items()}
    assert seq_pad == seq or seq_pad == chunk
    assert seq >= GDN_CONV - 1

    x2d = x.reshape(tokens, D_MODEL)
    p_rw, p_s5, p_qkv, p_z, p_ba, p_swa = (
        a.reshape(bsz, seq, -1) for a in _in_proj(x2d, prm["g_mix_pre"], prm["w_all"], tm))
    pad = lambda a: a if seq_pad == seq else jnp.pad(a, ((0, 0), (0, seq_pad - seq), (0, 0)))

    o_rw, rw_s_new = _rwkv(pad(p_rw), rw_shift, rw_s, layer, prm["rw"], bb["rwkv"], chunk, n_chunks, n_valid)
    rw_shift_new = p_rw[:, seq - 1:, :]

    rows = 16
    steps = min(seq, 64)
    o_s5_t, s5_re_new, s5_im_new = _s5(jnp.swapaxes(p_s5, 0, 1), s5_re.reshape(bsz, S5_LANES),
                                       s5_im.reshape(bsz, S5_LANES), prm["s5"], steps, rows)
    o_s5 = jnp.swapaxes(o_s5_t, 0, 1)

    o_gdn, gdn_s_new = _gdn(pad(p_qkv), pad(p_z), pad(p_ba), gdn_conv, gdn_s, layer, prm["gdn"],
                            bb["gdn"], chunk, n_chunks, n_valid)
    gdn_conv_new = p_qkv[:, seq - (GDN_CONV - 1):, :]

    qb = WINDOW if seq % WINDOW == 0 else seq_pad
    cos, sin = _rope_tables(start, seq_pad)
    wbuf = swa_k.shape[1]
    o_swa, k_rot = _swa(pad(p_swa), cos, sin, swa_k.reshape(bsz, wbuf, SWA_KV_WIDTH),
                        swa_v.reshape(bsz, wbuf, SWA_KV_WIDTH), prm["sinks"], bb["swa"], qb, start)
    k_new = k_rot[:, :qb - (seq_pad - seq), :] if seq_pad != seq else k_rot
    v_new = p_swa[:, seq - min(seq, wbuf):, GROUP_WIDTH + SWA_KV_WIDTH:]
    kv_shape = (bsz, -1, SWA_KV_HEADS, HEAD_DIM)
    if k_new.shape[1] >= wbuf:
        swa_k_new = k_new[:, -wbuf:].reshape(kv_shape)
        swa_v_new = v_new[:, -wbuf:].reshape(kv_shape)
    else:
        swa_k_new = jnp.concatenate([swa_k, k_new.reshape(kv_shape)], axis=1)[:, -wbuf:]
        swa_v_new = jnp.concatenate([swa_v, v_new.reshape(kv_shape)], axis=1)[:, -wbuf:]

    flat = lambda a: a[:, :seq, :].reshape(tokens, GROUP_WIDTH)
    y = _post(x2d, (flat(o_rw), flat(o_s5), flat(o_gdn), flat(o_swa)), prm["w_out"], prm["g_mix_post"],
              prm["g_mlp_pre"], prm["w_up"], prm["w_down"], prm["g_mlp_post"], tm)
    new_state = (rw_s_new, rw_shift_new, s5_re_new.reshape(s5_re.shape), s5_im_new.reshape(s5_im.shape),
                 gdn_s_new, gdn_conv_new, swa_k_new, swa_v_new)
    return y.reshape(bsz, seq, D_MODEL), new_state


def kernel(x_prompt, x_sample, state_rwkv, state_rwkv_shift, state_s5_re, state_s5_im, state_gdn, state_gdn_conv,
           cache_swa_k, cache_swa_v, g_mix_pre, g_mix_post, g_mlp_pre, g_mlp_post, w_in, w_out, rw_mu, rw_w0, rw_w2,
           rw_a0, rw_a2, rw_g2, rw_kk, rw_ka, rw_rk, rw_ln_w, rw_ln_b, s5_a_re, s5_a_im, s5_log_dt, s5_b_re, s5_b_im,
           s5_c_re, s5_c_im, s5_d, s5_w_glu, s5_b_glu, gdn_conv_w, gdn_a_log, gdn_dt_bias, gdn_norm_w, swa_sinks,
           w_up, w_down):
    weights = (g_mix_pre, g_mix_post, g_mlp_pre, g_mlp_post, w_in, w_out, rw_mu, rw_w0, rw_w2, rw_a0, rw_a2, rw_g2,
               rw_kk, rw_ka, rw_rk, rw_ln_w, rw_ln_b, s5_a_re, s5_a_im, s5_log_dt, s5_b_re, s5_b_im, s5_c_re, s5_c_im,
               s5_d, s5_w_glu, s5_b_glu, gdn_conv_w, gdn_a_log, gdn_dt_bias, gdn_norm_w, swa_sinks, w_up, w_down)
    depth = w_in.shape[0]
    bp = x_prompt.shape[0]
    past_len = 16384
    xp = x_prompt.astype(F32)
    xs = x_sample.astype(F32)
    zp = lambda *shape: jnp.zeros((bp,) + shape, F32)
    zero_s = jnp.zeros((1, bp, N_HEADS, HEAD_DIM, HEAD_DIM), F32)
    prompt_init = (zero_s, zp(1, RW_PROJ), zp(S5_GROUPS, S5_STATE), zp(S5_GROUPS, S5_STATE),
                   zero_s, zp(GDN_CONV - 1, GDN_QKV),
                   zp(WINDOW, SWA_KV_HEADS, HEAD_DIM), zp(WINDOW, SWA_KV_HEADS, HEAD_DIM))
    sample_states = (state_rwkv, state_rwkv_shift, state_s5_re, state_s5_im, state_gdn, state_gdn_conv,
                     cache_swa_k, cache_swa_v)
    new_p, new_s = [], []
    for l in range(depth):
        prm = _layer_params(l, *weights)
        xp, st_p = _layer(xp, 0, prompt_init, 0, prm)
        stacked = (0, 4)
        st_l = tuple(a.astype(F32) if i in stacked else a[l].astype(F32) for i, a in enumerate(sample_states))
        xs, st_s = _layer(xs, past_len, st_l, l, prm)
        new_p.append(st_p)
        new_s.append(st_s)
    outs = [xp.astype(x_prompt.dtype), xs.astype(x_sample.dtype)]
    for i in range(len(sample_states)):
        outs.append(jnp.stack([st[i] for st in new_p], axis=0))
        outs.append(jnp.stack([st[i] for st in new_s], axis=0))
    return tuple(outs)
```

```python
import functools
import math

import jax
import jax.numpy as jnp
from jax import lax
from jax.experimental import pallas as pl
from jax.experimental.pallas import tpu as pltpu

F32 = jnp.float32
BF16 = jnp.bfloat16

D_MODEL = 1024
HEAD_DIM = 64
N_HEADS = 4
GROUP_WIDTH = N_HEADS * HEAD_DIM
NORM_EPS = 1e-6
RW_LN_EPS = 64e-5
RW_PROJ = 1024
RW_LORA_OFF = 3 * GROUP_WIDTH
RW_DECAY_RANK, RW_ICLR_RANK, RW_GATE_RANK = 64, 64, 128
S5_GROUPS, S5_CH, S5_STATE = 16, 16, 64
S5_LANES = S5_GROUPS * S5_STATE
GDN_CONV = 4
GDN_QKV = 3 * GROUP_WIDTH
SWA_KV_HEADS = 2
SWA_KV_WIDTH = SWA_KV_HEADS * HEAD_DIM
SWA_PROJ = GROUP_WIDTH + 2 * SWA_KV_WIDTH
WINDOW = 128
PAST_LEN = 16384
ROPE_DIM = 16
ROPE_THETA = 500000.0
D_FF = 4096
LANES = 128
SUBLANES = 8
TOKEN_ROWS = 1024
CHUNK = 64
MIX_CHUNKS = 4
MIXER_BATCH = {"rwkv": (8, 16), "gdn": (8, 16), "swa": (4, 32)}
TRI_BLOCK = 16
VMEM_LIMIT = 56 * 1024 * 1024

PROJ_WIDTHS = (RW_PROJ, GROUP_WIDTH, GDN_QKV, GROUP_WIDTH, LANES, SWA_PROJ)


def _dot(a, b, dims):
    return lax.dot_general(a.astype(BF16), b.astype(BF16), (dims, ((), ())), preferred_element_type=F32)


def _mm(a, b):
    return _dot(a, b, ((1,), (0,)))


def _mm_nt(a, b):
    return _dot(a, b, ((1,), (1,)))


def _mm_tn(a, b):
    return _dot(a, b, ((0,), (0,)))


def _split2(x):
    hi = x.astype(BF16)
    lo = (x - hi.astype(F32)).astype(BF16)
    return hi, lo


def _mm_exact_rhs(x, m):
    hi, lo = _split2(x)
    return _mm(hi, m) + _mm(lo, m)


def _mm_exact_lhs(m, x):
    hi, lo = _split2(x)
    return _mm(m, hi) + _mm(m, lo)


def _head_ones(width):
    r = lax.broadcasted_iota(jnp.int32, (width, width), 0)
    c = lax.broadcasted_iota(jnp.int32, (width, width), 1)
    return (r // HEAD_DIM == c // HEAD_DIM).astype(BF16)


def _head_sum(x, ones):
    return _mm(x, ones)


def _tri_masks(n):
    r = lax.broadcasted_iota(jnp.int32, (n, n), 0)
    c = lax.broadcasted_iota(jnp.int32, (n, n), 1)
    return r, c, r > c, r >= c


def _softplus(x):
    return jnp.maximum(x, 0.0) + jnp.log(1.0 + jnp.exp(-jnp.abs(x)))


def _silu(x):
    return x * jax.nn.sigmoid(x)


def _rms(x, g):
    return x * lax.rsqrt(jnp.mean(x * x, axis=-1, keepdims=True) + NORM_EPS) * g


def _per_head(y, heads):
    if heads == 1:
        return y
    yb = y.astype(BF16)
    tall = jnp.concatenate([yb] * heads, axis=0)
    r = lax.broadcasted_iota(jnp.int32, tall.shape, 0) // y.shape[0]
    c = lax.broadcasted_iota(jnp.int32, tall.shape, 1) // (y.shape[1] // heads)
    return jnp.where(r == c, tall, jnp.zeros((), BF16))


def _each(fn, *lists):
    return [fn(*xs) for xs in zip(*lists)]


def _tri_inv(lows, n, paired=False):
    pair = paired or (2 * n == LANES and len(lows) % 2 == 0)
    if pair and not paired:
        lows = [jnp.concatenate(lows[j:j + 2], axis=1) for j in range(0, len(lows), 2)]
    width = 2 * n if pair else n
    r = lax.broadcasted_iota(jnp.int32, (n, width), 0)
    c = lax.broadcasted_iota(jnp.int32, (n, width), 1) % n
    eye = (r == c).astype(F32)
    mul = lambda x, y: _mm(x, _per_head(y, 2 if pair else 1))
    bs = min(TRI_BLOCK, n)
    if n > bs:
        same = (r // bs) == (c // bs)
        diags = [jnp.where(same, low, 0.0) for low in lows]
        rests = _each(lambda low, d: low - d, lows, diags)
    else:
        diags, rests = lows, None
    invs = [eye - d for d in diags]
    pws, p = _each(lambda x: mul(x, x), diags), 2
    while 2 * p < bs:
        both = _each(lambda x, i: mul(jnp.concatenate([x, i], axis=0), x), pws, invs)
        pws = [z[:n] for z in both]
        invs = _each(lambda i, z: i + z[n:], invs, both)
        p *= 2
    invs = _each(lambda i, x: i + mul(i, x), invs, pws)
    if rests is not None:
        nils = _each(mul, invs, rests)
        outs = [eye - x for x in nils]
        for _ in range(n // bs - 2):
            outs = _each(lambda x, o: eye - mul(x, o), nils, outs)
        invs = _each(mul, outs, invs)
    if pair and not paired:
        invs = [half for x in invs for half in (x[:, :n], x[:, n:])]
    return invs


def _head_groups(chunk):
    hp = 2 if chunk == HEAD_DIM else 1
    return hp, hp * HEAD_DIM, hp * chunk, N_HEADS // hp


def _own_blocks(x, hp):
    if hp == 1:
        return x
    r = lax.broadcasted_iota(jnp.int32, x.shape, 0) // HEAD_DIM
    c = lax.broadcasted_iota(jnp.int32, x.shape, 1) // HEAD_DIM
    return jnp.where(r == c, x, 0.0)


def _own_side(x, hp):
    if hp == 1:
        return x
    low = lax.broadcasted_iota(jnp.int32, (HEAD_DIM, x.shape[1]), 1) < HEAD_DIM
    return jnp.where(low, x[:HEAD_DIM], x[HEAD_DIM:])


def _carry_chunks(s_scr, bb, n_chunks, hp, out0, left, decay, pb, q_mat):
    n_grp = N_HEADS // hp
    group = bb * n_grp
    state = [s_scr[bi, g] if hp == 1 else jnp.concatenate([s_scr[bi, g * hp + i] for i in range(hp)], axis=1)
             for bi in range(bb) for g in range(n_grp)]
    out = {}
    for c in range(n_chunks):
        grp = slice(c * group, (c + 1) * group)
        outs = _each(lambda o_, l_, s_: o_ + _mm_nt(l_, _per_head(s_, hp)), out0[grp], left[grp], state)
        state = _each(lambda s_, d_, p_, q_: s_ * d_ - _mm(s_, p_) + q_, state, decay[grp], pb[grp], q_mat[grp])
        for bi in range(bb):
            out[bi, c] = jnp.concatenate(outs[bi * n_grp:(bi + 1) * n_grp], axis=1)
    for bi in range(bb):
        for g in range(n_grp):
            for i in range(hp):
                s_scr[bi, g * hp + i] = state[bi * n_grp + g][:, i * HEAD_DIM:(i + 1) * HEAD_DIM]
    pieces = [out[bi, c] for bi in range(bb) for c in range(n_chunks)]
    return pieces[0] if len(pieces) == 1 else jnp.concatenate(pieces, axis=0)


def _const_spec(shape):
    zeros = (0,) * len(shape)
    return pl.BlockSpec(shape, lambda *_: zeros, pipeline_mode=pl.Buffered(1))


def _in_proj_kernel(x_ref, g_ref, w_ref, *out_refs):
    h = _rms(x_ref[...], g_ref[...]).astype(BF16)
    off = 0
    for o_ref, width in zip(out_refs, PROJ_WIDTHS):
        o_ref[...] = _mm(h, w_ref[:, off:off + width])
        off += width


def _in_proj(x2d, g, w_all, tm):
    t = x2d.shape[0]
    wtot = sum(PROJ_WIDTHS)
    return pl.pallas_call(
        _in_proj_kernel,
        grid=(t // tm,),
        in_specs=[pl.BlockSpec((tm, D_MODEL), lambda i: (i, 0)),
                  _const_spec((1, D_MODEL)),
                  _const_spec((D_MODEL, wtot))],
        out_specs=[pl.BlockSpec((tm, w), lambda i: (i, 0)) for w in PROJ_WIDTHS],
        out_shape=[jax.ShapeDtypeStruct((t, w), F32) for w in PROJ_WIDTHS],
        compiler_params=pltpu.CompilerParams(dimension_semantics=("arbitrary",), vmem_limit_bytes=VMEM_LIMIT),
        name="in_proj",
    )(x2d, g, w_all)


FF_CHUNK = 1024


def _post_kernel(x_ref, o0_ref, o1_ref, o2_ref, o3_ref, wout_ref, gpost_ref, gpre_ref, wup_ref, wdn_ref,
                 gmlp_ref, y_ref):
    mix = None
    for i, o_ref in enumerate((o0_ref, o1_ref, o2_ref, o3_ref)):
        part = _mm(o_ref[...].astype(BF16), wout_ref[i * GROUP_WIDTH:(i + 1) * GROUP_WIDTH, :])
        mix = part if mix is None else mix + part
    x1 = x_ref[...] + _rms(mix, gpost_ref[...])
    h = _rms(x1, gpre_ref[...]).astype(BF16)
    acc = None
    for c in range(D_FF // FF_CHUNK):
        u = _mm(h, wup_ref[:, c * FF_CHUNK:(c + 1) * FF_CHUNK])
        u = jnp.square(jnp.maximum(u, 0.0)).astype(BF16)
        part = _mm(u, wdn_ref[c * FF_CHUNK:(c + 1) * FF_CHUNK, :])
        acc = part if acc is None else acc + part
    y_ref[...] = x1 + _rms(acc, gmlp_ref[...])


def _post(x2d, outs, w_out, g_post, g_pre, w_up, w_dn, g_mlp, tm):
    t = x2d.shape[0]
    row = lambda w: pl.BlockSpec((tm, w), lambda i: (i, 0))
    return pl.pallas_call(
        _post_kernel,
        grid=(t // tm,),
        in_specs=[row(D_MODEL)] + [row(GROUP_WIDTH)] * 4 + [
            _const_spec((D_MODEL, D_MODEL)), _const_spec((1, D_MODEL)), _const_spec((1, D_MODEL)),
            _const_spec((D_MODEL, D_FF)), _const_spec((D_FF, D_MODEL)), _const_spec((1, D_MODEL))],
        out_specs=row(D_MODEL),
        out_shape=jax.ShapeDtypeStruct((t, D_MODEL), F32),
        compiler_params=pltpu.CompilerParams(dimension_semantics=("arbitrary",), vmem_limit_bytes=VMEM_LIMIT),
        name="post_mlp",
    )(x2d, *outs, w_out, g_post, g_pre, w_up, w_dn, g_mlp)


def _chunk_sums(x, chunk, rows):
    _, _, _, incl = _tri_masks(chunk)
    tri = incl.astype(BF16)
    cums = [_mm_exact_lhs(tri, x[c * chunk:(c + 1) * chunk]) for c in range(rows // chunk)]
    if len(cums) == 1:
        return cums[0], cums[0][chunk - 1:chunk, :]
    tots = [jnp.broadcast_to(cu[chunk - 1:chunk, :], cu.shape) for cu in cums]
    return jnp.concatenate(cums, axis=0), jnp.concatenate(tots, axis=0)


def _rwkv_kernel(p_ref, shift_ref, s0_ref, mu_ref, w0_ref, a0_ref, kkp_ref, kap_ref, rk_ref, lnw_ref, lnb_ref,
                 w2_ref, a2_ref, g2_ref, o_ref, s_out_ref, s_scr, carry_scr, *, bb, chunk, n_chunks, n_valid):
    c_idx = pl.program_id(1)
    rows = chunk * n_chunks
    total = bb * rows

    @pl.when(c_idx == 0)
    def _():
        s_scr[...] = s0_ref[...]
        carry_scr[...] = shift_ref[...]

    first = lax.broadcasted_iota(jnp.int32, (rows, 1), 0) == 0
    last = rows - chunk + n_valid
    prevs = []
    for bi in range(bb):
        p_b = p_ref[bi]
        prevs.append(jnp.where(first, carry_scr[bi], pltpu.roll(p_b, 1, 0)))
        carry_scr[bi] = p_b[last - 1:last, :]
    p = p_ref[...].reshape(total, RW_PROJ)
    prev = prevs[0] if bb == 1 else jnp.concatenate(prevs, axis=0)
    row = lax.broadcasted_iota(jnp.int32, (total, 1), 0)
    xs = p + (prev - p) * mu_ref[...]
    r = xs[:, 0:GROUP_WIDTH]
    k = xs[:, GROUP_WIDTH:2 * GROUP_WIDTH]
    v = xs[:, 2 * GROUP_WIDTH:3 * GROUP_WIDTH]
    lora = xs[:, RW_LORA_OFF:RW_PROJ]

    z_w = w0_ref[...] + _mm(jnp.tanh(lora), w2_ref[...])
    logw = -math.exp(-0.5) * jax.nn.sigmoid(z_w)
    a = jax.nn.sigmoid(a0_ref[...] + _mm(lora, a2_ref[...]))
    gate = _mm(jax.nn.sigmoid(lora), g2_ref[...])

    ones = _head_ones(GROUP_WIDTH)
    kk = k * kkp_ref[...]
    kk = kk * lax.rsqrt(jnp.maximum(_head_sum(kk * kk, ones), 1e-12))
    k2 = k * (1.0 + (a - 1.0) * kap_ref[...])
    b = kk * a
    if n_valid < chunk:
        valid = (row % chunk) < n_valid
        logw = jnp.where(valid, logw, 0.0)
        kk = jnp.where(valid, kk, 0.0)
        b = jnp.where(valid, b, 0.0)
        k2v = jnp.where(valid, k2, 0.0)
        vv = jnp.where(valid, v, 0.0)
    else:
        k2v, vv = k2, v

    log_cum, log_end = _chunk_sums(logw, chunk, total)
    w_inv = jnp.exp(-log_cum)
    w_end = jnp.exp(log_end)
    w_tail = w_end * w_inv
    a_bar = kk * jnp.exp(log_cum - logw)
    b_bar = b * w_inv
    k_bar = k2v * w_inv
    r_bar = r * jnp.exp(log_cum)
    b_end = b * w_tail
    k_end = k2v * w_tail

    hp, wid, swid, n_grp = _head_groups(chunk)
    sr = lax.broadcasted_iota(jnp.int32, (chunk, swid), 0)
    sc = lax.broadcasted_iota(jnp.int32, (chunk, swid), 1) % chunk
    strict, incl = sr > sc, sr >= sc
    tiles = [(slice(bi * rows + c * chunk, bi * rows + (c + 1) * chunk), slice(g * wid, (g + 1) * wid))
             for c in range(n_chunks) for bi in range(bb) for g in range(n_grp)]
    cut = lambda x: [x[rs, sl] for rs, sl in tiles]
    ab, rb, bs, kb, be, ke, vh = map(cut, (a_bar, r_bar, b_bar, k_bar, b_end, k_end, vv))
    ar = _each(lambda a_, r_: jnp.concatenate([a_, r_], axis=0), ab, rb)
    gb = _each(lambda x, b_: _mm_nt(x, _per_head(b_, hp)), ar, bs)
    gk = _each(lambda x, k_: _mm_nt(x, _per_head(k_, hp)), ar, kb)
    l_ab = [jnp.where(strict, g[:chunk], 0.0) for g in gb]
    m_rb = [jnp.where(incl, g[chunk:], 0.0) for g in gb]
    l_ak = [jnp.where(strict, g[:chunk], 0.0) for g in gk]
    m_rk = [jnp.where(incl, g[chunk:], 0.0) for g in gk]
    tinv = _tri_inv(l_ab, chunk, paired=True) if hp == 2 else _tri_inv(l_ab, chunk)
    side = lambda x, y_: jnp.concatenate([_per_head(x, hp), _per_head(y_, hp)], axis=1)
    lv = _each(lambda l_, v_: _mm(l_, _per_head(v_, hp)), l_ak, vh)
    tt = _each(lambda t_, a_, l_: _mm(t_, side(a_, l_)), tinv, ab, lv)
    ta = [t_[:, :wid] for t_ in tt]
    c0 = [-t_[:, wid:] for t_ in tt]
    pq = _each(lambda a_, c_, b_: _mm_tn(jnp.concatenate([a_, c_], axis=1), b_), ta, c0, be)
    vk = _each(_mm_tn, vh, ke)
    ry = _each(lambda m_, a_, c_: _mm(m_, side(a_, c_)), m_rb, ta, c0)
    rkv = _each(lambda m_, v_: _mm(m_, _per_head(v_, hp)), m_rk, vh)
    pb = [_own_blocks(x[:wid], hp) for x in pq]
    q_mat = _each(lambda x, y_: _own_side(x[wid:] + y_, hp), pq, vk)
    rr = _each(lambda r_, x: r_ - x[:, :wid], rb, ry)
    y0 = _each(lambda x, y_: x[:, wid:] + y_, ry, rkv)
    we = [w_end[rs.start:rs.start + 1, sl] for rs, sl in tiles]
    y = _carry_chunks(s_scr, bb, n_chunks, hp, y0, rr, we, pb, q_mat)

    inv_d = 1.0 / HEAD_DIM
    mean = _head_sum(y, ones) * inv_d
    d = y - mean
    var = _head_sum(d * d, ones) * inv_d
    yn = d * lax.rsqrt(var + RW_LN_EPS) * lnw_ref[...] + lnb_ref[...]
    bonus = _head_sum(r * k2 * rk_ref[...], ones) * v
    o_ref[...] = ((yn + bonus) * gate).reshape(bb, rows, GROUP_WIDTH)

    @pl.when(c_idx == pl.num_programs(1) - 1)
    def _():
        s_out_ref[...] = s_scr[...]


def _state_specs(bb, layer):
    shape = (bb, N_HEADS, HEAD_DIM, HEAD_DIM)
    return (pl.BlockSpec((None,) + shape, lambda b, c: (layer, b, 0, 0, 0)),
            pl.BlockSpec(shape, lambda b, c: (b, 0, 0, 0)))


def _rwkv(p, shift, s0, layer, prm, bb, chunk, n_chunks, n_valid):
    bsz, seq, _ = p.shape
    rows = chunk * n_chunks
    assert n_valid == chunk or n_chunks == 1
    vec = _const_spec((1, GROUP_WIDTH))
    mat = _const_spec((GROUP_WIDTH, GROUP_WIDTH))
    state_in, state = _state_specs(bb, layer)
    return pl.pallas_call(
        functools.partial(_rwkv_kernel, bb=bb, chunk=chunk, n_chunks=n_chunks, n_valid=n_valid),
        grid=(bsz // bb, seq // rows),
        in_specs=[pl.BlockSpec((bb, rows, RW_PROJ), lambda b, c: (b, c, 0)),
                  pl.BlockSpec((bb, 1, RW_PROJ), lambda b, c: (b, 0, 0)),
                  state_in, _const_spec((1, RW_PROJ))] + [vec] * 7 + [mat] * 3,
        out_specs=[pl.BlockSpec((bb, rows, GROUP_WIDTH), lambda b, c: (b, c, 0)), state],
        out_shape=[jax.ShapeDtypeStruct((bsz, seq, GROUP_WIDTH), F32),
                   jax.ShapeDtypeStruct(s0.shape[1:], F32)],
        scratch_shapes=[pltpu.VMEM((bb, N_HEADS, HEAD_DIM, HEAD_DIM), F32), pltpu.VMEM((bb, 1, RW_PROJ), F32)],
        compiler_params=pltpu.CompilerParams(dimension_semantics=("arbitrary", "arbitrary")),
        name="rwkv7",
    )(p, shift, s0, prm["mu"], prm["w0"], prm["a0"], prm["kk"], prm["ka"], prm["rk"], prm["ln_w"], prm["ln_b"],
      prm["w2"], prm["a2"], prm["g2"])


def _gdn_kernel(qkv_ref, z_ref, ba_ref, buf_ref, s0_ref, convw_ref, alog_ref, dtb_ref, nw_ref,
                o_ref, s_out_ref, s_scr, x_scr, *, bb, chunk, n_chunks, n_valid):
    c_idx = pl.program_id(1)
    rows = chunk * n_chunks
    total = bb * rows
    pad = SUBLANES

    @pl.when(c_idx == 0)
    def _():
        s_scr[...] = s0_ref[...]
        x_scr[:, 0:pad, :] = jnp.zeros((bb, pad, GDN_QKV), F32)
        x_scr[:, pad - (GDN_CONV - 1):pad, :] = buf_ref[...]

    last = rows - chunk + n_valid
    convs = []
    for bi in range(bb):
        x_scr[bi, pad:pad + rows, :] = qkv_ref[bi]
        conv = None
        for j in range(GDN_CONV):
            term = x_scr[bi, pad - j:pad - j + rows, :] * convw_ref[GDN_CONV - 1 - j:GDN_CONV - j, :]
            conv = term if conv is None else conv + term
        x_scr[bi, 0:pad, :] = x_scr[bi, last:last + pad, :]
        convs.append(conv)
    act = _silu(convs[0] if bb == 1 else jnp.concatenate(convs, axis=0))
    q = act[:, 0:GROUP_WIDTH]
    k = act[:, GROUP_WIDTH:2 * GROUP_WIDTH]
    v = act[:, 2 * GROUP_WIDTH:3 * GROUP_WIDTH]
    ones = _head_ones(GROUP_WIDTH)
    q = q * lax.rsqrt(jnp.maximum(_head_sum(q * q, ones), 1e-12)) * (HEAD_DIM ** -0.5)
    k = k * lax.rsqrt(jnp.maximum(_head_sum(k * k, ones), 1e-12))

    ba = ba_ref[...].reshape(total, LANES)
    row = lax.broadcasted_iota(jnp.int32, (total, LANES), 0)
    beta = jax.nn.sigmoid(ba)
    glog = -jnp.exp(alog_ref[...]) * _softplus(ba + dtb_ref[...])
    if n_valid < chunk:
        valid = (row % chunk) < n_valid
        beta = jnp.where(valid, beta, 0.0)
        glog = jnp.where(valid, glog, 0.0)
    gcum, gend = _chunk_sums(glog, chunk, total)
    gcum_t = gcum.T
    er = lax.broadcasted_iota(jnp.int32, (LANES, GROUP_WIDTH), 0)
    ec = lax.broadcasted_iota(jnp.int32, (LANES, GROUP_WIDTH), 1) // HEAD_DIM
    beta_w = _mm_exact_rhs(beta, (er == ec).astype(BF16))
    spread = (er == ec + N_HEADS).astype(BF16)
    gcum_w = _mm_exact_rhs(gcum, spread)
    gend_w = _mm_exact_rhs(gend, spread)
    gam_w = jnp.exp(gcum_w)
    bgk = beta_w * gam_w * k
    bv = beta_w * v
    gq = gam_w * q
    kt = k * jnp.exp(gend_w - gcum_w)
    end_w = jnp.exp(gend_w)

    hp, wid, swid, n_grp = _head_groups(chunk)
    sr = lax.broadcasted_iota(jnp.int32, (chunk, swid), 0)
    sc = lax.broadcasted_iota(jnp.int32, (chunk, swid), 1) % chunk
    strict, incl = sr > sc, sr >= sc
    tiles = [(slice(bi * rows + c * chunk, bi * rows + (c + 1) * chunk), g)
             for c in range(n_chunks) for bi in range(bb) for g in range(n_grp)]
    lanes = lambda x: [x[rs, g * wid:(g + 1) * wid] for rs, g in tiles]
    slab = lambda x: [jnp.concatenate([x[rs, h * HEAD_DIM:h * HEAD_DIM + chunk] for h in range(g * hp, (g + 1) * hp)],
                                      axis=1) if hp * chunk != wid else x[rs, g * wid:(g + 1) * wid]
                      for rs, g in tiles]
    k_g, q_g, bgk_g, bv_g, gq_g, kt_g = map(lanes, (k, q, bgk, bv, gq, kt))
    g_row = [jnp.concatenate([gcum_t[N_HEADS + h:N_HEADS + h + 1, rs] for h in range(g * hp, (g + 1) * hp)], axis=1)
             for rs, g in tiles]
    decay = _each(lambda gc, gr: jnp.where(incl, jnp.exp(jnp.where(incl, gc - gr, 0.0)), 0.0), slab(gcum_w), g_row)
    gram = _each(lambda k_, q_: _mm_nt(jnp.concatenate([k_, q_], axis=0), _per_head(k_, hp)), k_g, q_g)
    a_mat = _each(lambda b_, d_, g_: jnp.where(strict, b_ * d_ * g_[:chunk], 0.0), slab(beta_w), decay, gram)
    tinv = _tri_inv(a_mat, chunk, paired=True) if hp == 2 else _tri_inv(a_mat, chunk)
    side = lambda x, y_: jnp.concatenate([_per_head(x, hp), _per_head(y_, hp)], axis=1)
    tt = _each(lambda t_, x, y_: _mm(t_, side(x, y_)), tinv, bgk_g, bv_g)
    qo = _each(lambda g_, d_, t_: _mm(g_[chunk:] * d_, side(t_[:, :wid], t_[:, wid:])), gram, decay, tt)
    pq = _each(_mm_tn, tt, kt_g)
    qq = _each(lambda g_, x: g_ - x[:, :wid], gq_g, qo)
    o0 = [x[:, wid:] for x in qo]
    pb = [_own_blocks(x[:wid], hp) for x in pq]
    q_mat = [_own_side(x[wid:], hp) for x in pq]
    ge = [end_w[rs.start:rs.start + 1, g * wid:(g + 1) * wid] for rs, g in tiles]
    o = _carry_chunks(s_scr, bb, n_chunks, hp, o0, qq, ge, pb, q_mat)
    ms = _head_sum(o * o, ones) * (1.0 / HEAD_DIM)
    zz = z_ref[...].reshape(total, GROUP_WIDTH)
    o_ref[...] = (o * lax.rsqrt(ms + NORM_EPS) * nw_ref[...] * _silu(zz)).reshape(bb, rows, GROUP_WIDTH)

    @pl.when(c_idx == pl.num_programs(1) - 1)
    def _():
        s_out_ref[...] = s_scr[...]


def _gdn(qkv, z, ba, buf, s0, layer, prm, bb, chunk, n_chunks, n_valid):
    bsz, seq, _ = qkv.shape
    rows = chunk * n_chunks
    assert n_valid == chunk or n_chunks == 1
    state_in, state = _state_specs(bb, layer)
    blk = lambda w: pl.BlockSpec((bb, rows, w), lambda b, c: (b, c, 0))
    return pl.pallas_call(
        functools.partial(_gdn_kernel, bb=bb, chunk=chunk, n_chunks=n_chunks, n_valid=n_valid),
        grid=(bsz // bb, seq // rows),
        in_specs=[blk(GDN_QKV), blk(GROUP_WIDTH), blk(LANES),
                  pl.BlockSpec((bb, GDN_CONV - 1, GDN_QKV), lambda b, c: (b, 0, 0)), state_in,
                  _const_spec((GDN_CONV, GDN_QKV)), _const_spec((1, LANES)), _const_spec((1, LANES)),
                  _const_spec((1, GROUP_WIDTH))],
        out_specs=[blk(GROUP_WIDTH), state],
        out_shape=[jax.ShapeDtypeStruct((bsz, seq, GROUP_WIDTH), F32), jax.ShapeDtypeStruct(s0.shape[1:], F32)],
        scratch_shapes=[pltpu.VMEM((bb, N_HEADS, HEAD_DIM, HEAD_DIM), F32),
                        pltpu.VMEM((bb, SUBLANES + rows, GDN_QKV), F32)],
        compiler_params=pltpu.CompilerParams(dimension_semantics=("arbitrary", "arbitrary")),
        name="gdn",
    )(qkv, z, ba, buf, s0, prm["conv_w"], prm["a_log"], prm["dt_bias"], prm["norm_w"])


def _gelu_tanh(x):
    return 0.5 * x * (1.0 + jnp.tanh(math.sqrt(2.0 / math.pi) * (x + 0.044715 * (x * x * x))))


def _s5_kernel(u_ref, x0re_ref, x0im_ref, are_ref, aim_ref, ldt_ref, bre_ref, bim_ref, cre_ref, cim_ref, d_ref,
               wglu_ref, bglu_ref, o_ref, hre_out_ref, him_out_ref, hre_scr, him_scr, sre_scr, sim_scr, *, steps, rows):
    c_idx = pl.program_id(1)

    @pl.when(c_idx == 0)
    def _():
        sre_scr[...] = x0re_ref[...]
        sim_scr[...] = x0im_ref[...]

    a_re, a_im = are_ref[...], aim_ref[...]
    dt = jnp.exp(ldt_ref[...])
    mag = jnp.exp(dt * a_re)
    ab_re = mag * jnp.cos(dt * a_im)
    ab_im = mag * jnp.sin(dt * a_im)
    den = a_re * a_re + a_im * a_im
    nr = ab_re - 1.0
    cf_re = (nr * a_re + ab_im * a_im) / den
    cf_im = (ab_im * a_re - nr * a_im) / den

    u = jnp.swapaxes(u_ref[...], 0, 1).reshape(steps * rows, GROUP_WIDTH)
    ub = u.astype(BF16)
    bu_re = _mm(ub, bre_ref[...])
    bu_im = _mm(ub, bim_ref[...])
    hre_scr[...] = cf_re * bu_re - cf_im * bu_im
    him_scr[...] = cf_re * bu_im + cf_im * bu_re

    abr = jnp.broadcast_to(ab_re, (rows, S5_LANES))
    abi = jnp.broadcast_to(ab_im, (rows, S5_LANES))

    def step(t, carry):
        h_re, h_im = carry
        base = pl.multiple_of(t * rows, rows)
        n_re = abr * h_re - abi * h_im + hre_scr[pl.ds(base, rows), :]
        n_im = abr * h_im + abi * h_re + him_scr[pl.ds(base, rows), :]
        hre_scr[pl.ds(base, rows), :] = n_re
        him_scr[pl.ds(base, rows), :] = n_im
        return n_re, n_im

    h_re, h_im = lax.fori_loop(0, steps, step, (sre_scr[...], sim_scr[...]))
    sre_scr[...] = h_re
    sim_scr[...] = h_im

    y = (_mm(hre_scr[...].astype(BF16), cre_ref[...]) - _mm(him_scr[...].astype(BF16), cim_ref[...])
         + d_ref[...] * u)
    zz = _gelu_tanh(y)
    out = zz * jax.nn.sigmoid(_mm(zz.astype(BF16), wglu_ref[...]) + bglu_ref[...])
    o_ref[...] = jnp.swapaxes(out.reshape(steps, rows, GROUP_WIDTH), 0, 1)

    @pl.when(c_idx == pl.num_programs(1) - 1)
    def _():
        hre_out_ref[...] = h_re
        him_out_ref[...] = h_im


def _s5(u, x0_re, x0_im, prm, steps, rows):
    bsz, seq, _ = u.shape
    st = pl.BlockSpec((rows, S5_LANES), lambda b, c: (b, 0))
    vec = _const_spec((1, S5_LANES))
    return pl.pallas_call(
        functools.partial(_s5_kernel, steps=steps, rows=rows),
        grid=(bsz // rows, seq // steps),
        in_specs=[pl.BlockSpec((rows, steps, GROUP_WIDTH), lambda b, c: (b, c, 0)), st, st, vec, vec, vec,
                  _const_spec((GROUP_WIDTH, S5_LANES)), _const_spec((GROUP_WIDTH, S5_LANES)),
                  _const_spec((S5_LANES, GROUP_WIDTH)), _const_spec((S5_LANES, GROUP_WIDTH)),
                  _const_spec((1, GROUP_WIDTH)), _const_spec((GROUP_WIDTH, GROUP_WIDTH)),
                  _const_spec((1, GROUP_WIDTH))],
        out_specs=[pl.BlockSpec((rows, steps, GROUP_WIDTH), lambda b, c: (b, c, 0)), st, st],
        out_shape=[jax.ShapeDtypeStruct((bsz, seq, GROUP_WIDTH), F32),
                   jax.ShapeDtypeStruct((bsz, S5_LANES), F32), jax.ShapeDtypeStruct((bsz, S5_LANES), F32)],
        scratch_shapes=[pltpu.VMEM((steps * rows, S5_LANES), F32), pltpu.VMEM((steps * rows, S5_LANES), F32),
                        pltpu.VMEM((rows, S5_LANES), F32), pltpu.VMEM((rows, S5_LANES), F32)],
        compiler_params=pltpu.CompilerParams(dimension_semantics=("arbitrary", "arbitrary"),
                                             vmem_limit_bytes=VMEM_LIMIT),
        name="s5",
    )(u, x0_re, x0_im, prm["a_re"], prm["a_im"], prm["log_dt"], prm["b_re"], prm["b_im"], prm["c_re"],
      prm["c_im"], prm["d"], prm["w_glu"], prm["b_glu"])


def _rope(x, cos, sin, width):
    fwd = pltpu.roll(x, width - ROPE_DIM // 2, 1)
    bwd = pltpu.roll(x, ROPE_DIM // 2, 1)
    lane = lax.broadcasted_iota(jnp.int32, x.shape, 1) % HEAD_DIM
    return x * cos + jnp.where(lane < ROPE_DIM // 2, fwd, bwd) * sin


def _swa_kernel(p_ref, cos_ref, sin_ref, kbuf_ref, vbuf_ref, sink_ref, o_ref, krot_ref, k_scr, v_scr,
                *, bb, qb, start, carry):
    j = pl.program_id(1)
    wb = WINDOW

    @pl.when(j == 0)
    def _():
        k_scr[:, 0:wb, :] = kbuf_ref[...]
        v_scr[:, 0:wb, :] = vbuf_ref[...]
        if qb < wb:
            k_scr[:, wb:2 * wb, :] = jnp.zeros((bb, wb, SWA_KV_WIDTH), F32)
            v_scr[:, wb:2 * wb, :] = jnp.zeros((bb, wb, SWA_KV_WIDTH), F32)

    p = p_ref[...].reshape(bb * qb, SWA_PROJ)
    cos, sin = cos_ref[...], sin_ref[...]
    if bb > 1:
        cos, sin = jnp.concatenate([cos] * bb, axis=0), jnp.concatenate([sin] * bb, axis=0)
    q = _rope(p[:, 0:GROUP_WIDTH], cos, sin, GROUP_WIDTH)
    k = _rope(p[:, GROUP_WIDTH:GROUP_WIDTH + SWA_KV_WIDTH], cos[:, 0:SWA_KV_WIDTH], sin[:, 0:SWA_KV_WIDTH],
              SWA_KV_WIDTH)
    v = p[:, GROUP_WIDTH + SWA_KV_WIDTH:SWA_PROJ]
    krot_ref[...] = k.reshape(bb, qb, SWA_KV_WIDTH)
    for bi in range(bb):
        k_scr[bi, wb:wb + qb, :] = k[bi * qb:(bi + 1) * qb]
        v_scr[bi, wb:wb + qb, :] = v[bi * qb:(bi + 1) * qb]

    nk = 2 * wb
    rq = lax.broadcasted_iota(jnp.int32, (2 * qb, nk), 0) % qb
    ck = lax.broadcasted_iota(jnp.int32, (2 * qb, nk), 1)
    rel = rq + wb - ck
    kpos = start + j * qb - wb + ck
    valid = (rel >= 0) & (rel < WINDOW) & (kpos >= 0)
    sink_col = ck == 0
    upper = lax.broadcasted_iota(jnp.int32, (2 * qb, nk), 0) < qb
    sinks = sink_ref[...]
    v_lane = lax.broadcasted_iota(jnp.int32, (nk, SWA_KV_WIDTH), 1)
    v_live = lax.broadcasted_iota(jnp.int32, (nk, SWA_KV_WIDTH), 0) > 0
    low, high = v_live & (v_lane < HEAD_DIM), v_live & (v_lane >= HEAD_DIM)
    ones = jnp.ones((nk, SWA_KV_WIDTH), BF16)

    probs = [(bi, h) for bi in range(bb) for h in range(SWA_KV_HEADS)]
    q2 = [jnp.concatenate([q[bi * qb:(bi + 1) * qb, (2 * h) * HEAD_DIM:(2 * h + 1) * HEAD_DIM],
                           q[bi * qb:(bi + 1) * qb, (2 * h + 1) * HEAD_DIM:(2 * h + 2) * HEAD_DIM]], axis=0)
          for bi, h in probs]
    kh = [k_scr[bi, :, h * HEAD_DIM:(h + 1) * HEAD_DIM] for bi, h in probs]
    vals = [v_scr[bi] for bi in range(bb)]
    swapped = [pltpu.roll(x, HEAD_DIM, 1) for x in vals]
    v_lo = [jnp.where(low, vals[bi] if h == 0 else swapped[bi], 0.0) for bi, h in probs]
    v_hi = [jnp.where(high, swapped[bi] if h == 0 else vals[bi], 0.0) for bi, h in probs]
    sink = [jnp.where(upper, sinks[:, 2 * h:2 * h + 1], sinks[:, 2 * h + 1:2 * h + 2]) for _, h in probs]
    sc = _each(lambda q_, k_, z_: jnp.where(sink_col, z_, jnp.where(valid, _mm_nt(q_, k_) * (HEAD_DIM ** -0.5),
                                                                     -jnp.inf)), q2, kh, sink)
    ex = _each(lambda s_: jnp.exp(s_ - jnp.max(s_, axis=-1, keepdims=True)).astype(BF16), sc)
    den = _each(lambda e_: _mm(e_, ones), ex)
    o_lo = _each(lambda e_, v_: _mm(e_[:qb], v_), ex, v_lo)
    o_hi = _each(lambda e_, v_: _mm(e_[qb:], v_), ex, v_hi)
    res = _each(lambda a_, b_, d_: a_ / d_[:qb] + b_ / d_[qb:], o_lo, o_hi, den)
    rows_out = [jnp.concatenate(res[bi * SWA_KV_HEADS:(bi + 1) * SWA_KV_HEADS], axis=1) for bi in range(bb)]
    o_ref[...] = (rows_out[0] if bb == 1 else jnp.concatenate(rows_out, axis=0)).reshape(bb, qb, GROUP_WIDTH)

    if carry:
        for bi in range(bb):
            k_scr[bi, 0:wb, :] = k[bi * qb:(bi + 1) * qb]
            v_scr[bi, 0:wb, :] = v[bi * qb:(bi + 1) * qb]


def _swa(p, cos, sin, kbuf, vbuf, sinks, bb, qb, start):
    bsz, seq, _ = p.shape
    nb = seq // qb
    assert nb == 1 or qb == WINDOW
    blk = lambda w: pl.BlockSpec((bb, qb, w), lambda b, j: (b, j, 0))
    tab = pl.BlockSpec((qb, GROUP_WIDTH), lambda b, j: (j, 0))
    buf = pl.BlockSpec((bb, WINDOW, SWA_KV_WIDTH), lambda b, j: (b, 0, 0))
    return pl.pallas_call(
        functools.partial(_swa_kernel, bb=bb, qb=qb, start=start, carry=nb > 1),
        grid=(bsz // bb, nb),
        in_specs=[blk(SWA_PROJ), tab, tab, buf, buf, _const_spec((1, LANES))],
        out_specs=[blk(GROUP_WIDTH), pl.BlockSpec((bb, qb, SWA_KV_WIDTH), lambda b, j: (b, 0, 0))],
        out_shape=[jax.ShapeDtypeStruct((bsz, seq, GROUP_WIDTH), F32),
                   jax.ShapeDtypeStruct((bsz, qb, SWA_KV_WIDTH), F32)],
        scratch_shapes=[pltpu.VMEM((bb, 2 * WINDOW, SWA_KV_WIDTH), F32),
                        pltpu.VMEM((bb, 2 * WINDOW, SWA_KV_WIDTH), F32)],
        compiler_params=pltpu.CompilerParams(dimension_semantics=("arbitrary", "arbitrary")),
        name="swa",
    )(p, cos, sin, kbuf, vbuf, sinks)


def _rope_tables(start, seq):
    half = ROPE_DIM // 2
    inv = ROPE_THETA ** (-jnp.arange(0, ROPE_DIM, 2, dtype=F32) / ROPE_DIM)
    ang = (start + jnp.arange(seq)).astype(F32)[:, None] * inv[None, :]
    cos, sin = jnp.cos(ang), jnp.sin(ang)
    rest = HEAD_DIM - ROPE_DIM
    cos_h = jnp.concatenate([cos, cos, jnp.ones((seq, rest), F32)], axis=1)
    sin_h = jnp.concatenate([-sin, sin, jnp.zeros((seq, rest), F32)], axis=1)
    return jnp.tile(cos_h, (1, N_HEADS)), jnp.tile(sin_h, (1, N_HEADS))


def _pad_rows(w, top, total):
    return jnp.pad(w, ((top, total - top - w.shape[0]), (0, 0)))


def _block_diag_in(b):
    eye = jnp.eye(S5_GROUPS, dtype=b.dtype)
    return jnp.einsum("gnc,gh->gchn", b, eye).reshape(S5_GROUPS * S5_CH, S5_GROUPS * S5_STATE)


def _block_diag_out(c):
    eye = jnp.eye(S5_GROUPS, dtype=c.dtype)
    return jnp.einsum("gcn,gh->gnhc", c, eye).reshape(S5_GROUPS * S5_STATE, S5_GROUPS * S5_CH)


def _layer_params(l, g_mix_pre, g_mix_post, g_mlp_pre, g_mlp_post, w_in, w_out, rw_mu, rw_w0, rw_w2, rw_a0, rw_a2,
                  rw_g2, rw_kk, rw_ka, rw_rk, rw_ln_w, rw_ln_b, s5_a_re, s5_a_im, s5_log_dt, s5_b_re, s5_b_im,
                  s5_c_re, s5_c_im, s5_d, s5_w_glu, s5_b_glu, gdn_conv_w, gdn_a_log, gdn_dt_bias, gdn_norm_w,
                  swa_sinks, w_up, w_down):
    row = lambda a: a[l].astype(F32).reshape(1, -1)
    wi = w_in[l].astype(F32)
    o_s5 = RW_PROJ
    o_gdn = o_s5 + GROUP_WIDTH
    o_ba = o_gdn + GDN_QKV
    o_z = o_ba + 2 * N_HEADS
    o_swa = o_z + GROUP_WIDTH
    w_all = jnp.concatenate([
        wi[:, :o_s5], wi[:, o_s5:o_gdn], wi[:, o_gdn:o_ba], wi[:, o_z:o_swa],
        jnp.pad(wi[:, o_ba:o_z], ((0, 0), (0, LANES - 2 * N_HEADS))), wi[:, o_swa:]], axis=1).astype(BF16)
    lane_pad = lambda a: jnp.pad(a[l].astype(F32), (N_HEADS, LANES - 2 * N_HEADS)).reshape(1, LANES)
    return {
        "g_mix_pre": row(g_mix_pre), "g_mix_post": row(g_mix_post), "g_mlp_pre": row(g_mlp_pre),
        "g_mlp_post": row(g_mlp_post), "w_all": w_all, "w_out": w_out[l].astype(BF16),
        "w_up": w_up[l].astype(BF16), "w_down": w_down[l].astype(BF16),
        "rw": {"mu": row(rw_mu), "w0": row(rw_w0), "a0": row(rw_a0), "kk": row(rw_kk), "ka": row(rw_ka),
               "rk": row(rw_rk), "ln_w": row(rw_ln_w), "ln_b": row(rw_ln_b),
               "w2": _pad_rows(rw_w2[l].astype(F32), 0, GROUP_WIDTH).astype(BF16),
               "a2": _pad_rows(rw_a2[l].astype(F32), RW_DECAY_RANK, GROUP_WIDTH).astype(BF16),
               "g2": _pad_rows(rw_g2[l].astype(F32), RW_DECAY_RANK + RW_ICLR_RANK, GROUP_WIDTH).astype(BF16)},
        "s5": {"a_re": row(s5_a_re), "a_im": row(s5_a_im),
               "log_dt": jnp.repeat(s5_log_dt[l].astype(F32), S5_STATE).reshape(1, S5_LANES),
               "b_re": _block_diag_in(s5_b_re[l].astype(F32)).astype(BF16),
               "b_im": _block_diag_in(s5_b_im[l].astype(F32)).astype(BF16),
               "c_re": _block_diag_out(s5_c_re[l].astype(F32)).astype(BF16),
               "c_im": _block_diag_out(s5_c_im[l].astype(F32)).astype(BF16),
               "d": row(s5_d), "w_glu": s5_w_glu[l].astype(BF16), "b_glu": row(s5_b_glu)},
        "gdn": {"conv_w": gdn_conv_w[l].astype(F32), "a_log": lane_pad(gdn_a_log), "dt_bias": lane_pad(gdn_dt_bias),
                "norm_w": jnp.tile(gdn_norm_w[l].astype(F32), N_HEADS).reshape(1, GROUP_WIDTH)},
        "sinks": jnp.pad(swa_sinks[l].astype(F32), (0, LANES - N_HEADS)).reshape(1, LANES),
    }


def _layer(x, start, st, layer, prm):
    rw_s, rw_shift, s5_re, s5_im, gdn_s, gdn_conv, swa_k, swa_v = st
    bsz, seq, _ = x.shape
    tokens = bsz * seq
    tm = min(TOKEN_ROWS, tokens)
    chunk = CHUNK if seq % CHUNK == 0 else -(-seq // SUBLANES) * SUBLANES
    seq_pad = -(-seq // chunk) * chunk
    n_valid = chunk - (seq_pad - seq)
    n_chunks = math.gcd(seq_pad // chunk, MIX_CHUNKS)
    bb = {name: math.gcd(bsz, pair[seq_pad == chunk]) for name, pair in MIXER_BATCH.items()}
    assert seq_pad == seq or seq_pad == chunk
    assert seq >= GDN_CONV - 1

    x2d = x.reshape(tokens, D_MODEL)
    p_rw, p_s5, p_qkv, p_z, p_ba, p_swa = (
        a.reshape(bsz, seq, -1) for a in _in_proj(x2d, prm["g_mix_pre"], prm["w_all"], tm))
    pad = lambda a: a if seq_pad == seq else jnp.pad(a, ((0, 0), (0, seq_pad - seq), (0, 0)))

    o_rw, rw_s_new = _rwkv(pad(p_rw), rw_shift, rw_s, layer, prm["rw"], bb["rwkv"], chunk, n_chunks, n_valid)
    rw_shift_new = p_rw[:, seq - 1:, :]

    rows = 16
    steps = min(seq, 64)
    o_s5, s5_re_new, s5_im_new = _s5(p_s5, s5_re.reshape(bsz, S5_LANES), s5_im.reshape(bsz, S5_LANES), prm["s5"],
                                     steps, rows)

    o_gdn, gdn_s_new = _gdn(pad(p_qkv), pad(p_z), pad(p_ba), gdn_conv, gdn_s, layer, prm["gdn"],
                            bb["gdn"], chunk, n_chunks, n_valid)
    gdn_conv_new = p_qkv[:, seq - (GDN_CONV - 1):, :]

    qb = WINDOW if seq % WINDOW == 0 else seq_pad
    cos, sin = _rope_tables(start, seq_pad)
    wbuf = swa_k.shape[1]
    o_swa, k_rot = _swa(pad(p_swa), cos, sin, swa_k.reshape(bsz, wbuf, SWA_KV_WIDTH),
                        swa_v.reshape(bsz, wbuf, SWA_KV_WIDTH), prm["sinks"], bb["swa"], qb, start)
    k_new = k_rot[:, :qb - (seq_pad - seq), :] if seq_pad != seq else k_rot
    v_new = p_swa[:, seq - min(seq, wbuf):, GROUP_WIDTH + SWA_KV_WIDTH:]
    kv_shape = (bsz, -1, SWA_KV_HEADS, HEAD_DIM)
    if k_new.shape[1] >= wbuf:
        swa_k_new = k_new[:, -wbuf:].reshape(kv_shape)
        swa_v_new = v_new[:, -wbuf:].reshape(kv_shape)
    else:
        swa_k_new = jnp.concatenate([swa_k, k_new.reshape(kv_shape)], axis=1)[:, -wbuf:]
        swa_v_new = jnp.concatenate([swa_v, v_new.reshape(kv_shape)], axis=1)[:, -wbuf:]

    flat = lambda a: a[:, :seq, :].reshape(tokens, GROUP_WIDTH)
    y = _post(x2d, (flat(o_rw), flat(o_s5), flat(o_gdn), flat(o_swa)), prm["w_out"], prm["g_mix_post"],
              prm["g_mlp_pre"], prm["w_up"], prm["w_down"], prm["g_mlp_post"], tm)
    new_state = (rw_s_new, rw_shift_new, s5_re_new.reshape(s5_re.shape), s5_im_new.reshape(s5_im.shape),
                 gdn_s_new, gdn_conv_new, swa_k_new, swa_v_new)
    return y.reshape(bsz, seq, D_MODEL), new_state


def kernel(x_prompt, x_sample, state_rwkv, state_rwkv_shift, state_s5_re, state_s5_im, state_gdn, state_gdn_conv,
           cache_swa_k, cache_swa_v, g_mix_pre, g_mix_post, g_mlp_pre, g_mlp_post, w_in, w_out, rw_mu, rw_w0, rw_w2,
           rw_a0, rw_a2, rw_g2, rw_kk, rw_ka, rw_rk, rw_ln_w, rw_ln_b, s5_a_re, s5_a_im, s5_log_dt, s5_b_re, s5_b_im,
           s5_c_re, s5_c_im, s5_d, s5_w_glu, s5_b_glu, gdn_conv_w, gdn_a_log, gdn_dt_bias, gdn_norm_w, swa_sinks,
           w_up, w_down):
    weights = (g_mix_pre, g_mix_post, g_mlp_pre, g_mlp_post, w_in, w_out, rw_mu, rw_w0, rw_w2, rw_a0, rw_a2, rw_g2,
               rw_kk, rw_ka, rw_rk, rw_ln_w, rw_ln_b, s5_a_re, s5_a_im, s5_log_dt, s5_b_re, s5_b_im, s5_c_re, s5_c_im,
               s5_d, s5_w_glu, s5_b_glu, gdn_conv_w, gdn_a_log, gdn_dt_bias, gdn_norm_w, swa_sinks, w_up, w_down)
    depth = w_in.shape[0]
    bp = x_prompt.shape[0]
    xp = x_prompt.astype(F32)
    xs = x_sample.astype(F32)
    zp = lambda *shape: jnp.zeros((bp,) + shape, F32)
    zero_s = jnp.zeros((1, bp, N_HEADS, HEAD_DIM, HEAD_DIM), F32)
    prompt_init = (zero_s, zp(1, RW_PROJ), zp(S5_GROUPS, S5_STATE), zp(S5_GROUPS, S5_STATE),
                   zero_s, zp(GDN_CONV - 1, GDN_QKV),
                   zp(WINDOW, SWA_KV_HEADS, HEAD_DIM), zp(WINDOW, SWA_KV_HEADS, HEAD_DIM))
    sample_states = (state_rwkv, state_rwkv_shift, state_s5_re, state_s5_im, state_gdn, state_gdn_conv,
                     cache_swa_k, cache_swa_v)
    new_p, new_s = [], []
    for l in range(depth):
        prm = _layer_params(l, *weights)
        xp, st_p = _layer(xp, 0, prompt_init, 0, prm)
        stacked = (0, 4)
        st_l = tuple(a.astype(F32) if i in stacked else a[l].astype(F32) for i, a in enumerate(sample_states))
        xs, st_s = _layer(xs, PAST_LEN, st_l, l, prm)
        new_p.append(st_p)
        new_s.append(st_s)
    outs = [xp.astype(x_prompt.dtype), xs.astype(x_sample.dtype)]
    for i in range(len(sample_states)):
        outs.append(jnp.stack([st[i] for st in new_p], axis=0))
        outs.append(jnp.stack([st[i] for st in new_s], axis=0))
    return tuple(outs)
```

```python
import functools
import math

import jax
import jax.numpy as jnp
from jax import lax
from jax.experimental import pallas as pl
from jax.experimental.pallas import tpu as pltpu

F32 = jnp.float32
BF16 = jnp.bfloat16

D_MODEL = 1024
HEAD_DIM = 64
N_HEADS = 4
GROUP_WIDTH = N_HEADS * HEAD_DIM
NORM_EPS = 1e-6
RW_LN_EPS = 64e-5
RW_PROJ = 1024
RW_LORA_OFF = 3 * GROUP_WIDTH
RW_DECAY_RANK, RW_ICLR_RANK, RW_GATE_RANK = 64, 64, 128
S5_GROUPS, S5_CH, S5_STATE = 16, 16, 64
S5_LANES = S5_GROUPS * S5_STATE
GDN_CONV = 4
GDN_QKV = 3 * GROUP_WIDTH
SWA_KV_HEADS = 2
SWA_KV_WIDTH = SWA_KV_HEADS * HEAD_DIM
SWA_PROJ = GROUP_WIDTH + 2 * SWA_KV_WIDTH
WINDOW = 128
PAST_LEN = 16384
ROPE_DIM = 16
ROPE_THETA = 500000.0
D_FF = 4096
LANES = 128
SUBLANES = 8
TOKEN_ROWS = 1024
CHUNK = 64
MIX_CHUNKS = 4
MIXER_BATCH = {"rwkv": (8, 16), "gdn": (8, 16), "swa": (4, 32)}
TRI_BLOCK = 16
VMEM_LIMIT = 56 * 1024 * 1024

PROJ_WIDTHS = (RW_PROJ, GROUP_WIDTH, GDN_QKV, GROUP_WIDTH, LANES, SWA_PROJ)


def _dot(a, b, dims):
    return lax.dot_general(a.astype(BF16), b.astype(BF16), (dims, ((), ())), preferred_element_type=F32)


def _mm(a, b):
    return _dot(a, b, ((1,), (0,)))


def _mm_nt(a, b):
    return _dot(a, b, ((1,), (1,)))


def _mm_tn(a, b):
    return _dot(a, b, ((0,), (0,)))


def _split2(x):
    hi = x.astype(BF16)
    lo = (x - hi.astype(F32)).astype(BF16)
    return hi, lo


def _mm_exact_rhs(x, m):
    hi, lo = _split2(x)
    return _mm(hi, m) + _mm(lo, m)


def _mm_exact_lhs(m, x):
    hi, lo = _split2(x)
    return _mm(m, hi) + _mm(m, lo)


def _head_ones(width):
    r = lax.broadcasted_iota(jnp.int32, (width, width), 0)
    c = lax.broadcasted_iota(jnp.int32, (width, width), 1)
    return (r // HEAD_DIM == c // HEAD_DIM).astype(BF16)


def _head_sum(x, ones):
    return _mm(x, ones)


def _tri_masks(n):
    r = lax.broadcasted_iota(jnp.int32, (n, n), 0)
    c = lax.broadcasted_iota(jnp.int32, (n, n), 1)
    return r, c, r > c, r >= c


def _softplus(x):
    return jnp.maximum(x, 0.0) + jnp.log(1.0 + jnp.exp(-jnp.abs(x)))


def _silu(x):
    return x * jax.nn.sigmoid(x)


def _rms(x, g):
    return x * lax.rsqrt(jnp.mean(x * x, axis=-1, keepdims=True) + NORM_EPS) * g


def _per_head(y, heads):
    if heads == 1:
        return y
    yb = y.astype(BF16)
    tall = jnp.concatenate([yb] * heads, axis=0)
    r = lax.broadcasted_iota(jnp.int32, tall.shape, 0) // y.shape[0]
    c = lax.broadcasted_iota(jnp.int32, tall.shape, 1) // (y.shape[1] // heads)
    return jnp.where(r == c, tall, jnp.zeros((), BF16))


def _each(fn, *lists):
    return [fn(*xs) for xs in zip(*lists)]


def _tri_inv(lows, n, paired=False):
    pair = paired or (2 * n == LANES and len(lows) % 2 == 0)
    if pair and not paired:
        lows = [jnp.concatenate(lows[j:j + 2], axis=1) for j in range(0, len(lows), 2)]
    width = 2 * n if pair else n
    r = lax.broadcasted_iota(jnp.int32, (n, width), 0)
    c = lax.broadcasted_iota(jnp.int32, (n, width), 1) % n
    eye = (r == c).astype(F32)
    mul = lambda x, y: _mm(x, _per_head(y, 2 if pair else 1))
    bs = min(TRI_BLOCK, n)
    if n > bs:
        same = (r // bs) == (c // bs)
        diags = [jnp.where(same, low, 0.0) for low in lows]
        rests = _each(lambda low, d: low - d, lows, diags)
    else:
        diags, rests = lows, None
    invs = [eye - d for d in diags]
    pws, p = _each(lambda x: mul(x, x), diags), 2
    while 2 * p < bs:
        both = _each(lambda x, i: mul(jnp.concatenate([x, i], axis=0), x), pws, invs)
        pws = [z[:n] for z in both]
        invs = _each(lambda i, z: i + z[n:], invs, both)
        p *= 2
    invs = _each(lambda i, x: i + mul(i, x), invs, pws)
    if rests is not None:
        nils = _each(mul, invs, rests)
        outs = [eye - x for x in nils]
        for _ in range(n // bs - 2):
            outs = _each(lambda x, o: eye - mul(x, o), nils, outs)
        invs = _each(mul, outs, invs)
    if pair and not paired:
        invs = [half for x in invs for half in (x[:, :n], x[:, n:])]
    return invs


def _head_groups(chunk):
    hp = 2 if chunk == HEAD_DIM else 1
    return hp, hp * HEAD_DIM, hp * chunk, N_HEADS // hp


def _own_blocks(x, hp):
    if hp == 1:
        return x
    r = lax.broadcasted_iota(jnp.int32, x.shape, 0) // HEAD_DIM
    c = lax.broadcasted_iota(jnp.int32, x.shape, 1) // HEAD_DIM
    return jnp.where(r == c, x, 0.0)


def _own_side(x, hp):
    if hp == 1:
        return x
    low = lax.broadcasted_iota(jnp.int32, (HEAD_DIM, x.shape[1]), 1) < HEAD_DIM
    return jnp.where(low, x[:HEAD_DIM], x[HEAD_DIM:])


def _carry_chunks(s_scr, bb, n_chunks, hp, out0, left, decay, pb, q_mat):
    n_grp = N_HEADS // hp
    group = bb * n_grp
    state = [s_scr[bi, g] if hp == 1 else jnp.concatenate([s_scr[bi, g * hp + i] for i in range(hp)], axis=1)
             for bi in range(bb) for g in range(n_grp)]
    out = {}
    for c in range(n_chunks):
        grp = slice(c * group, (c + 1) * group)
        outs = _each(lambda o_, l_, s_: o_ + _mm_nt(l_, _per_head(s_, hp)), out0[grp], left[grp], state)
        state = _each(lambda s_, d_, p_, q_: s_ * d_ - _mm(s_, p_) + q_, state, decay[grp], pb[grp], q_mat[grp])
        for bi in range(bb):
            out[bi, c] = jnp.concatenate(outs[bi * n_grp:(bi + 1) * n_grp], axis=1)
    for bi in range(bb):
        for g in range(n_grp):
            for i in range(hp):
                s_scr[bi, g * hp + i] = state[bi * n_grp + g][:, i * HEAD_DIM:(i + 1) * HEAD_DIM]
    pieces = [out[bi, c] for bi in range(bb) for c in range(n_chunks)]
    return pieces[0] if len(pieces) == 1 else jnp.concatenate(pieces, axis=0)


def _const_spec(shape):
    zeros = (0,) * len(shape)
    return pl.BlockSpec(shape, lambda *_: zeros, pipeline_mode=pl.Buffered(1))


def _in_proj_kernel(x_ref, g_ref, w_ref, *out_refs):
    h = _rms(x_ref[...], g_ref[...]).astype(BF16)
    off = 0
    for o_ref, width in zip(out_refs, PROJ_WIDTHS):
        o_ref[...] = _mm(h, w_ref[:, off:off + width])
        off += width


def _in_proj(x2d, g, w_all, tm):
    t = x2d.shape[0]
    wtot = sum(PROJ_WIDTHS)
    return pl.pallas_call(
        _in_proj_kernel,
        grid=(t // tm,),
        in_specs=[pl.BlockSpec((tm, D_MODEL), lambda i: (i, 0)),
                  _const_spec((1, D_MODEL)),
                  _const_spec((D_MODEL, wtot))],
        out_specs=[pl.BlockSpec((tm, w), lambda i: (i, 0)) for w in PROJ_WIDTHS],
        out_shape=[jax.ShapeDtypeStruct((t, w), F32) for w in PROJ_WIDTHS],
        compiler_params=pltpu.CompilerParams(dimension_semantics=("arbitrary",), vmem_limit_bytes=VMEM_LIMIT),
        name="in_proj",
    )(x2d, g, w_all)


FF_CHUNK = 1024


def _post_kernel(x_ref, o0_ref, o1_ref, o2_ref, o3_ref, wout_ref, gpost_ref, gpre_ref, wup_ref, wdn_ref,
                 gmlp_ref, y_ref):
    mix = None
    for i, o_ref in enumerate((o0_ref, o1_ref, o2_ref, o3_ref)):
        part = _mm(o_ref[...].astype(BF16), wout_ref[i * GROUP_WIDTH:(i + 1) * GROUP_WIDTH, :])
        mix = part if mix is None else mix + part
    x1 = x_ref[...] + _rms(mix, gpost_ref[...])
    h = _rms(x1, gpre_ref[...]).astype(BF16)
    acc = None
    for c in range(D_FF // FF_CHUNK):
        u = _mm(h, wup_ref[:, c * FF_CHUNK:(c + 1) * FF_CHUNK])
        u = jnp.square(jnp.maximum(u, 0.0)).astype(BF16)
        part = _mm(u, wdn_ref[c * FF_CHUNK:(c + 1) * FF_CHUNK, :])
        acc = part if acc is None else acc + part
    y_ref[...] = x1 + _rms(acc, gmlp_ref[...])


def _post(x2d, outs, w_out, g_post, g_pre, w_up, w_dn, g_mlp, tm):
    t = x2d.shape[0]
    row = lambda w: pl.BlockSpec((tm, w), lambda i: (i, 0))
    return pl.pallas_call(
        _post_kernel,
        grid=(t // tm,),
        in_specs=[row(D_MODEL)] + [row(GROUP_WIDTH)] * 4 + [
            _const_spec((D_MODEL, D_MODEL)), _const_spec((1, D_MODEL)), _const_spec((1, D_MODEL)),
            _const_spec((D_MODEL, D_FF)), _const_spec((D_FF, D_MODEL)), _const_spec((1, D_MODEL))],
        out_specs=row(D_MODEL),
        out_shape=jax.ShapeDtypeStruct((t, D_MODEL), F32),
        compiler_params=pltpu.CompilerParams(dimension_semantics=("arbitrary",), vmem_limit_bytes=VMEM_LIMIT),
        name="post_mlp",
    )(x2d, *outs, w_out, g_post, g_pre, w_up, w_dn, g_mlp)


def _chunk_sums(x, chunk, rows):
    _, _, _, incl = _tri_masks(chunk)
    tri = incl.astype(BF16)
    cums = [_mm_exact_lhs(tri, x[c * chunk:(c + 1) * chunk]) for c in range(rows // chunk)]
    if len(cums) == 1:
        return cums[0], cums[0][chunk - 1:chunk, :]
    tots = [jnp.broadcast_to(cu[chunk - 1:chunk, :], cu.shape) for cu in cums]
    return jnp.concatenate(cums, axis=0), jnp.concatenate(tots, axis=0)


def _rwkv_kernel(p_ref, shift_ref, s0_ref, mu_ref, w0_ref, a0_ref, kkp_ref, kap_ref, rk_ref, lnw_ref, lnb_ref,
                 w2_ref, a2_ref, g2_ref, o_ref, s_out_ref, s_scr, carry_scr, *, bb, chunk, n_chunks, n_valid):
    c_idx = pl.program_id(1)
    rows = chunk * n_chunks
    total = bb * rows

    @pl.when(c_idx == 0)
    def _():
        s_scr[...] = s0_ref[...]
        carry_scr[...] = shift_ref[...]

    first = lax.broadcasted_iota(jnp.int32, (rows, 1), 0) == 0
    last = rows - chunk + n_valid
    prevs = []
    for bi in range(bb):
        p_b = p_ref[bi]
        prevs.append(jnp.where(first, carry_scr[bi], pltpu.roll(p_b, 1, 0)))
        carry_scr[bi] = p_b[last - 1:last, :]
    p = p_ref[...].reshape(total, RW_PROJ)
    prev = prevs[0] if bb == 1 else jnp.concatenate(prevs, axis=0)
    row = lax.broadcasted_iota(jnp.int32, (total, 1), 0)
    xs = p + (prev - p) * mu_ref[...]
    r = xs[:, 0:GROUP_WIDTH]
    k = xs[:, GROUP_WIDTH:2 * GROUP_WIDTH]
    v = xs[:, 2 * GROUP_WIDTH:3 * GROUP_WIDTH]
    lora = xs[:, RW_LORA_OFF:RW_PROJ]

    z_w = w0_ref[...] + _mm(jnp.tanh(lora), w2_ref[...])
    logw = -math.exp(-0.5) * jax.nn.sigmoid(z_w)
    a = jax.nn.sigmoid(a0_ref[...] + _mm(lora, a2_ref[...]))
    gate = _mm(jax.nn.sigmoid(lora), g2_ref[...])

    ones = _head_ones(GROUP_WIDTH)
    kk = k * kkp_ref[...]
    kk = kk * lax.rsqrt(jnp.maximum(_head_sum(kk * kk, ones), 1e-12))
    k2 = k * (1.0 + (a - 1.0) * kap_ref[...])
    b = kk * a
    if n_valid < chunk:
        valid = (row % chunk) < n_valid
        logw = jnp.where(valid, logw, 0.0)
        kk = jnp.where(valid, kk, 0.0)
        b = jnp.where(valid, b, 0.0)
        k2v = jnp.where(valid, k2, 0.0)
        vv = jnp.where(valid, v, 0.0)
    else:
        k2v, vv = k2, v

    log_cum, log_end = _chunk_sums(logw, chunk, total)
    w_inv = jnp.exp(-log_cum)
    w_end = jnp.exp(log_end)
    w_tail = w_end * w_inv
    a_bar = kk * jnp.exp(log_cum - logw)
    b_bar = b * w_inv
    k_bar = k2v * w_inv
    r_bar = r * jnp.exp(log_cum)
    b_end = b * w_tail
    k_end = k2v * w_tail

    hp, wid, swid, n_grp = _head_groups(chunk)
    sr = lax.broadcasted_iota(jnp.int32, (chunk, swid), 0)
    sc = lax.broadcasted_iota(jnp.int32, (chunk, swid), 1) % chunk
    strict, incl = sr > sc, sr >= sc
    tiles = [(slice(bi * rows + c * chunk, bi * rows + (c + 1) * chunk), slice(g * wid, (g + 1) * wid))
             for c in range(n_chunks) for bi in range(bb) for g in range(n_grp)]
    cut = lambda x: [x[rs, sl] for rs, sl in tiles]
    ab, rb, bs, kb, be, ke, vh = map(cut, (a_bar, r_bar, b_bar, k_bar, b_end, k_end, vv))
    ar = _each(lambda a_, r_: jnp.concatenate([a_, r_], axis=0), ab, rb)
    gb = _each(lambda x, b_: _mm_nt(x, _per_head(b_, hp)), ar, bs)
    gk = _each(lambda x, k_: _mm_nt(x, _per_head(k_, hp)), ar, kb)
    l_ab = [jnp.where(strict, g[:chunk], 0.0) for g in gb]
    m_rb = [jnp.where(incl, g[chunk:], 0.0) for g in gb]
    l_ak = [jnp.where(strict, g[:chunk], 0.0) for g in gk]
    m_rk = [jnp.where(incl, g[chunk:], 0.0) for g in gk]
    tinv = _tri_inv(l_ab, chunk, paired=True) if hp == 2 else _tri_inv(l_ab, chunk)
    side = lambda x, y_: jnp.concatenate([_per_head(x, hp), _per_head(y_, hp)], axis=1)
    lv = _each(lambda l_, v_: _mm(l_, _per_head(v_, hp)), l_ak, vh)
    tt = _each(lambda t_, a_, l_: _mm(t_, side(a_, l_)), tinv, ab, lv)
    ta = [t_[:, :wid] for t_ in tt]
    c0 = [-t_[:, wid:] for t_ in tt]
    pq = _each(lambda a_, c_, b_: _mm_tn(jnp.concatenate([a_, c_], axis=1), b_), ta, c0, be)
    vk = _each(_mm_tn, vh, ke)
    ry = _each(lambda m_, a_, c_: _mm(m_, side(a_, c_)), m_rb, ta, c0)
    rkv = _each(lambda m_, v_: _mm(m_, _per_head(v_, hp)), m_rk, vh)
    pb = [_own_blocks(x[:wid], hp) for x in pq]
    q_mat = _each(lambda x, y_: _own_side(x[wid:] + y_, hp), pq, vk)
    rr = _each(lambda r_, x: r_ - x[:, :wid], rb, ry)
    y0 = _each(lambda x, y_: x[:, wid:] + y_, ry, rkv)
    we = [w_end[rs.start:rs.start + 1, sl] for rs, sl in tiles]
    y = _carry_chunks(s_scr, bb, n_chunks, hp, y0, rr, we, pb, q_mat)

    inv_d = 1.0 / HEAD_DIM
    mean = _head_sum(y, ones) * inv_d
    d = y - mean
    var = _head_sum(d * d, ones) * inv_d
    yn = d * lax.rsqrt(var + RW_LN_EPS) * lnw_ref[...] + lnb_ref[...]
    bonus = _head_sum(r * k2 * rk_ref[...], ones) * v
    o_ref[...] = ((yn + bonus) * gate).reshape(bb, rows, GROUP_WIDTH)

    @pl.when(c_idx == pl.num_programs(1) - 1)
    def _():
        s_out_ref[...] = s_scr[...]


def _state_specs(bb, layer):
    shape = (bb, N_HEADS, HEAD_DIM, HEAD_DIM)
    return (pl.BlockSpec((None,) + shape, lambda b, c: (layer, b, 0, 0, 0)),
            pl.BlockSpec(shape, lambda b, c: (b, 0, 0, 0)))


def _rwkv(p, shift, s0, layer, prm, bb, chunk, n_chunks, n_valid):
    bsz, seq, _ = p.shape
    rows = chunk * n_chunks
    assert n_valid == chunk or n_chunks == 1
    vec = _const_spec((1, GROUP_WIDTH))
    mat = _const_spec((GROUP_WIDTH, GROUP_WIDTH))
    state_in, state = _state_specs(bb, layer)
    return pl.pallas_call(
        functools.partial(_rwkv_kernel, bb=bb, chunk=chunk, n_chunks=n_chunks, n_valid=n_valid),
        grid=(bsz // bb, seq // rows),
        in_specs=[pl.BlockSpec((bb, rows, RW_PROJ), lambda b, c: (b, c, 0)),
                  pl.BlockSpec((bb, 1, RW_PROJ), lambda b, c: (b, 0, 0)),
                  state_in, _const_spec((1, RW_PROJ))] + [vec] * 7 + [mat] * 3,
        out_specs=[pl.BlockSpec((bb, rows, GROUP_WIDTH), lambda b, c: (b, c, 0)), state],
        out_shape=[jax.ShapeDtypeStruct((bsz, seq, GROUP_WIDTH), F32),
                   jax.ShapeDtypeStruct(s0.shape[1:], F32)],
        scratch_shapes=[pltpu.VMEM((bb, N_HEADS, HEAD_DIM, HEAD_DIM), F32), pltpu.VMEM((bb, 1, RW_PROJ), F32)],
        compiler_params=pltpu.CompilerParams(dimension_semantics=("arbitrary", "arbitrary")),
        name="rwkv7",
    )(p, shift, s0, prm["mu"], prm["w0"], prm["a0"], prm["kk"], prm["ka"], prm["rk"], prm["ln_w"], prm["ln_b"],
      prm["w2"], prm["a2"], prm["g2"])


def _gdn_kernel(qkv_ref, z_ref, ba_ref, buf_ref, s0_ref, convw_ref, alog_ref, dtb_ref, nw_ref,
                o_ref, s_out_ref, s_scr, x_scr, *, bb, chunk, n_chunks, n_valid):
    c_idx = pl.program_id(1)
    rows = chunk * n_chunks
    total = bb * rows
    pad = SUBLANES

    @pl.when(c_idx == 0)
    def _():
        s_scr[...] = s0_ref[...]
        x_scr[:, 0:pad, :] = jnp.zeros((bb, pad, GDN_QKV), F32)
        x_scr[:, pad - (GDN_CONV - 1):pad, :] = buf_ref[...]

    last = rows - chunk + n_valid
    convs = []
    for bi in range(bb):
        x_scr[bi, pad:pad + rows, :] = qkv_ref[bi]
        conv = None
        for j in range(GDN_CONV):
            term = x_scr[bi, pad - j:pad - j + rows, :] * convw_ref[GDN_CONV - 1 - j:GDN_CONV - j, :]
            conv = term if conv is None else conv + term
        x_scr[bi, 0:pad, :] = x_scr[bi, last:last + pad, :]
        convs.append(conv)
    act = _silu(convs[0] if bb == 1 else jnp.concatenate(convs, axis=0))
    q = act[:, 0:GROUP_WIDTH]
    k = act[:, GROUP_WIDTH:2 * GROUP_WIDTH]
    v = act[:, 2 * GROUP_WIDTH:3 * GROUP_WIDTH]
    ones = _head_ones(GROUP_WIDTH)
    q = q * lax.rsqrt(jnp.maximum(_head_sum(q * q, ones), 1e-12)) * (HEAD_DIM ** -0.5)
    k = k * lax.rsqrt(jnp.maximum(_head_sum(k * k, ones), 1e-12))

    ba = ba_ref[...].reshape(total, LANES)
    row = lax.broadcasted_iota(jnp.int32, (total, LANES), 0)
    beta = jax.nn.sigmoid(ba)
    glog = -jnp.exp(alog_ref[...]) * _softplus(ba + dtb_ref[...])
    if n_valid < chunk:
        valid = (row % chunk) < n_valid
        beta = jnp.where(valid, beta, 0.0)
        glog = jnp.where(valid, glog, 0.0)
    gcum, gend = _chunk_sums(glog, chunk, total)
    gcum_t = gcum.T
    er = lax.broadcasted_iota(jnp.int32, (LANES, GROUP_WIDTH), 0)
    ec = lax.broadcasted_iota(jnp.int32, (LANES, GROUP_WIDTH), 1) // HEAD_DIM
    beta_w = _mm_exact_rhs(beta, (er == ec).astype(BF16))
    spread = (er == ec + N_HEADS).astype(BF16)
    gcum_w = _mm_exact_rhs(gcum, spread)
    gend_w = _mm_exact_rhs(gend, spread)
    gam_w = jnp.exp(gcum_w)
    bgk = beta_w * gam_w * k
    bv = beta_w * v
    gq = gam_w * q
    kt = k * jnp.exp(gend_w - gcum_w)
    end_w = jnp.exp(gend_w)

    hp, wid, swid, n_grp = _head_groups(chunk)
    sr = lax.broadcasted_iota(jnp.int32, (chunk, swid), 0)
    sc = lax.broadcasted_iota(jnp.int32, (chunk, swid), 1) % chunk
    strict, incl = sr > sc, sr >= sc
    tiles = [(slice(bi * rows + c * chunk, bi * rows + (c + 1) * chunk), g)
             for c in range(n_chunks) for bi in range(bb) for g in range(n_grp)]
    lanes = lambda x: [x[rs, g * wid:(g + 1) * wid] for rs, g in tiles]
    slab = lambda x: [jnp.concatenate([x[rs, h * HEAD_DIM:h * HEAD_DIM + chunk] for h in range(g * hp, (g + 1) * hp)],
                                      axis=1) if hp * chunk != wid else x[rs, g * wid:(g + 1) * wid]
                      for rs, g in tiles]
    k_g, q_g, bgk_g, bv_g, gq_g, kt_g = map(lanes, (k, q, bgk, bv, gq, kt))
    g_row = [jnp.concatenate([gcum_t[N_HEADS + h:N_HEADS + h + 1, rs] for h in range(g * hp, (g + 1) * hp)], axis=1)
             for rs, g in tiles]
    decay = _each(lambda gc, gr: jnp.where(incl, jnp.exp(jnp.where(incl, gc - gr, 0.0)), 0.0), slab(gcum_w), g_row)
    gram = _each(lambda k_, q_: _mm_nt(jnp.concatenate([k_, q_], axis=0), _per_head(k_, hp)), k_g, q_g)
    a_mat = _each(lambda b_, d_, g_: jnp.where(strict, b_ * d_ * g_[:chunk], 0.0), slab(beta_w), decay, gram)
    tinv = _tri_inv(a_mat, chunk, paired=True) if hp == 2 else _tri_inv(a_mat, chunk)
    side = lambda x, y_: jnp.concatenate([_per_head(x, hp), _per_head(y_, hp)], axis=1)
    tt = _each(lambda t_, x, y_: _mm(t_, side(x, y_)), tinv, bgk_g, bv_g)
    qo = _each(lambda g_, d_, t_: _mm(g_[chunk:] * d_, side(t_[:, :wid], t_[:, wid:])), gram, decay, tt)
    pq = _each(_mm_tn, tt, kt_g)
    qq = _each(lambda g_, x: g_ - x[:, :wid], gq_g, qo)
    o0 = [x[:, wid:] for x in qo]
    pb = [_own_blocks(x[:wid], hp) for x in pq]
    q_mat = [_own_side(x[wid:], hp) for x in pq]
    ge = [end_w[rs.start:rs.start + 1, g * wid:(g + 1) * wid] for rs, g in tiles]
    o = _carry_chunks(s_scr, bb, n_chunks, hp, o0, qq, ge, pb, q_mat)
    ms = _head_sum(o * o, ones) * (1.0 / HEAD_DIM)
    zz = z_ref[...].reshape(total, GROUP_WIDTH)
    o_ref[...] = (o * lax.rsqrt(ms + NORM_EPS) * nw_ref[...] * _silu(zz)).reshape(bb, rows, GROUP_WIDTH)

    @pl.when(c_idx == pl.num_programs(1) - 1)
    def _():
        s_out_ref[...] = s_scr[...]


def _gdn(qkv, z, ba, buf, s0, layer, prm, bb, chunk, n_chunks, n_valid):
    bsz, seq, _ = qkv.shape
    rows = chunk * n_chunks
    assert n_valid == chunk or n_chunks == 1
    state_in, state = _state_specs(bb, layer)
    blk = lambda w: pl.BlockSpec((bb, rows, w), lambda b, c: (b, c, 0))
    return pl.pallas_call(
        functools.partial(_gdn_kernel, bb=bb, chunk=chunk, n_chunks=n_chunks, n_valid=n_valid),
        grid=(bsz // bb, seq // rows),
        in_specs=[blk(GDN_QKV), blk(GROUP_WIDTH), blk(LANES),
                  pl.BlockSpec((bb, GDN_CONV - 1, GDN_QKV), lambda b, c: (b, 0, 0)), state_in,
                  _const_spec((GDN_CONV, GDN_QKV)), _const_spec((1, LANES)), _const_spec((1, LANES)),
                  _const_spec((1, GROUP_WIDTH))],
        out_specs=[blk(GROUP_WIDTH), state],
        out_shape=[jax.ShapeDtypeStruct((bsz, seq, GROUP_WIDTH), F32), jax.ShapeDtypeStruct(s0.shape[1:], F32)],
        scratch_shapes=[pltpu.VMEM((bb, N_HEADS, HEAD_DIM, HEAD_DIM), F32),
                        pltpu.VMEM((bb, SUBLANES + rows, GDN_QKV), F32)],
        compiler_params=pltpu.CompilerParams(dimension_semantics=("arbitrary", "arbitrary")),
        name="gdn",
    )(qkv, z, ba, buf, s0, prm["conv_w"], prm["a_log"], prm["dt_bias"], prm["norm_w"])


def _gelu_tanh(x):
    return 0.5 * x * (1.0 + jnp.tanh(math.sqrt(2.0 / math.pi) * (x + 0.044715 * (x * x * x))))


def _s5_kernel(u_ref, x0re_ref, x0im_ref, are_ref, aim_ref, ldt_ref, bre_ref, bim_ref, cre_ref, cim_ref, d_ref,
               wglu_ref, bglu_ref, o_ref, hre_out_ref, him_out_ref, hre_scr, him_scr, sre_scr, sim_scr, *, steps, rows):
    c_idx = pl.program_id(1)

    @pl.when(c_idx == 0)
    def _():
        sre_scr[...] = x0re_ref[...]
        sim_scr[...] = x0im_ref[...]

    a_re, a_im = are_ref[...], aim_ref[...]
    dt = jnp.exp(ldt_ref[...])
    mag = jnp.exp(dt * a_re)
    ab_re = mag * jnp.cos(dt * a_im)
    ab_im = mag * jnp.sin(dt * a_im)
    den = a_re * a_re + a_im * a_im
    nr = ab_re - 1.0
    cf_re = (nr * a_re + ab_im * a_im) / den
    cf_im = (ab_im * a_re - nr * a_im) / den

    u = jnp.swapaxes(u_ref[...], 0, 1).reshape(steps * rows, GROUP_WIDTH)
    ub = u.astype(BF16)
    b_re, b_im = bre_ref[...], bim_ref[...]
    hre_scr[...] = _mm(ub, cf_re * b_re - cf_im * b_im)
    him_scr[...] = _mm(ub, cf_re * b_im + cf_im * b_re)

    abr = jnp.broadcast_to(ab_re, (rows, S5_LANES))
    abi = jnp.broadcast_to(ab_im, (rows, S5_LANES))

    def step(t, carry):
        h_re, h_im = carry
        base = pl.multiple_of(t * rows, rows)
        n_re = abr * h_re - abi * h_im + hre_scr[pl.ds(base, rows), :]
        n_im = abr * h_im + abi * h_re + him_scr[pl.ds(base, rows), :]
        hre_scr[pl.ds(base, rows), :] = n_re
        him_scr[pl.ds(base, rows), :] = n_im
        return n_re, n_im

    h_re, h_im = lax.fori_loop(0, steps, step, (sre_scr[...], sim_scr[...]))
    sre_scr[...] = h_re
    sim_scr[...] = h_im

    y = (_mm(hre_scr[...].astype(BF16), cre_ref[...]) - _mm(him_scr[...].astype(BF16), cim_ref[...])
         + d_ref[...] * u)
    zz = _gelu_tanh(y)
    out = zz * jax.nn.sigmoid(_mm(zz.astype(BF16), wglu_ref[...]) + bglu_ref[...])
    o_ref[...] = jnp.swapaxes(out.reshape(steps, rows, GROUP_WIDTH), 0, 1)

    @pl.when(c_idx == pl.num_programs(1) - 1)
    def _():
        hre_out_ref[...] = h_re
        him_out_ref[...] = h_im


def _s5(u, x0_re, x0_im, prm, steps, rows):
    bsz, seq, _ = u.shape
    st = pl.BlockSpec((rows, S5_LANES), lambda b, c: (b, 0))
    vec = _const_spec((1, S5_LANES))
    return pl.pallas_call(
        functools.partial(_s5_kernel, steps=steps, rows=rows),
        grid=(bsz // rows, seq // steps),
        in_specs=[pl.BlockSpec((rows, steps, GROUP_WIDTH), lambda b, c: (b, c, 0)), st, st, vec, vec, vec,
                  _const_spec((GROUP_WIDTH, S5_LANES)), _const_spec((GROUP_WIDTH, S5_LANES)),
                  _const_spec((S5_LANES, GROUP_WIDTH)), _const_spec((S5_LANES, GROUP_WIDTH)),
                  _const_spec((1, GROUP_WIDTH)), _const_spec((GROUP_WIDTH, GROUP_WIDTH)),
                  _const_spec((1, GROUP_WIDTH))],
        out_specs=[pl.BlockSpec((rows, steps, GROUP_WIDTH), lambda b, c: (b, c, 0)), st, st],
        out_shape=[jax.ShapeDtypeStruct((bsz, seq, GROUP_WIDTH), F32),
                   jax.ShapeDtypeStruct((bsz, S5_LANES), F32), jax.ShapeDtypeStruct((bsz, S5_LANES), F32)],
        scratch_shapes=[pltpu.VMEM((steps * rows, S5_LANES), F32), pltpu.VMEM((steps * rows, S5_LANES), F32),
                        pltpu.VMEM((rows, S5_LANES), F32), pltpu.VMEM((rows, S5_LANES), F32)],
        compiler_params=pltpu.CompilerParams(dimension_semantics=("arbitrary", "arbitrary"),
                                             vmem_limit_bytes=VMEM_LIMIT),
        name="s5",
    )(u, x0_re, x0_im, prm["a_re"], prm["a_im"], prm["log_dt"], prm["b_re"], prm["b_im"], prm["c_re"],
      prm["c_im"], prm["d"], prm["w_glu"], prm["b_glu"])


def _rope(x, cos, sin, width):
    fwd = pltpu.roll(x, width - ROPE_DIM // 2, 1)
    bwd = pltpu.roll(x, ROPE_DIM // 2, 1)
    lane = lax.broadcasted_iota(jnp.int32, x.shape, 1) % HEAD_DIM
    return x * cos + jnp.where(lane < ROPE_DIM // 2, fwd, bwd) * sin


def _swa_kernel(p_ref, cos_ref, sin_ref, kbuf_ref, vbuf_ref, sink_ref, o_ref, krot_ref, k_scr, v_scr,
                *, bb, qb, start, carry):
    j = pl.program_id(1)
    wb = WINDOW

    @pl.when(j == 0)
    def _():
        k_scr[:, 0:wb, :] = kbuf_ref[...]
        v_scr[:, 0:wb, :] = vbuf_ref[...]
        if qb < wb:
            k_scr[:, wb:2 * wb, :] = jnp.zeros((bb, wb, SWA_KV_WIDTH), F32)
            v_scr[:, wb:2 * wb, :] = jnp.zeros((bb, wb, SWA_KV_WIDTH), F32)

    p = p_ref[...].reshape(bb * qb, SWA_PROJ)
    cos, sin = cos_ref[...], sin_ref[...]
    if bb > 1:
        cos, sin = jnp.concatenate([cos] * bb, axis=0), jnp.concatenate([sin] * bb, axis=0)
    q = _rope(p[:, 0:GROUP_WIDTH], cos, sin, GROUP_WIDTH)
    k = _rope(p[:, GROUP_WIDTH:GROUP_WIDTH + SWA_KV_WIDTH], cos[:, 0:SWA_KV_WIDTH], sin[:, 0:SWA_KV_WIDTH],
              SWA_KV_WIDTH)
    v = p[:, GROUP_WIDTH + SWA_KV_WIDTH:SWA_PROJ]
    krot_ref[...] = k.reshape(bb, qb, SWA_KV_WIDTH)
    for bi in range(bb):
        k_scr[bi, wb:wb + qb, :] = k[bi * qb:(bi + 1) * qb]
        v_scr[bi, wb:wb + qb, :] = v[bi * qb:(bi + 1) * qb]

    nk = 2 * wb
    rq = lax.broadcasted_iota(jnp.int32, (2 * qb, nk), 0) % qb
    ck = lax.broadcasted_iota(jnp.int32, (2 * qb, nk), 1)
    rel = rq + wb - ck
    kpos = start + j * qb - wb + ck
    valid = (rel >= 0) & (rel < WINDOW) & (kpos >= 0)
    sink_col = ck == 0
    upper = lax.broadcasted_iota(jnp.int32, (2 * qb, nk), 0) < qb
    sinks = sink_ref[...]
    v_lane = lax.broadcasted_iota(jnp.int32, (nk, SWA_KV_WIDTH), 1)
    v_live = lax.broadcasted_iota(jnp.int32, (nk, SWA_KV_WIDTH), 0) > 0
    low, high = v_live & (v_lane < HEAD_DIM), v_live & (v_lane >= HEAD_DIM)
    ones = jnp.ones((nk, SWA_KV_WIDTH), BF16)

    probs = [(bi, h) for bi in range(bb) for h in range(SWA_KV_HEADS)]
    q2 = [jnp.concatenate([q[bi * qb:(bi + 1) * qb, (2 * h) * HEAD_DIM:(2 * h + 1) * HEAD_DIM],
                           q[bi * qb:(bi + 1) * qb, (2 * h + 1) * HEAD_DIM:(2 * h + 2) * HEAD_DIM]], axis=0)
          for bi, h in probs]
    kh = [k_scr[bi, :, h * HEAD_DIM:(h + 1) * HEAD_DIM] for bi, h in probs]
    vals = [v_scr[bi] for bi in range(bb)]
    swapped = [pltpu.roll(x, HEAD_DIM, 1) for x in vals]
    v_lo = [jnp.where(low, vals[bi] if h == 0 else swapped[bi], 0.0) for bi, h in probs]
    v_hi = [jnp.where(high, swapped[bi] if h == 0 else vals[bi], 0.0) for bi, h in probs]
    sink = [jnp.where(upper, sinks[:, 2 * h:2 * h + 1], sinks[:, 2 * h + 1:2 * h + 2]) for _, h in probs]
    sc = _each(lambda q_, k_, z_: jnp.where(sink_col, z_, jnp.where(valid, _mm_nt(q_, k_) * (HEAD_DIM ** -0.5),
                                                                     -jnp.inf)), q2, kh, sink)
    ex = _each(lambda s_: jnp.exp(s_ - jnp.max(s_, axis=-1, keepdims=True)).astype(BF16), sc)
    den = _each(lambda e_: _mm(e_, ones), ex)
    o_lo = _each(lambda e_, v_: _mm(e_[:qb], v_), ex, v_lo)
    o_hi = _each(lambda e_, v_: _mm(e_[qb:], v_), ex, v_hi)
    res = _each(lambda a_, b_, d_: a_ / d_[:qb] + b_ / d_[qb:], o_lo, o_hi, den)
    rows_out = [jnp.concatenate(res[bi * SWA_KV_HEADS:(bi + 1) * SWA_KV_HEADS], axis=1) for bi in range(bb)]
    o_ref[...] = (rows_out[0] if bb == 1 else jnp.concatenate(rows_out, axis=0)).reshape(bb, qb, GROUP_WIDTH)

    if carry:
        for bi in range(bb):
            k_scr[bi, 0:wb, :] = k[bi * qb:(bi + 1) * qb]
            v_scr[bi, 0:wb, :] = v[bi * qb:(bi + 1) * qb]


def _swa(p, cos, sin, kbuf, vbuf, sinks, bb, qb, start):
    bsz, seq, _ = p.shape
    nb = seq // qb
    assert nb == 1 or qb == WINDOW
    blk = lambda w: pl.BlockSpec((bb, qb, w), lambda b, j: (b, j, 0))
    tab = pl.BlockSpec((qb, GROUP_WIDTH), lambda b, j: (j, 0))
    buf = pl.BlockSpec((bb, WINDOW, SWA_KV_WIDTH), lambda b, j: (b, 0, 0))
    return pl.pallas_call(
        functools.partial(_swa_kernel, bb=bb, qb=qb, start=start, carry=nb > 1),
        grid=(bsz // bb, nb),
        in_specs=[blk(SWA_PROJ), tab, tab, buf, buf, _const_spec((1, LANES))],
        out_specs=[blk(GROUP_WIDTH), pl.BlockSpec((bb, qb, SWA_KV_WIDTH), lambda b, j: (b, 0, 0))],
        out_shape=[jax.ShapeDtypeStruct((bsz, seq, GROUP_WIDTH), F32),
                   jax.ShapeDtypeStruct((bsz, qb, SWA_KV_WIDTH), F32)],
        scratch_shapes=[pltpu.VMEM((bb, 2 * WINDOW, SWA_KV_WIDTH), F32),
                        pltpu.VMEM((bb, 2 * WINDOW, SWA_KV_WIDTH), F32)],
        compiler_params=pltpu.CompilerParams(dimension_semantics=("arbitrary", "arbitrary")),
        name="swa",
    )(p, cos, sin, kbuf, vbuf, sinks)


def _rope_tables(start, seq):
    half = ROPE_DIM // 2
    inv = ROPE_THETA ** (-jnp.arange(0, ROPE_DIM, 2, dtype=F32) / ROPE_DIM)
    ang = (start + jnp.arange(seq)).astype(F32)[:, None] * inv[None, :]
    cos, sin = jnp.cos(ang), jnp.sin(ang)
    rest = HEAD_DIM - ROPE_DIM
    cos_h = jnp.concatenate([cos, cos, jnp.ones((seq, rest), F32)], axis=1)
    sin_h = jnp.concatenate([-sin, sin, jnp.zeros((seq, rest), F32)], axis=1)
    return jnp.tile(cos_h, (1, N_HEADS)), jnp.tile(sin_h, (1, N_HEADS))


def _pad_rows(w, top, total):
    return jnp.pad(w, ((top, total - top - w.shape[0]), (0, 0)))


def _block_diag_in(b):
    eye = jnp.eye(S5_GROUPS, dtype=b.dtype)
    return jnp.einsum("gnc,gh->gchn", b, eye).reshape(S5_GROUPS * S5_CH, S5_GROUPS * S5_STATE)


def _block_diag_out(c):
    eye = jnp.eye(S5_GROUPS, dtype=c.dtype)
    return jnp.einsum("gcn,gh->gnhc", c, eye).reshape(S5_GROUPS * S5_STATE, S5_GROUPS * S5_CH)


def _layer_params(l, g_mix_pre, g_mix_post, g_mlp_pre, g_mlp_post, w_in, w_out, rw_mu, rw_w0, rw_w2, rw_a0, rw_a2,
                  rw_g2, rw_kk, rw_ka, rw_rk, rw_ln_w, rw_ln_b, s5_a_re, s5_a_im, s5_log_dt, s5_b_re, s5_b_im,
                  s5_c_re, s5_c_im, s5_d, s5_w_glu, s5_b_glu, gdn_conv_w, gdn_a_log, gdn_dt_bias, gdn_norm_w,
                  swa_sinks, w_up, w_down):
    row = lambda a: a[l].astype(F32).reshape(1, -1)
    wi = w_in[l].astype(F32)
    o_s5 = RW_PROJ
    o_gdn = o_s5 + GROUP_WIDTH
    o_ba = o_gdn + GDN_QKV
    o_z = o_ba + 2 * N_HEADS
    o_swa = o_z + GROUP_WIDTH
    w_all = jnp.concatenate([
        wi[:, :o_s5], wi[:, o_s5:o_gdn], wi[:, o_gdn:o_ba], wi[:, o_z:o_swa],
        jnp.pad(wi[:, o_ba:o_z], ((0, 0), (0, LANES - 2 * N_HEADS))), wi[:, o_swa:]], axis=1).astype(BF16)
    lane_pad = lambda a: jnp.pad(a[l].astype(F32), (N_HEADS, LANES - 2 * N_HEADS)).reshape(1, LANES)
    return {
        "g_mix_pre": row(g_mix_pre), "g_mix_post": row(g_mix_post), "g_mlp_pre": row(g_mlp_pre),
        "g_mlp_post": row(g_mlp_post), "w_all": w_all, "w_out": w_out[l].astype(BF16),
        "w_up": w_up[l].astype(BF16), "w_down": w_down[l].astype(BF16),
        "rw": {"mu": row(rw_mu), "w0": row(rw_w0), "a0": row(rw_a0), "kk": row(rw_kk), "ka": row(rw_ka),
               "rk": row(rw_rk), "ln_w": row(rw_ln_w), "ln_b": row(rw_ln_b),
               "w2": _pad_rows(rw_w2[l].astype(F32), 0, GROUP_WIDTH).astype(BF16),
               "a2": _pad_rows(rw_a2[l].astype(F32), RW_DECAY_RANK, GROUP_WIDTH).astype(BF16),
               "g2": _pad_rows(rw_g2[l].astype(F32), RW_DECAY_RANK + RW_ICLR_RANK, GROUP_WIDTH).astype(BF16)},
        "s5": {"a_re": row(s5_a_re), "a_im": row(s5_a_im),
               "log_dt": jnp.repeat(s5_log_dt[l].astype(F32), S5_STATE).reshape(1, S5_LANES),
               "b_re": _block_diag_in(s5_b_re[l].astype(F32)),
               "b_im": _block_diag_in(s5_b_im[l].astype(F32)),
               "c_re": _block_diag_out(s5_c_re[l].astype(F32)).astype(BF16),
               "c_im": _block_diag_out(s5_c_im[l].astype(F32)).astype(BF16),
               "d": row(s5_d), "w_glu": s5_w_glu[l].astype(BF16), "b_glu": row(s5_b_glu)},
        "gdn": {"conv_w": gdn_conv_w[l].astype(F32), "a_log": lane_pad(gdn_a_log), "dt_bias": lane_pad(gdn_dt_bias),
                "norm_w": jnp.tile(gdn_norm_w[l].astype(F32), N_HEADS).reshape(1, GROUP_WIDTH)},
        "sinks": jnp.pad(swa_sinks[l].astype(F32), (0, LANES - N_HEADS)).reshape(1, LANES),
    }


def _layer(x, start, st, layer, prm):
    rw_s, rw_shift, s5_re, s5_im, gdn_s, gdn_conv, swa_k, swa_v = st
    bsz, seq, _ = x.shape
    tokens = bsz * seq
    tm = min(TOKEN_ROWS, tokens)
    chunk = CHUNK if seq % CHUNK == 0 else -(-seq // SUBLANES) * SUBLANES
    seq_pad = -(-seq // chunk) * chunk
    n_valid = chunk - (seq_pad - seq)
    n_chunks = math.gcd(seq_pad // chunk, MIX_CHUNKS)
    bb = {name: math.gcd(bsz, pair[seq_pad == chunk]) for name, pair in MIXER_BATCH.items()}
    assert seq_pad == seq or seq_pad == chunk
    assert seq >= GDN_CONV - 1

    x2d = x.reshape(tokens, D_MODEL)
    p_rw, p_s5, p_qkv, p_z, p_ba, p_swa = (
        a.reshape(bsz, seq, -1) for a in _in_proj(x2d, prm["g_mix_pre"], prm["w_all"], tm))
    pad = lambda a: a if seq_pad == seq else jnp.pad(a, ((0, 0), (0, seq_pad - seq), (0, 0)))

    o_rw, rw_s_new = _rwkv(pad(p_rw), rw_shift, rw_s, layer, prm["rw"], bb["rwkv"], chunk, n_chunks, n_valid)
    rw_shift_new = p_rw[:, seq - 1:, :]

    rows = 16
    steps = min(seq, 64)
    o_s5, s5_re_new, s5_im_new = _s5(p_s5, s5_re.reshape(bsz, S5_LANES), s5_im.reshape(bsz, S5_LANES), prm["s5"],
                                     steps, rows)

    o_gdn, gdn_s_new = _gdn(pad(p_qkv), pad(p_z), pad(p_ba), gdn_conv, gdn_s, layer, prm["gdn"],
                            bb["gdn"], chunk, n_chunks, n_valid)
    gdn_conv_new = p_qkv[:, seq - (GDN_CONV - 1):, :]

    qb = WINDOW if seq % WINDOW == 0 else seq_pad
    cos, sin = _rope_tables(start, seq_pad)
    wbuf = swa_k.shape[1]
    o_swa, k_rot = _swa(pad(p_swa), cos, sin, swa_k.reshape(bsz, wbuf, SWA_KV_WIDTH),
                        swa_v.reshape(bsz, wbuf, SWA_KV_WIDTH), prm["sinks"], bb["swa"], qb, start)
    k_new = k_rot[:, :qb - (seq_pad - seq), :] if seq_pad != seq else k_rot
    v_new = p_swa[:, seq - min(seq, wbuf):, GROUP_WIDTH + SWA_KV_WIDTH:]
    kv_shape = (bsz, -1, SWA_KV_HEADS, HEAD_DIM)
    if k_new.shape[1] >= wbuf:
        swa_k_new = k_new[:, -wbuf:].reshape(kv_shape)
        swa_v_new = v_new[:, -wbuf:].reshape(kv_shape)
    else:
        swa_k_new = jnp.concatenate([swa_k, k_new.reshape(kv_shape)], axis=1)[:, -wbuf:]
        swa_v_new = jnp.concatenate([swa_v, v_new.reshape(kv_shape)], axis=1)[:, -wbuf:]

    flat = lambda a: a[:, :seq, :].reshape(tokens, GROUP_WIDTH)
    y = _post(x2d, (flat(o_rw), flat(o_s5), flat(o_gdn), flat(o_swa)), prm["w_out"], prm["g_mix_post"],
              prm["g_mlp_pre"], prm["w_up"], prm["w_down"], prm["g_mlp_post"], tm)
    new_state = (rw_s_new, rw_shift_new, s5_re_new.reshape(s5_re.shape), s5_im_new.reshape(s5_im.shape),
                 gdn_s_new, gdn_conv_new, swa_k_new, swa_v_new)
    return y.reshape(bsz, seq, D_MODEL), new_state


def kernel(x_prompt, x_sample, state_rwkv, state_rwkv_shift, state_s5_re, state_s5_im, state_gdn, state_gdn_conv,
           cache_swa_k, cache_swa_v, g_mix_pre, g_mix_post, g_mlp_pre, g_mlp_post, w_in, w_out, rw_mu, rw_w0, rw_w2,
           rw_a0, rw_a2, rw_g2, rw_kk, rw_ka, rw_rk, rw_ln_w, rw_ln_b, s5_a_re, s5_a_im, s5_log_dt, s5_b_re, s5_b_im,
           s5_c_re, s5_c_im, s5_d, s5_w_glu, s5_b_glu, gdn_conv_w, gdn_a_log, gdn_dt_bias, gdn_norm_w, swa_sinks,
           w_up, w_down):
    weights = (g_mix_pre, g_mix_post, g_mlp_pre, g_mlp_post, w_in, w_out, rw_mu, rw_w0, rw_w2, rw_a0, rw_a2, rw_g2,
               rw_kk, rw_ka, rw_rk, rw_ln_w, rw_ln_b, s5_a_re, s5_a_im, s5_log_dt, s5_b_re, s5_b_im, s5_c_re, s5_c_im,
               s5_d, s5_w_glu, s5_b_glu, gdn_conv_w, gdn_a_log, gdn_dt_bias, gdn_norm_w, swa_sinks, w_up, w_down)
    depth = w_in.shape[0]
    bp = x_prompt.shape[0]
    xp = x_prompt.astype(F32)
    xs = x_sample.astype(F32)
    zp = lambda *shape: jnp.zeros((bp,) + shape, F32)
    zero_s = jnp.zeros((1, bp, N_HEADS, HEAD_DIM, HEAD_DIM), F32)
    prompt_init = (zero_s, zp(1, RW_PROJ), zp(S5_GROUPS, S5_STATE), zp(S5_GROUPS, S5_STATE),
                   zero_s, zp(GDN_CONV - 1, GDN_QKV),
                   zp(WINDOW, SWA_KV_HEADS, HEAD_DIM), zp(WINDOW, SWA_KV_HEADS, HEAD_DIM))
    sample_states = (state_rwkv, state_rwkv_shift, state_s5_re, state_s5_im, state_gdn, state_gdn_conv,
                     cache_swa_k, cache_swa_v)
    new_p, new_s = [], []
    for l in range(depth):
        prm = _layer_params(l, *weights)
        xp, st_p = _layer(xp, 0, prompt_init, 0, prm)
        stacked = (0, 4)
        st_l = tuple(a.astype(F32) if i in stacked else a[l].astype(F32) for i, a in enumerate(sample_states))
        xs, st_s = _layer(xs, PAST_LEN, st_l, l, prm)
        new_p.append(st_p)
        new_s.append(st_s)
    outs = [xp.astype(x_prompt.dtype), xs.astype(x_sample.dtype)]
    for i in range(len(sample_states)):
        outs.append(jnp.stack([st[i] for st in new_p], axis=0))
        outs.append(jnp.stack([st[i] for st in new_s], axis=0))
    return tuple(outs)
```
